```python
import math
import jax
import jax.numpy as jnp
from jax import lax
import numpy as np

D_MODEL = 1024
BATCH = 1
SEQ = 16384
DEPTH = 1
DEC_BATCH = 32
DEC_SEQ = 8
PAST_LEN = 16384
PAGE_SIZE = 128

HEAD_DIM = 64
NSA_HEADS = 8
NSA_KV_HEADS = 2
NSA_GROUP = NSA_HEADS // NSA_KV_HEADS
CMP_STRIDE = 16
CMP_BLOCK = 2 * CMP_STRIDE
CMP_HIDDEN = 64
SEL_BLOCK = 64
TOP_N = 16
WINDOW = 512
FORCED_SCORE = 1e4
DIFF_HEADS = 4
DIFF_VDIM = 2 * HEAD_DIM
N_GROUPS = 4
EXPERTS_PER_GROUP = 8
N_EXPERTS = N_GROUPS * EXPERTS_PER_GROUP
EXPERT_TOP_K = 2
EXPERT_FF = 512
MOE_BLOCK = 128
ROPE_THETA = 10000.0
EPS = 1e-6
QUERY_BLOCK = 128
IN_COLS = NSA_HEADS * HEAD_DIM + 6 * NSA_KV_HEADS * HEAD_DIM + 3 * NSA_HEADS + 3 * DIFF_HEADS * 2 * HEAD_DIM + 2 * D_MODEL

kernel_name = "nsa_diffattn_hier_moe_decode_step"


def _in_sizes():
    kv = NSA_KV_HEADS * HEAD_DIM
    return [NSA_HEADS * HEAD_DIM, kv, kv, kv, kv, kv, kv, 3 * NSA_HEADS,
            DIFF_HEADS * 2 * HEAD_DIM, DIFF_HEADS * 2 * HEAD_DIM, DIFF_HEADS * DIFF_VDIM, 2 * D_MODEL]


def rmsnorm(x, g):
    xf = x.astype(jnp.float32)
    y = xf * lax.rsqrt(jnp.mean(xf * xf, axis=-1, keepdims=True) + EPS)
    return (y * g.astype(jnp.float32)).astype(x.dtype)


def rope(x, pos):
    half = HEAD_DIM // 2
    inv = ROPE_THETA ** (-jnp.arange(half, dtype=jnp.float32) / half)
    ang = pos.astype(jnp.float32)[:, None] * inv[None, :]
    shape = (1, ang.shape[0]) + (1,) * (x.ndim - 3) + (half,)
    cos = jnp.cos(ang).reshape(shape)
    sin = jnp.sin(ang).reshape(shape)
    xf = x.astype(jnp.float32)
    x1, x2 = xf[..., :half], xf[..., half:]
    return jnp.concatenate([x1 * cos - x2 * sin, x2 * cos + x1 * sin], axis=-1).astype(x.dtype)


def masked_softmax(s, mask):
    s = jnp.where(mask, s.astype(jnp.float32), -1e30)
    m = jnp.max(s, axis=-1, keepdims=True)
    p = jnp.where(mask, jnp.exp(s - m), 0.0)
    return p / jnp.maximum(jnp.sum(p, axis=-1, keepdims=True), 1e-30)


def gather_pages(pool, page_table):
    g = pool[page_table]
    return g.reshape((page_table.shape[0], -1) + pool.shape[2:])


def project_mixer_inputs(h, pos, w_in):
    B, T, _ = h.shape
    sizes = _in_sizes()
    cuts = [sum(sizes[:i + 1]) for i in range(len(sizes) - 1)]
    qa, kc, vc, ks, vs, kw, vw, ga, qb, kb, vb, gm = jnp.split(h @ w_in, cuts, axis=-1)
    kvs = (B, T, NSA_KV_HEADS, HEAD_DIM)
    qa = qa.reshape(B, T, NSA_KV_HEADS, NSA_GROUP, HEAD_DIM)
    return (qa, rope(qa, pos), kc.reshape(kvs), vc.reshape(kvs),
            rope(ks.reshape(kvs), pos), vs.reshape(kvs),
            rope(kw.reshape(kvs), pos), vw.reshape(kvs),
            ga.reshape(B, T, NSA_KV_HEADS, NSA_GROUP, 3),
            rope(qb.reshape(B, T, DIFF_HEADS, 2, HEAD_DIM), pos),
            rope(kb.reshape(B, T, DIFF_HEADS, 2, HEAD_DIM), pos),
            vb.reshape(B, T, DIFF_HEADS, DIFF_VDIM),
            gm.reshape(B, T, 2, D_MODEL))


def compress(x, pos_enc, w1, w2):
    B, L, H, D = x.shape
    n_sub = L // CMP_STRIDE
    sub = x[:, :n_sub * CMP_STRIDE].reshape(B, n_sub, CMP_STRIDE, H, D)
    blocks = jnp.concatenate([sub[:, :-1], sub[:, 1:]], axis=2) + pos_enc[None, None, :, None, :]
    flat = blocks.transpose(0, 1, 3, 2, 4).reshape(B, n_sub - 1, H, CMP_BLOCK * D)
    return jax.nn.gelu(flat @ w1) @ w2


def to_sel_blocks(x):
    B, L, H, D = x.shape
    ns = -(-L // SEL_BLOCK)
    x = jnp.pad(x, ((0, 0), (0, ns * SEL_BLOCK - L), (0, 0), (0, 0)))
    return x.reshape(B, ns, SEL_BLOCK, H, D).transpose(0, 3, 1, 2, 4)


def slc_importance(p, n_sel):
    r = SEL_BLOCK // CMP_STRIDE
    nc = p.shape[-1]
    padded = jnp.pad(p, [(0, 0)] * (p.ndim - 1) + [(1, r * n_sel + r - nc - 1)])
    main = padded[..., :r * n_sel].reshape(p.shape[:-1] + (n_sel, r)).sum(-1)
    edge = padded[..., r::r][..., :n_sel]
    return main + edge


def nsa_core(q, q_rot, qpos, gates, k_cmp, v_cmp, k_sel_blk, v_sel_blk, k_win, v_win, win_pos):
    B, Tq = q.shape[:2]
    scale = HEAD_DIM ** -0.5
    nc = k_cmp.shape[1]
    cmp_end = jnp.arange(nc) * CMP_STRIDE + CMP_BLOCK - 1
    s = jnp.einsum('bqhgd,bnhd->bhgqn', q, k_cmp) * scale
    p_cmp = masked_softmax(s, cmp_end[None, :] <= qpos[:, None])
    o_cmp = jnp.einsum('bhgqn,bnhd->bqhgd', p_cmp.astype(v_cmp.dtype), v_cmp)
    ns = k_sel_blk.shape[2]
    n_pick = min(TOP_N, ns)
    imp = slc_importance(p_cmp.sum(2), ns)
    blk = jnp.arange(ns)[None, :]
    cur = (qpos // SEL_BLOCK)[:, None]
    forced = (blk == 0) | (blk == cur) | (blk == cur - 1)
    score = jnp.where(blk > cur, -1.0, jnp.where(forced, FORCED_SCORE, imp))
    _, idx = lax.top_k(score, n_pick)
    bi = jnp.arange(B)[:, None, None, None]
    hi = jnp.arange(NSA_KV_HEADS)[None, :, None, None]
    ks = k_sel_blk[bi, hi, idx]
    vs = v_sel_blk[bi, hi, idx]
    tok_pos = idx[..., None] * SEL_BLOCK + jnp.arange(SEL_BLOCK)
    s = jnp.einsum('bqhgd,bhqnsd->bhgqns', q_rot, ks) * scale
    smask = (tok_pos <= qpos[:, None, None])[:, :, None]
    m = n_pick * SEL_BLOCK
    p_sel = masked_softmax(s.reshape(B, NSA_KV_HEADS, NSA_GROUP, Tq, m),
                           smask.reshape(B, NSA_KV_HEADS, 1, Tq, m))
    o_sel = jnp.einsum('bhgqm,bhqmd->bqhgd', p_sel.astype(vs.dtype),
                       vs.reshape(B, NSA_KV_HEADS, Tq, m, HEAD_DIM))
    s = jnp.einsum('bqhgd,blhd->bhgql', q_rot, k_win) * scale
    wp = win_pos[None, :]
    wmask = (wp <= qpos[:, None]) & (wp >= qpos[:, None] - WINDOW) & (wp >= 0)
    p_win = masked_softmax(s, wmask)
    o_win = jnp.einsum('bhgql,blhd->bqhgd', p_win.astype(v_win.dtype), v_win)
    g = jax.nn.sigmoid(gates.astype(jnp.float32))
    o = g[..., 0:1] * o_cmp + g[..., 1:2] * o_sel + g[..., 2:3] * o_win
    return o.reshape(B, Tq, NSA_HEADS * HEAD_DIM).astype(q.dtype)


def nsa_prompt(qa, qa_rot, ga, k_cmp, v_cmp, ks, vs, kw, vw):
    B, L = qa.shape[:2]
    k_sel_blk = to_sel_blocks(ks)
    v_sel_blk = to_sel_blocks(vs)
    pad = ((0, 0), (WINDOW, 0), (0, 0), (0, 0))
    kw_p = jnp.pad(kw, pad)
    vw_p = jnp.pad(vw, pad)

    def one_block(i):
        s0 = i * QUERY_BLOCK
        sl = lambda a, n: lax.dynamic_slice_in_dim(a, s0, n, axis=1)
        qpos = s0 + jnp.arange(QUERY_BLOCK)
        wpos = s0 - WINDOW + jnp.arange(WINDOW + QUERY_BLOCK)
        return nsa_core(sl(qa, QUERY_BLOCK), sl(qa_rot, QUERY_BLOCK), qpos, sl(ga, QUERY_BLOCK),
                        k_cmp, v_cmp, k_sel_blk, v_sel_blk,
                        sl(kw_p, WINDOW + QUERY_BLOCK), sl(vw_p, WINDOW + QUERY_BLOCK), wpos)

    out = lax.map(one_block, jnp.arange(L // QUERY_BLOCK))
    return jnp.moveaxis(out, 0, 1).reshape(B, L, -1)


def diff_lambda_value(lam_vec, lam_init):
    lv = lam_vec.astype(jnp.float32)
    return jnp.exp(jnp.sum(lv[0] * lv[1])) - jnp.exp(jnp.sum(lv[2] * lv[3])) + lam_init


def diff_core(q, k, v, qpos, kpos, lam, subln_g, lam_init):
    B, Tq = q.shape[:2]
    s = jnp.einsum('bqhcd,bkhcd->bhcqk', q, k) * (HEAD_DIM ** -0.5)
    p = masked_softmax(s, kpos[None, :] <= qpos[:, None])
    a = p[:, :, 0] - lam * p[:, :, 1]
    o = jnp.einsum('bhqk,bkhd->bqhd', a.astype(v.dtype), v)
    o = rmsnorm(o, subln_g) * (1.0 - lam_init)
    return o.reshape(B, Tq, DIFF_HEADS * DIFF_VDIM)


def diff_prompt(qb, kb, vb, lam, subln_g, lam_init):
    B, L = qb.shape[:2]
    kpos = jnp.arange(L)

    def one_block(i):
        s0 = i * QUERY_BLOCK
        q = lax.dynamic_slice_in_dim(qb, s0, QUERY_BLOCK, axis=1)
        return diff_core(q, kb, vb, s0 + jnp.arange(QUERY_BLOCK), kpos, lam, subln_g, lam_init)

    out = lax.map(one_block, jnp.arange(L // QUERY_BLOCK))
    return jnp.moveaxis(out, 0, 1).reshape(B, L, -1)


def merge_out(o_a, o_b, gm, w_proj_a, w_proj_b, w_out):
    g = jax.nn.sigmoid(gm.astype(jnp.float32)).astype(o_a.dtype)
    return (g[:, :, 0] * (o_a @ w_proj_a) + g[:, :, 1] * (o_b @ w_proj_b)) @ w_out


def expert_dispatch(hf, experts, weights, w_gate, w_up, w_down):
    n_tok, D = hf.shape
    n_asg = n_tok * EXPERT_TOP_K
    flat_e = experts.reshape(-1)
    flat_t = jnp.broadcast_to(jnp.arange(n_tok, dtype=jnp.int32)[:, None], experts.shape).reshape(-1)
    flat_w = weights.reshape(-1)
    order = jnp.argsort(flat_e)
    e_sorted = flat_e[order]
    counts = jnp.bincount(flat_e, length=N_EXPERTS)
    start = jnp.cumsum(counts) - counts
    padded = (counts + MOE_BLOCK - 1) // MOE_BLOCK * MOE_BLOCK
    pend = jnp.cumsum(padded)
    pstart = pend - padded
    dest = pstart[e_sorted] + jnp.arange(n_asg) - start[e_sorted]
    n_blocks = -(-n_asg // MOE_BLOCK) + N_EXPERTS
    n_slots = n_blocks * MOE_BLOCK
    slot_tok = jnp.full((n_slots,), n_tok, jnp.int32).at[dest].set(flat_t[order])
    slot_w = jnp.zeros((n_slots,), hf.dtype).at[dest].set(flat_w[order].astype(hf.dtype))
    block_e = jnp.minimum(jnp.searchsorted(pend, jnp.arange(n_blocks) * MOE_BLOCK, side='right'), N_EXPERTS - 1)
    x_pad = jnp.concatenate([hf, jnp.zeros((1, D), hf.dtype)], axis=0)
    xs = x_pad[slot_tok].reshape(n_blocks, MOE_BLOCK, D)

    def run(args):
        xb, e = args
        return (jax.nn.silu(xb @ w_gate[e]) * (xb @ w_up[e])) @ w_down[e]

    ys = lax.map(run, (xs, block_e)).reshape(n_slots, D)
    return jnp.zeros((n_tok + 1, D), hf.dtype).at[slot_tok].add(ys * slot_w[:, None])[:n_tok]


def hier_moe(h, rg_w, rg_b, re_w, re_b, w_gate, w_up, w_down):
    B, T, D = h.shape
    hf = h.reshape(B * T, D)
    n_tok = B * T
    pg = jax.nn.softmax((hf @ rg_w + rg_b).astype(jnp.float32), axis=-1)
    p_grp, grp = lax.top_k(pg, 1)
    grp = grp[:, 0]
    le = (hf @ re_w + re_b).astype(jnp.float32).reshape(n_tok, N_GROUPS, EXPERTS_PER_GROUP)
    pe = jax.nn.softmax(le[jnp.arange(n_tok), grp], axis=-1)
    top_p, top_i = lax.top_k(pe, EXPERT_TOP_K)
    weights = p_grp * top_p / jnp.sum(top_p, axis=-1, keepdims=True)
    experts = grp[:, None] * EXPERTS_PER_GROUP + top_i
    return expert_dispatch(hf, experts, weights, w_gate, w_up, w_down).reshape(B, T, D)


def setup_inputs(seed: int = 0) -> dict:
    key = jax.random.key(seed)
    ks = jax.random.split(key, 32)
    n_pages = PAST_LEN // PAGE_SIZE
    n_used = DEC_BATCH * n_pages
    n_pool = n_used + max(1, n_used // 4)
    win_buf = min(WINDOW, PAST_LEN)
    f32 = jnp.float32

    def nrm(k, shape, scale=1.0):
        return jax.random.normal(k, shape, f32) * scale

    page_table = jax.random.permutation(ks[0], n_pool)[:n_used].reshape(DEC_BATCH, n_pages).astype(jnp.int32)
    wa = NSA_HEADS * HEAD_DIM
    wb = DIFF_HEADS * DIFF_VDIM
    return {
        "x_prompt": nrm(ks[1], (BATCH, SEQ, D_MODEL)),
        "x_sample": nrm(ks[2], (DEC_BATCH, DEC_SEQ, D_MODEL)),
        "cache_nsa_kv": nrm(ks[3], (DEPTH, n_pool, PAGE_SIZE, 4, NSA_KV_HEADS, HEAD_DIM)),
        "cache_diff_k": nrm(ks[4], (DEPTH, n_pool, PAGE_SIZE, DIFF_HEADS, 2, HEAD_DIM)),
        "cache_diff_v": nrm(ks[5], (DEPTH, n_pool, PAGE_SIZE, DIFF_HEADS, DIFF_VDIM)),
        "state_nsa_win_kv": nrm(ks[6], (DEPTH, DEC_BATCH, win_buf, 2, NSA_KV_HEADS, HEAD_DIM)),
        "page_table": page_table,
        "norm_mix_g": 1.0 + nrm(ks[7], (DEPTH, D_MODEL), 0.02),
        "w_in": nrm(ks[8], (DEPTH, D_MODEL, IN_COLS), D_MODEL ** -0.5),
        "nsa_cmp_pos": nrm(ks[9], (DEPTH, CMP_BLOCK, HEAD_DIM), 0.1),
        "nsa_cmp_k_w1": nrm(ks[10], (DEPTH, CMP_BLOCK * HEAD_DIM, CMP_HIDDEN), (CMP_BLOCK * HEAD_DIM) ** -0.5),
        "nsa_cmp_k_w2": nrm(ks[11], (DEPTH, CMP_HIDDEN, HEAD_DIM), CMP_HIDDEN ** -0.5),
        "nsa_cmp_v_w1": nrm(ks[12], (DEPTH, CMP_BLOCK * HEAD_DIM, CMP_HIDDEN), (CMP_BLOCK * HEAD_DIM) ** -0.5),
        "nsa_cmp_v_w2": nrm(ks[13], (DEPTH, CMP_HIDDEN, HEAD_DIM), CMP_HIDDEN ** -0.5),
        "diff_lambda": nrm(ks[14], (DEPTH, 4, HEAD_DIM), 0.1),
        "diff_subln_g": 1.0 + nrm(ks[15], (DEPTH, DIFF_VDIM), 0.02),
        "w_proj_a": nrm(ks[16], (DEPTH, wa, D_MODEL), wa ** -0.5),
        "w_proj_b": nrm(ks[17], (DEPTH, wb, D_MODEL), wb ** -0.5),
        "w_out": nrm(ks[18], (DEPTH, D_MODEL, D_MODEL), D_MODEL ** -0.5),
        "norm_ffn_g": 1.0 + nrm(ks[19], (DEPTH, D_MODEL), 0.02),
        "router_group_w": nrm(ks[20], (DEPTH, D_MODEL, N_GROUPS), D_MODEL ** -0.5),
        "router_group_b": nrm(ks[21], (DEPTH, N_GROUPS), 0.01),
        "router_expert_w": nrm(ks[22], (DEPTH, D_MODEL, N_EXPERTS), D_MODEL ** -0.5),
        "router_expert_b": nrm(ks[23], (DEPTH, N_EXPERTS), 0.01),
        "expert_w_gate": nrm(ks[24], (DEPTH, N_EXPERTS, D_MODEL, EXPERT_FF), D_MODEL ** -0.5),
        "expert_w_up": nrm(ks[25], (DEPTH, N_EXPERTS, D_MODEL, EXPERT_FF), D_MODEL ** -0.5),
        "expert_w_down": nrm(ks[26], (DEPTH, N_EXPERTS, EXPERT_FF, D_MODEL), EXPERT_FF ** -0.5),
        "norm_final_g": 1.0 + nrm(ks[27], (D_MODEL,), 0.02),
    }


def reference(x_prompt, x_sample, cache_nsa_kv, cache_diff_k, cache_diff_v, state_nsa_win_kv, page_table,
              norm_mix_g, w_in, nsa_cmp_pos, nsa_cmp_k_w1, nsa_cmp_k_w2, nsa_cmp_v_w1, nsa_cmp_v_w2,
              diff_lambda, diff_subln_g, w_proj_a, w_proj_b, w_out, norm_ffn_g,
              router_group_w, router_group_b, router_expert_w, router_expert_b,
              expert_w_gate, expert_w_up, expert_w_down, norm_final_g):
    L = x_prompt.shape[1]
    T = x_sample.shape[1]
    past = page_table.shape[1] * cache_nsa_kv.shape[2]
    win_buf = state_nsa_win_kv.shape[2]
    pos_p = jnp.arange(L)
    pos_s = past + jnp.arange(T)
    xp, xs = x_prompt, x_sample
    nsa_p, nsa_s, dk_p, dk_s, dv_p, dv_s, win_p, win_s = [], [], [], [], [], [], [], []
    for l in range(DEPTH):
        lam_init = 0.8 - 0.6 * math.exp(-0.3 * l)
        lam = diff_lambda_value(diff_lambda[l], lam_init)
        cmp_k = lambda a: compress(a, nsa_cmp_pos[l], nsa_cmp_k_w1[l], nsa_cmp_k_w2[l])
        cmp_v = lambda a: compress(a, nsa_cmp_pos[l], nsa_cmp_v_w1[l], nsa_cmp_v_w2[l])
        moe = lambda a: hier_moe(a, router_group_w[l], router_group_b[l], router_expert_w[l], router_expert_b[l],
                                 expert_w_gate[l], expert_w_up[l], expert_w_down[l])
        hp = rmsnorm(xp, norm_mix_g[l])
        qa, qa_r, kc, vc, ksl, vsl, kw, vw, ga, qb, kb, vb, gm = project_mixer_inputs(hp, pos_p, w_in[l])
        o_a = nsa_prompt(qa, qa_r, ga, cmp_k(kc), cmp_v(vc), ksl, vsl, kw, vw)
        o_b = diff_prompt(qb, kb, vb, lam, diff_subln_g[l], lam_init)
        xp = xp + merge_out(o_a, o_b, gm, w_proj_a[l], w_proj_b[l], w_out[l])
        xp = xp + moe(rmsnorm(xp, norm_ffn_g[l]))
        nsa_p.append(jnp.stack([kc, vc, ksl, vsl], axis=2))
        dk_p.append(kb)
        dv_p.append(vb)
        win_p.append(jnp.stack([kw, vw], axis=2)[:, -min(WINDOW, L):])
        hs = rmsnorm(xs, norm_mix_g[l])
        qa, qa_r, kc, vc, ksl, vsl, kw, vw, ga, qb, kb, vb, gm = project_mixer_inputs(hs, pos_s, w_in[l])
        past_nsa = gather_pages(cache_nsa_kv[l], page_table)
        kc_all = jnp.concatenate([past_nsa[:, :, 0], kc], axis=1)
        vc_all = jnp.concatenate([past_nsa[:, :, 1], vc], axis=1)
        ks_all = jnp.concatenate([past_nsa[:, :, 2], ksl], axis=1)
        vs_all = jnp.concatenate([past_nsa[:, :, 3], vsl], axis=1)
        win_all = jnp.concatenate([state_nsa_win_kv[l], jnp.stack([kw, vw], axis=2)], axis=1)
        win_pos = past - win_buf + jnp.arange(win_all.shape[1])
        o_a = nsa_core(qa, qa_r, pos_s, ga, cmp_k(kc_all), cmp_v(vc_all), to_sel_blocks(ks_all), to_sel_blocks(vs_all),
                       win_all[:, :, 0], win_all[:, :, 1], win_pos)
        kb_all = jnp.concatenate([gather_pages(cache_diff_k[l], page_table), kb], axis=1)
        vb_all = jnp.concatenate([gather_pages(cache_diff_v[l], page_table), vb], axis=1)
        o_b = diff_core(qb, kb_all, vb_all, pos_s, jnp.arange(past + T), lam, diff_subln_g[l], lam_init)
        xs = xs + merge_out(o_a, o_b, gm, w_proj_a[l], w_proj_b[l], w_out[l])
        xs = xs + moe(rmsnorm(xs, norm_ffn_g[l]))
        nsa_s.append(jnp.stack([kc, vc, ksl, vsl], axis=2))
        dk_s.append(kb)
        dv_s.append(vb)
        win_s.append(win_all[:, -min(WINDOW, past + T):])
    y_prompt = rmsnorm(xp, norm_final_g)
    y_sample = rmsnorm(xs, norm_final_g)
    return (y_prompt, y_sample, jnp.stack(nsa_p), jnp.stack(nsa_s), jnp.stack(dk_p), jnp.stack(dk_s),
            jnp.stack(dv_p), jnp.stack(dv_s), jnp.stack(win_p), jnp.stack(win_s))
```

```python
import functools
import math

import jax
import jax.numpy as jnp
from jax import lax
from jax.experimental import pallas as pl
from jax.experimental.pallas import tpu as pltpu

F32 = jnp.float32
BF16 = jnp.bfloat16

HEAD_DIM = 64
HALF = HEAD_DIM // 2
NSA_HEADS = 8
NSA_KV_HEADS = 2
NSA_GROUP = NSA_HEADS // NSA_KV_HEADS
CMP_STRIDE = 16
CMP_BLOCK = 2 * CMP_STRIDE
SEL_BLOCK = 64
SEL_PER_CMP = SEL_BLOCK // CMP_STRIDE
TOP_N = 16
WINDOW = 512
FORCED_SCORE = 1e4
DIFF_HEADS = 4
DIFF_VDIM = 2 * HEAD_DIM
N_GROUPS = 4
EXPERTS_PER_GROUP = 8
N_EXPERTS = N_GROUPS * EXPERTS_PER_GROUP
EXPERT_TOP_K = 2
ROPE_THETA = 10000.0
EPS = 1e-6
NEG = -1e30
SCALE = HEAD_DIM ** -0.5

LANES = 128
VMEM_LIMIT = 56 * 1024 * 1024

WA = NSA_HEADS * HEAD_DIM
KV = NSA_KV_HEADS * HEAD_DIM
WB = DIFF_HEADS * 2 * HEAD_DIM
WV = DIFF_HEADS * DIFF_VDIM
N_GATE = 3 * NSA_HEADS


def _cparams(sem):
    return pltpu.CompilerParams(dimension_semantics=sem, vmem_limit_bytes=VMEM_LIMIT)


def _full(shape):
    return pl.BlockSpec(shape, lambda *_: (0,) * len(shape))


def _swap_halves(t):
    lane = lax.broadcasted_iota(jnp.int32, t.shape, 1)
    fwd = pltpu.roll(t, LANES - HALF, axis=1)
    bwd = pltpu.roll(t, HALF, axis=1)
    return jnp.where(lane % HEAD_DIM < HALF, fwd, bwd)


def _rope_rows(t, cos, sin):
    outs = []
    for a in range(0, t.shape[1], LANES):
        x = t[:, a:a + LANES]
        outs.append(x * cos + _swap_halves(x) * sin)
    return outs[0] if len(outs) == 1 else jnp.concatenate(outs, axis=1)


def _rope_cols(t, cos, sin):
    outs = []
    for a in range(0, t.shape[0], HEAD_DIM):
        x1 = t[a:a + HALF]
        x2 = t[a + HALF:a + HEAD_DIM]
        outs.append(x1 * cos - x2 * sin)
        outs.append(x2 * cos + x1 * sin)
    return jnp.concatenate(outs, axis=0)


def _proj_kernel(x_ref, g_ref, wn_ref, wt_ref, cosn_ref, sinn_ref, cost_ref, sint_ref,
                 nsa_ref, win_ref, dk_ref, dv_ref, gm_ref, ksb_ref, kwb_ref, kbb_ref, *rest, transposed_q):
    x = x_ref[...]
    h = x * lax.rsqrt(jnp.mean(x * x, axis=-1, keepdims=True) + EPS) * g_ref[...]
    hb = h.astype(BF16)
    cosn = cosn_ref[...]
    sinn = sinn_ref[...]

    def mm(a, b):
        return jnp.dot(hb, wn_ref[:, a:b], preferred_element_type=F32)

    c = 0
    y = mm(c, c + 4 * KV)
    ks = _rope_rows(y[:, 2 * KV:3 * KV], cosn, sinn)
    nsa_ref[:, :2 * KV] = y[:, :2 * KV]
    nsa_ref[:, 2 * KV:3 * KV] = ks
    nsa_ref[:, 3 * KV:] = y[:, 3 * KV:]
    ksb_ref[...] = ks.astype(BF16)
    c += 4 * KV
    y = mm(c, c + 2 * KV)
    kw = _rope_rows(y[:, :KV], cosn, sinn)
    win_ref[:, :KV] = kw
    win_ref[:, KV:] = y[:, KV:]
    kwb_ref[...] = kw.astype(BF16)
    c += 2 * KV
    kb = _rope_rows(mm(c, c + WB), cosn, sinn)
    dk_ref[...] = kb
    kbb_ref[...] = kb.astype(BF16)
    c += WB
    dv_ref[...] = mm(c, c + WV)
    c += WV
    d_model = x.shape[1]
    gm_ref[...] = jax.nn.sigmoid(mm(c, c + 2 * d_model))
    c += 2 * d_model

    if transposed_q:
        qat_ref, qart_ref, qbt_ref, vst_ref, vwt_ref, vbt_ref, gat_ref = rest
        cost = cost_ref[...]
        sint = sint_ref[...]

        def mmt(a, b):
            return lax.dot_general(wt_ref[a:b, :], hb, (((1,), (1,)), ((), ())), preferred_element_type=F32)

        r = 0
        qa = mmt(r, r + WA) * SCALE
        qat_ref[...] = qa.astype(BF16)
        qart_ref[...] = _rope_cols(qa, cost, sint).astype(BF16)
        r += WA
        qbt_ref[...] = _rope_cols(mmt(r, r + WB) * SCALE, cost, sint).astype(BF16)
        r += WB
        vst_ref[...] = mmt(r, r + KV).astype(BF16)
        r += KV
        vwt_ref[...] = mmt(r, r + KV).astype(BF16)
        r += KV
        vbt_ref[...] = mmt(r, r + WV).astype(BF16)
        r += WV
        gat_ref[...] = jax.nn.sigmoid(mmt(r, r + 32))
    else:
        qa_ref, qar_ref, qb_ref, ga_ref = rest
        qa = mm(c, c + WA) * SCALE
        qa_ref[...] = qa.astype(BF16)
        qar_ref[...] = _rope_rows(qa, cosn, sinn).astype(BF16)
        c += WA
        qb_ref[...] = _rope_rows(mm(c, c + WB) * SCALE, cosn, sinn).astype(BF16)
        c += WB
        ga_ref[...] = jax.nn.sigmoid(mm(c, c + LANES))


def _split_w_in(w_in, d_model):
    sizes = [WA, KV, KV, KV, KV, KV, KV, N_GATE, WB, WB, WV, 2 * d_model]
    offs = [0]
    for s in sizes:
        offs.append(offs[-1] + s)
    names = ["qa", "kc", "vc", "ks", "vs", "kw", "vw", "ga", "qb", "kb", "vb", "gm"]
    return {n: w_in[:, offs[i]:offs[i + 1]] for i, n in enumerate(names)}


def _rope_tables(pos):
    inv = ROPE_THETA ** (-jnp.arange(HALF, dtype=F32) / HALF)
    ang = pos.astype(F32)[:, None] * inv[None, :]
    cos, sin = jnp.cos(ang), jnp.sin(ang)
    cosn = jnp.tile(cos, (1, LANES // HALF))
    sinn = jnp.tile(jnp.concatenate([-sin, sin], axis=1), (1, LANES // HEAD_DIM))
    return cosn, sinn, cos.T, sin.T


def _project(x, pos, norm_g, w, tm, transposed_q):
    n, d = x.shape
    assert n % tm == 0
    cosn, sinn, cost, sint = _rope_tables(pos)
    wn_parts = [w["kc"], w["vc"], w["ks"], w["vs"], w["kw"], w["vw"], w["kb"], w["vb"], w["gm"]]
    ga_pad = jnp.pad(w["ga"], ((0, 0), (0, LANES - N_GATE)))
    if transposed_q:
        wt = jnp.concatenate([w["qa"], w["qb"], w["vs"], w["vw"], w["vb"], ga_pad[:, :32]], axis=1).T.astype(BF16)
    else:
        wn_parts += [w["qa"], w["qb"], ga_pad]
        wt = jnp.zeros((8, d), BF16)
    wn = jnp.concatenate(wn_parts, axis=1).astype(BF16)

    row = lambda c: pl.BlockSpec((tm, c), lambda i: (i, 0))
    col = lambda r: pl.BlockSpec((r, tm), lambda i: (0, i))
    out_shape = [jax.ShapeDtypeStruct((n, 4 * KV), F32), jax.ShapeDtypeStruct((n, 2 * KV), F32),
                 jax.ShapeDtypeStruct((n, WB), F32), jax.ShapeDtypeStruct((n, WV), F32),
                 jax.ShapeDtypeStruct((n, 2 * d), F32), jax.ShapeDtypeStruct((n, KV), BF16),
                 jax.ShapeDtypeStruct((n, KV), BF16), jax.ShapeDtypeStruct((n, WB), BF16)]
    out_specs = [row(4 * KV), row(2 * KV), row(WB), row(WV), row(2 * d), row(KV), row(KV), row(WB)]
    if transposed_q:
        out_shape += [jax.ShapeDtypeStruct((WA, n), BF16), jax.ShapeDtypeStruct((WA, n), BF16),
                      jax.ShapeDtypeStruct((WB, n), BF16), jax.ShapeDtypeStruct((KV, n), BF16),
                      jax.ShapeDtypeStruct((KV, n), BF16), jax.ShapeDtypeStruct((WV, n), BF16),
                      jax.ShapeDtypeStruct((32, n), F32)]
        out_specs += [col(WA), col(WA), col(WB), col(KV), col(KV), col(WV), col(32)]
    else:
        out_shape += [jax.ShapeDtypeStruct((n, WA), BF16), jax.ShapeDtypeStruct((n, WA), BF16),
                      jax.ShapeDtypeStruct((n, WB), BF16), jax.ShapeDtypeStruct((n, LANES), F32)]
        out_specs += [row(WA), row(WA), row(WB), row(LANES)]
    return pl.pallas_call(
        functools.partial(_proj_kernel, transposed_q=transposed_q),
        grid=(n // tm,),
        in_specs=[row(d), _full((1, d)), _full(wn.shape), _full(wt.shape),
                  row(LANES), row(LANES), col(HALF), col(HALF)],
        out_specs=out_specs,
        out_shape=out_shape,
        compiler_params=_cparams(("parallel",)),
        name="proj",
    )(x, norm_g.reshape(1, d), wn, wt, cosn, sinn, cost, sint)


def _pad_head(qt, slot):
    z = jnp.zeros_like(qt)
    return jnp.concatenate([qt, z] if slot == 0 else [z, qt], axis=0)


def _flash_step(st, vt, m_prev, l_prev, acc_prev):
    m_new = jnp.maximum(m_prev, jnp.max(st, axis=0, keepdims=True))
    alpha = jnp.exp(m_prev - m_new)
    p = jnp.exp(st - m_new)
    l_new = alpha * l_prev + jnp.sum(p, axis=0, keepdims=True)
    acc_new = alpha * acc_prev + jnp.dot(vt, p.astype(BF16), preferred_element_type=F32)
    return m_new, l_new, acc_new


def _diff_lambda(lam_ref, lam_init):
    lv = lam_ref[...]
    a = jnp.sum(lv[0:1] * lv[1:2], axis=-1, keepdims=True)
    b = jnp.sum(lv[2:3] * lv[3:4], axis=-1, keepdims=True)
    return jnp.exp(a) - jnp.exp(b) + lam_init


def _diff_prompt_kernel(qt_ref, k_ref, vt_ref, lam_ref, g_ref, o_ref, m_ref, l_ref, acc_ref, *, tq, tk, lam_init):
    i = pl.program_id(0)
    j = pl.program_id(1)

    @pl.when(j == 0)
    def _():
        m_ref[...] = jnp.full_like(m_ref, NEG)
        l_ref[...] = jnp.zeros_like(l_ref)
        acc_ref[...] = jnp.zeros_like(acc_ref)

    @pl.when(j * tk < (i + 1) * tq)
    def _():
        kpos = j * tk + lax.broadcasted_iota(jnp.int32, (tk, tq), 0)
        qpos = i * tq + lax.broadcasted_iota(jnp.int32, (tk, tq), 1)
        bias = jnp.where(kpos <= qpos, 0.0, NEG)
        for h in range(DIFF_HEADS):
            k2 = k_ref[:, 2 * HEAD_DIM * h:2 * HEAD_DIM * (h + 1)]
            vt = vt_ref[DIFF_VDIM * h:DIFF_VDIM * (h + 1), :]
            for c in range(2):
                hc = 2 * h + c
                qt = _pad_head(qt_ref[HEAD_DIM * hc:HEAD_DIM * (hc + 1), :], c)
                st = jnp.dot(k2, qt, preferred_element_type=F32) + bias
                m, l, acc = _flash_step(st, vt, m_ref[hc:hc + 1], l_ref[hc:hc + 1], acc_ref[hc])
                m_ref[hc:hc + 1] = m
                l_ref[hc:hc + 1] = l
                acc_ref[hc] = acc

    @pl.when(j == pl.num_programs(1) - 1)
    def _():
        lam = _diff_lambda(lam_ref, lam_init)
        for h in range(DIFF_HEADS):
            o0 = acc_ref[2 * h] / l_ref[2 * h:2 * h + 1]
            o1 = acc_ref[2 * h + 1] / l_ref[2 * h + 1:2 * h + 2]
            a = o0 - lam * o1
            y = a * lax.rsqrt(jnp.mean(a * a, axis=0, keepdims=True) + EPS) * g_ref[...] * (1.0 - lam_init)
            o_ref[DIFF_VDIM * h:DIFF_VDIM * (h + 1), :] = y.astype(o_ref.dtype)


def _diff_prompt(qbt, kb, vbt, diff_lambda, subln_g, lam_init, tq, tk):
    n = kb.shape[0]
    assert n % tq == 0 and n % tk == 0 and tq % tk == 0
    last = lambda i, j: jnp.minimum(j, ((i + 1) * tq - 1) // tk)
    return pl.pallas_call(
        functools.partial(_diff_prompt_kernel, tq=tq, tk=tk, lam_init=lam_init),
        grid=(n // tq, n // tk),
        in_specs=[pl.BlockSpec((WB, tq), lambda i, j: (0, i)),
                  pl.BlockSpec((tk, WB), lambda i, j: (last(i, j), 0)),
                  pl.BlockSpec((WV, tk), lambda i, j: (0, last(i, j))),
                  _full(diff_lambda.shape), _full((DIFF_VDIM, 1))],
        out_specs=pl.BlockSpec((WV, tq), lambda i, j: (0, i)),
        out_shape=jax.ShapeDtypeStruct((WV, n), BF16),
        scratch_shapes=[pltpu.VMEM((2 * DIFF_HEADS, tq), F32), pltpu.VMEM((2 * DIFF_HEADS, tq), F32),
                        pltpu.VMEM((2 * DIFF_HEADS, DIFF_VDIM, tq), F32)],
        compiler_params=_cparams(("parallel", "arbitrary")),
        name="diff_prompt",
    )(qbt, kb, vbt, diff_lambda, subln_g.reshape(DIFF_VDIM, 1))


def _compress_weights(pos, k_w1, k_w2, v_w1, v_w2):
    hd = HEAD_DIM
    z = jnp.zeros((CMP_STRIDE, hd, hd), F32)

    def halves(w1):
        w3 = w1.reshape(CMP_BLOCK, hd, -1)
        return w3[:CMP_STRIDE], w3[CMP_STRIDE:]

    def expand(top, bot):
        rows = [[top, z, bot, z], [z, top, z, bot]]
        return jnp.concatenate([jnp.concatenate(r, axis=2) for r in rows], axis=1).astype(BF16)

    pf = pos.reshape(1, -1)
    ck, cv = pf @ k_w1, pf @ v_w1
    z2 = jnp.zeros((hd, hd), F32)
    w2k = jnp.block([[k_w2, z2], [z2, k_w2]]).astype(BF16)
    w2v = jnp.block([[v_w2, z2], [z2, v_w2]]).astype(BF16)
    return (expand(*halves(k_w1)), expand(*halves(v_w1)), jnp.concatenate([ck, ck], axis=1),
            jnp.concatenate([cv, cv], axis=1), w2k, w2v, w2v.T)


def _compress_ab(x_ref, w_ref, n_sub):
    acc = jnp.zeros((n_sub, 2 * KV), F32)
    for r in range(CMP_STRIDE):
        xr = x_ref[pl.ds(r, n_sub, stride=CMP_STRIDE), :].astype(BF16)
        acc += jnp.dot(xr, w_ref[r], preferred_element_type=F32)
    return acc


def _compress_hidden(ab, c):
    n_sub = ab.shape[0]
    nxt = pltpu.roll(ab[:, KV:], n_sub - 1, axis=0)
    return jax.nn.gelu(ab[:, :KV] + nxt + c).astype(BF16)


def _compress_ab_kernel(xk_ref, xv_ref, wk_ref, wv_ref, abk_ref, abv_ref, *, n_sub):
    abk_ref[...] = _compress_ab(xk_ref, wk_ref, n_sub)
    abv_ref[...] = _compress_ab(xv_ref, wv_ref, n_sub)


def _compress_mlp_kernel(abk_ref, abv_ref, ck_ref, cv_ref, w2k_ref, w2v_ref, w2vt_ref, kc_ref, vc_ref, vct_ref):
    gk = _compress_hidden(abk_ref[...], ck_ref[...])
    gv = _compress_hidden(abv_ref[...], cv_ref[...])
    kc_ref[...] = jnp.dot(gk, w2k_ref[...], preferred_element_type=F32).astype(BF16)
    vc_ref[...] = jnp.dot(gv, w2v_ref[...], preferred_element_type=F32).astype(BF16)
    vct_ref[...] = lax.dot_general(w2vt_ref[...], gv, (((1,), (1,)), ((), ())),
                                   preferred_element_type=F32).astype(BF16)


def _compress(kv, cw, sub_tile):
    w1k, w1v, ck, cv, w2k, w2v, w2vt = cw
    n_sub = kv.shape[0] // CMP_STRIDE
    assert n_sub % sub_tile == 0
    ab_shape = jax.ShapeDtypeStruct((n_sub, 2 * KV), F32)
    ab_spec = pl.BlockSpec((sub_tile, 2 * KV), lambda i: (i, 0))
    abk, abv = pl.pallas_call(
        functools.partial(_compress_ab_kernel, n_sub=sub_tile),
        grid=(n_sub // sub_tile,),
        in_specs=[pl.BlockSpec((sub_tile * CMP_STRIDE, KV), lambda i: (i, 0)),
                  pl.BlockSpec((sub_tile * CMP_STRIDE, KV), lambda i: (i, 1)), _full(w1k.shape), _full(w1v.shape)],
        out_specs=[ab_spec, ab_spec],
        out_shape=[ab_shape, ab_shape],
        compiler_params=_cparams(("parallel",)),
        name="compress_ab",
    )(kv, kv, w1k, w1v)
    return pl.pallas_call(
        _compress_mlp_kernel,
        out_shape=[jax.ShapeDtypeStruct((n_sub, KV), BF16), jax.ShapeDtypeStruct((n_sub, KV), BF16),
                   jax.ShapeDtypeStruct((KV, n_sub), BF16)],
        compiler_params=pltpu.CompilerParams(vmem_limit_bytes=VMEM_LIMIT),
        name="compress_mlp",
    )(abk, abv, ck, cv, w2k, w2v, w2vt)


def _importance_matrix(n_sel, n_cmp):
    j = jnp.arange(n_sel)[:, None]
    n = jnp.arange(n_cmp)[None, :]
    return ((n >= SEL_PER_CMP * j - 1) & (n <= SEL_PER_CMP * j + SEL_PER_CMP - 1)).astype(BF16)


def _split3(x):
    hi = x.astype(BF16)
    r = x - hi.astype(F32)
    mid = r.astype(BF16)
    lo = (r - mid.astype(F32)).astype(BF16)
    return hi, mid, lo


def _tile_lanes(x, k):
    return jnp.concatenate([x] * k, axis=1)


def _select_blocks(score, blk):
    big = jnp.int32(2 ** 30)

    def body(_, carry):
        sc, bias = carry
        mx = jnp.max(sc, axis=0, keepdims=True)
        first = jnp.min(jnp.where(sc == mx, blk, big), axis=0, keepdims=True)
        hit = blk == first
        return jnp.where(hit, -3e38, sc), jnp.where(hit, 0.0, bias)

    n_pick = min(TOP_N, score.shape[0])
    return lax.fori_loop(0, n_pick, body, (score, jnp.full(score.shape, NEG, F32)))[1]


def _softmax_cols(st, valid):
    sm = jnp.where(valid, st, NEG)
    m = jnp.max(sm, axis=0, keepdims=True)
    e = jnp.where(valid, jnp.exp(sm - m), 0.0)
    return e / jnp.maximum(jnp.sum(e, axis=0, keepdims=True), 1e-30)


def _nsa_prompt_kernel(qat_ref, qart_ref, gat_ref, kc_ref, vct_ref, imp_ref, ks_ref, vst_ref, kw_ref, vwt_ref,
                       o_ref, bias_ref, *, tq, tk, n_cmp):
    i = pl.program_id(0)
    g4 = NSA_GROUP
    ncp = kc_ref.shape[0]
    n_sel = imp_ref.shape[0]
    q0 = i * tq
    qlane = q0 + lax.broadcasted_iota(jnp.int32, (1, tq), 1)

    def heads_t(ref, h):
        return jnp.concatenate([ref[HEAD_DIM * (g4 * h + g):HEAD_DIM * (g4 * h + g + 1), :] for g in range(g4)],
                               axis=1)

    for h in range(NSA_KV_HEADS):
        qt = _pad_head(heads_t(qat_ref, h), h)
        st = jnp.dot(kc_ref[...], qt, preferred_element_type=F32)
        nrow = lax.broadcasted_iota(jnp.int32, (ncp, tq), 0)
        valid = (nrow * CMP_STRIDE + (CMP_BLOCK - 1) <= qlane) & (nrow < n_cmp)
        p = _softmax_cols(st, _tile_lanes(valid, g4))
        o_cmp = jnp.dot(vct_ref[HEAD_DIM * h:HEAD_DIM * (h + 1), :], p.astype(BF16), preferred_element_type=F32)
        psum = p[:, :tq]
        for g in range(1, g4):
            psum = psum + p[:, g * tq:(g + 1) * tq]
        imp = jnp.zeros((n_sel, tq), F32)
        for part in _split3(psum):
            imp += jnp.dot(imp_ref[...], part, preferred_element_type=F32)
        blk = lax.broadcasted_iota(jnp.int32, (n_sel, tq), 0)
        cur = qlane // SEL_BLOCK
        forced = (blk == 0) | (blk == cur) | (blk == cur - 1)
        score = jnp.where(blk > cur, -1.0, jnp.where(forced, FORCED_SCORE, imp))
        bias_ref[...] = _select_blocks(score, blk)

        qrt = _pad_head(heads_t(qart_ref, h), h)
        per_chunk = tk // SEL_BLOCK

        def chunk(c, carry):
            m, l, acc = carry
            k0 = pl.multiple_of(c * tk, tk)
            st = jnp.dot(ks_ref[pl.ds(k0, tk), :], qrt, preferred_element_type=F32)
            b8 = bias_ref[pl.ds(pl.multiple_of(c * per_chunk, per_chunk), per_chunk), :]
            bias = jnp.broadcast_to(b8[:, None, :], (per_chunk, SEL_BLOCK, tq)).reshape(tk, tq)
            kpos = k0 + lax.broadcasted_iota(jnp.int32, (tk, tq), 0)
            bias = bias + jnp.where(kpos <= qlane, 0.0, NEG)
            vt = vst_ref[HEAD_DIM * h:HEAD_DIM * (h + 1), pl.ds(k0, tk)]
            return _flash_step(st + _tile_lanes(bias, g4), vt, m, l, acc)

        n_chunks = (q0 + tq - 1) // tk + 1
        init = (jnp.full((1, g4 * tq), NEG, F32), jnp.zeros((1, g4 * tq), F32), jnp.zeros((HEAD_DIM, g4 * tq), F32))
        m, l, acc = lax.fori_loop(0, n_chunks, chunk, init)
        o_sel = acc / jnp.maximum(l, 1e-30)

        nw = WINDOW + tq
        w0 = pl.multiple_of(jnp.maximum(q0 - WINDOW, 0), LANES)
        st = jnp.dot(kw_ref[pl.ds(w0, nw), :], qrt, preferred_element_type=F32)
        kpos = w0 + lax.broadcasted_iota(jnp.int32, (nw, tq), 0)
        valid = (kpos <= qlane) & (kpos >= qlane - WINDOW)
        p = _softmax_cols(st, _tile_lanes(valid, g4))
        o_win = jnp.dot(vwt_ref[HEAD_DIM * h:HEAD_DIM * (h + 1), pl.ds(w0, nw)], p.astype(BF16),
                        preferred_element_type=F32)

        for g in range(g4):
            r = 3 * (g4 * h + g)
            sl = slice(g * tq, (g + 1) * tq)
            o = (gat_ref[r:r + 1, :] * o_cmp[:, sl] + gat_ref[r + 1:r + 2, :] * o_sel[:, sl]
                 + gat_ref[r + 2:r + 3, :] * o_win[:, sl])
            o_ref[HEAD_DIM * (g4 * h + g):HEAD_DIM * (g4 * h + g + 1), :] = o.astype(o_ref.dtype)


def _nsa_prompt(qat, qart, gat, kc, vct, ks, vst, kw, vwt, tq, tk):
    n = ks.shape[0]
    n_sub = kc.shape[0]
    n_cmp = n_sub - 1
    n_sel = n // SEL_BLOCK
    assert n % tq == 0 and n % tk == 0 and tk % SEL_BLOCK == 0 and n >= WINDOW + tq and tq % LANES == 0
    imp = _importance_matrix(n_sel, n_sub)
    col = lambda r: pl.BlockSpec((r, tq), lambda i: (0, i))
    return pl.pallas_call(
        functools.partial(_nsa_prompt_kernel, tq=tq, tk=tk, n_cmp=n_cmp),
        grid=(n // tq,),
        in_specs=[col(WA), col(WA), col(32), _full(kc.shape), _full(vct.shape), _full(imp.shape),
                  _full(ks.shape), _full(vst.shape), _full(kw.shape), _full(vwt.shape)],
        out_specs=col(WA),
        out_shape=jax.ShapeDtypeStruct((WA, n), BF16),
        scratch_shapes=[pltpu.VMEM((n_sel, tq), F32)],
        compiler_params=_cparams(("parallel",)),
        name="nsa_prompt",
    )(qat, qart, gat, kc, vct, imp, ks, vst, kw, vwt)


ROUTE_COLS = LANES


def _first_lane_of_max(v, lane):
    mx = jnp.max(v, axis=-1, keepdims=True)
    return mx, jnp.min(jnp.where(v == mx, lane, ROUTE_COLS), axis=-1, keepdims=True)


def _merge_kernel(oa_ref, ob_ref, gm_ref, x_ref, wa_ref, wb_ref, wo_ref, g_ref, rwh_ref, rwl_ref, rb_ref,
                  x1_ref, h2_ref, re_ref, rw_ref, *, transposed):
    d = x_ref.shape[1]
    dims = (((0,), (0,)), ((), ())) if transposed else (((1,), (0,)), ((), ()))
    ya = lax.dot_general(oa_ref[...], wa_ref[...], dims, preferred_element_type=F32)
    yb = lax.dot_general(ob_ref[...], wb_ref[...], dims, preferred_element_type=F32)
    mix = gm_ref[:, :d] * ya + gm_ref[:, d:] * yb
    x1 = x_ref[...] + jnp.dot(mix.astype(BF16), wo_ref[...], preferred_element_type=F32)
    x1_ref[...] = x1
    h2 = x1 * lax.rsqrt(jnp.mean(x1 * x1, axis=-1, keepdims=True) + EPS) * g_ref[...]
    h2_ref[...] = h2.astype(BF16)

    hi = h2.astype(BF16)
    lo = (h2 - hi.astype(F32)).astype(BF16)
    logits = (jnp.dot(hi, rwh_ref[...], preferred_element_type=F32)
              + jnp.dot(lo, rwh_ref[...], preferred_element_type=F32)
              + jnp.dot(hi, rwl_ref[...], preferred_element_type=F32)) + rb_ref[...]
    lane = lax.broadcasted_iota(jnp.int32, logits.shape, 1)
    is_g = lane < N_GROUPS
    gl = jnp.where(is_g, logits, NEG)
    gmx, grp = _first_lane_of_max(gl, lane)
    p_grp = 1.0 / jnp.sum(jnp.where(is_g, jnp.exp(gl - gmx), 0.0), axis=-1, keepdims=True)
    e_id = lane - N_GROUPS
    in_grp = (e_id >= 0) & (e_id < N_EXPERTS) & (e_id // EXPERTS_PER_GROUP == grp)
    el = jnp.where(in_grp, logits, NEG)
    emx = jnp.max(el, axis=-1, keepdims=True)
    ee = jnp.where(in_grp, jnp.exp(el - emx), -1.0)
    e1, i1 = _first_lane_of_max(ee, lane)
    e2, i2 = _first_lane_of_max(jnp.where(lane == i1, -1.0, ee), lane)
    inv = p_grp / (e1 + e2)
    re_ref[...] = jnp.where(lane == 0, i1 - N_GROUPS, jnp.where(lane == 1, i2 - N_GROUPS, 0))
    rw_ref[...] = jnp.where(lane == 0, e1 * inv, jnp.where(lane == 1, e2 * inv, 0.0))


def _merge(oa, ob, gm, x, wa, wb, wo, ffn_g, rwh, rwl, rb, tm, transposed):
    n, d = x.shape
    assert n % tm == 0
    row = lambda c: pl.BlockSpec((tm, c), lambda i: (i, 0))
    o_spec = pl.BlockSpec((WA, tm), lambda i: (0, i)) if transposed else row(WA)
    return pl.pallas_call(
        functools.partial(_merge_kernel, transposed=transposed),
        grid=(n // tm,),
        in_specs=[o_spec, o_spec, row(2 * d), row(d), _full(wa.shape), _full(wb.shape), _full(wo.shape),
                  _full((1, d)), _full(rwh.shape), _full(rwl.shape), _full((1, ROUTE_COLS))],
        out_specs=[row(d), row(d), row(ROUTE_COLS), row(ROUTE_COLS)],
        out_shape=[jax.ShapeDtypeStruct((n, d), F32), jax.ShapeDtypeStruct((n, d), BF16),
                   jax.ShapeDtypeStruct((n, ROUTE_COLS), jnp.int32), jax.ShapeDtypeStruct((n, ROUTE_COLS), F32)],
        compiler_params=_cparams(("parallel",)),
        name="merge_route",
    )(oa, ob, gm, x, wa, wb, wo, ffn_g.reshape(1, d), rwh, rwl, rb)


def _expert_kernel(be_ref, xs_ref, sw_ref, wg_ref, wu_ref, wd_ref, ys_ref):
    del be_ref
    xb = xs_ref[...]
    gate = jnp.dot(xb, wg_ref[0].astype(BF16), preferred_element_type=F32)
    up = jnp.dot(xb, wu_ref[0].astype(BF16), preferred_element_type=F32)
    act = (jax.nn.silu(gate) * up).astype(BF16)
    ys_ref[...] = jnp.dot(act, wd_ref[0].astype(BF16), preferred_element_type=F32) * sw_ref[...]


def _expert_ffn(block_e, xs, slot_w, w_gate, w_up, w_down, bm):
    n_slots, d = xs.shape
    ff = w_gate.shape[2]
    grid_spec = pltpu.PrefetchScalarGridSpec(
        num_scalar_prefetch=1,
        grid=(n_slots // bm,),
        in_specs=[pl.BlockSpec((bm, d), lambda b, be: (b, 0)),
                  pl.BlockSpec((bm, 1), lambda b, be: (b, 0)),
                  pl.BlockSpec((1, d, ff), lambda b, be: (be[b], 0, 0)),
                  pl.BlockSpec((1, d, ff), lambda b, be: (be[b], 0, 0)),
                  pl.BlockSpec((1, ff, d), lambda b, be: (be[b], 0, 0))],
        out_specs=pl.BlockSpec((bm, d), lambda b, be: (b, 0)),
    )
    return pl.pallas_call(
        _expert_kernel,
        grid_spec=grid_spec,
        out_shape=jax.ShapeDtypeStruct((n_slots, d), F32),
        compiler_params=_cparams(("arbitrary",)),
        name="expert_ffn",
    )(block_e, xs, slot_w, w_gate, w_up, w_down)


def _flash_rows(s, v, m_prev, l_prev, acc_prev):
    m_new = jnp.maximum(m_prev, jnp.max(s, axis=-1, keepdims=True))
    alpha = jnp.exp(m_prev - m_new)
    p = jnp.exp(s - m_new)
    l_new = alpha * l_prev + jnp.sum(p, axis=-1, keepdims=True)
    acc_new = alpha * acc_prev + jnp.dot(p.astype(BF16), v, preferred_element_type=F32)
    return m_new, l_new, acc_new


def _nt(a, b):
    return lax.dot_general(a, b, (((1,), (1,)), ((), ())), preferred_element_type=F32)


def _page_copies(pt_ref, first_page, n, src_hbm, col0, ncols, dst, sem, page):
    out = []
    for k in range(n):
        p0 = pl.multiple_of(pt_ref[first_page + k] * page, page)
        out.append(pltpu.make_async_copy(src_hbm.at[pl.ds(p0, page), pl.ds(col0, ncols)],
                                         dst.at[pl.ds(k * page, page), :], sem))
    return out


def _diff_sample_kernel(pt_ref, q_ref, kc_hbm, vc_hbm, kn_ref, vn_ref, lam_ref, g_ref, o_ref,
                        kbuf, vbuf, sem, m_ref, l_ref, acc_ref, *, page, ppc, n_chunks, lam_init):
    b = pl.program_id(0)
    c = pl.program_id(1)
    step = b * n_chunks + c
    total = pl.num_programs(0) * n_chunks
    t = q_ref.shape[1]

    def copies(s, slot):
        first = s * ppc
        return (_page_copies(pt_ref, first, ppc, kc_hbm, 0, WB, kbuf.at[slot], sem.at[0, slot], page)
                + _page_copies(pt_ref, first, ppc, vc_hbm, 0, WV, vbuf.at[slot], sem.at[1, slot], page))

    slot = step % 2

    @pl.when(step == 0)
    def _():
        for cp in copies(0, 0):
            cp.start()

    @pl.when(step + 1 < total)
    def _():
        for cp in copies(step + 1, 1 - slot):
            cp.start()

    @pl.when(c == 0)
    def _():
        m_ref[...] = jnp.full_like(m_ref, NEG)
        l_ref[...] = jnp.zeros_like(l_ref)
        acc_ref[...] = jnp.zeros_like(acc_ref)

    for cp in copies(step, slot):
        cp.wait()

    q = q_ref[0].astype(F32)
    lane = lax.broadcasted_iota(jnp.int32, (t, 2 * HEAD_DIM), 1)

    def q_pair(h):
        qh = q[:, 2 * HEAD_DIM * h:2 * HEAD_DIM * (h + 1)]
        return jnp.concatenate([jnp.where(lane < HEAD_DIM, qh, 0.0), jnp.where(lane >= HEAD_DIM, qh, 0.0)],
                               axis=0).astype(BF16)

    for h in range(DIFF_HEADS):
        k2 = kbuf[slot, :, 2 * HEAD_DIM * h:2 * HEAD_DIM * (h + 1)].astype(BF16)
        v2 = vbuf[slot, :, DIFF_VDIM * h:DIFF_VDIM * (h + 1)].astype(BF16)
        m, l, acc = _flash_rows(_nt(q_pair(h), k2), v2, m_ref[h], l_ref[h], acc_ref[h])
        m_ref[h] = m
        l_ref[h] = l
        acc_ref[h] = acc

    @pl.when(c == n_chunks - 1)
    def _():
        lam = _diff_lambda(lam_ref, lam_init)
        trow = lax.broadcasted_iota(jnp.int32, (2 * t, t), 0) % t
        tcol = lax.broadcasted_iota(jnp.int32, (2 * t, t), 1)
        for h in range(DIFF_HEADS):
            kn = kn_ref[0][:, 2 * HEAD_DIM * h:2 * HEAD_DIM * (h + 1)].astype(BF16)
            vn = vn_ref[0][:, DIFF_VDIM * h:DIFF_VDIM * (h + 1)].astype(BF16)
            s = jnp.where(tcol <= trow, _nt(q_pair(h), kn), NEG)
            m, l, acc = _flash_rows(s, vn, m_ref[h], l_ref[h], acc_ref[h])
            o = acc / l
            a = o[:t] - lam * o[t:]
            y = a * lax.rsqrt(jnp.mean(a * a, axis=-1, keepdims=True) + EPS) * g_ref[...] * (1.0 - lam_init)
            o_ref[0, :, DIFF_VDIM * h:DIFF_VDIM * (h + 1)] = y.astype(o_ref.dtype)


def _diff_sample(page_table, q3, kcache, vcache, kn3, vn3, diff_lambda, subln_g, lam_init, page, ppc):
    db, t, _ = q3.shape
    n_pages = page_table.shape[1]
    assert n_pages % ppc == 0 and t == 8
    n_chunks = n_pages // ppc
    tks = ppc * page
    per_b = lambda w: pl.BlockSpec((1, t, w), lambda b, c, pt: (b, 0, 0))
    const = lambda shape: pl.BlockSpec(shape, lambda b, c, pt: (0,) * len(shape))
    grid_spec = pltpu.PrefetchScalarGridSpec(
        num_scalar_prefetch=1,
        grid=(db, n_chunks),
        in_specs=[per_b(WB), pl.BlockSpec(memory_space=pl.ANY), pl.BlockSpec(memory_space=pl.ANY),
                  per_b(WB), per_b(WV), const(diff_lambda.shape), const((1, DIFF_VDIM))],
        out_specs=per_b(WV),
        scratch_shapes=[pltpu.VMEM((2, tks, WB), F32), pltpu.VMEM((2, tks, WV), F32),
                        pltpu.SemaphoreType.DMA((2, 2)),
                        pltpu.VMEM((DIFF_HEADS, 2 * t, 1), F32), pltpu.VMEM((DIFF_HEADS, 2 * t, 1), F32),
                        pltpu.VMEM((DIFF_HEADS, 2 * t, DIFF_VDIM), F32)],
    )
    return pl.pallas_call(
        functools.partial(_diff_sample_kernel, page=page, ppc=ppc, n_chunks=n_chunks, lam_init=lam_init),
        grid_spec=grid_spec,
        out_shape=jax.ShapeDtypeStruct((db, t, WV), BF16),
        compiler_params=_cparams(("arbitrary", "arbitrary")),
        name="diff_sample",
    )(page_table.reshape(-1), q3, kcache, vcache, kn3, vn3, diff_lambda, subln_g.reshape(1, DIFF_VDIM))


def _softmax_rows(s, valid):
    sm = jnp.where(valid, s, NEG)
    m = jnp.max(sm, axis=-1, keepdims=True)
    e = jnp.where(valid, jnp.exp(sm - m), 0.0)
    return e / jnp.maximum(jnp.sum(e, axis=-1, keepdims=True), 1e-30)


def _select_blocks_rows(score, blk):
    big = jnp.int32(2 ** 30)

    def body(_, carry):
        sc, bias = carry
        mx = jnp.max(sc, axis=-1, keepdims=True)
        first = jnp.min(jnp.where(sc == mx, blk, big), axis=-1, keepdims=True)
        hit = blk == first
        return jnp.where(hit, -3e38, sc), jnp.where(hit, 0.0, bias)

    return lax.fori_loop(0, TOP_N, body, (score, jnp.full(score.shape, NEG, F32)))[1]


def _pad_rows(x, rows):
    return jnp.concatenate([x, jnp.zeros((rows - x.shape[0], x.shape[1]), x.dtype)], axis=0)


def _nsa_sample_kernel(pt_ref, q_ref, qr_ref, ga_ref, cache_hbm, new_ref, wst_ref, wnew_ref,
                       w1k_ref, w1v_ref, ck_ref, cv_ref, w2k_ref, w2v_ref, impt_ref, exp_ref, o_ref,
                       kcbuf, vcbuf, sbuf, sem, *, page, n_pages, tks, past, n_sel):
    b = pl.program_id(0)
    nb = pl.num_programs(0)
    t = q_ref.shape[1]
    g4 = NSA_GROUP
    n_sub = past // CMP_STRIDE
    n_cmp = n_sub - 1
    n_selp = impt_ref.shape[1]

    def copy(bb, k, col0, dst, s):
        p0 = pl.multiple_of(pt_ref[bb * n_pages + k] * page, page)
        return pltpu.make_async_copy(cache_hbm.at[pl.ds(p0, page), pl.ds(col0, dst.shape[1])],
                                     dst.at[pl.ds(pl.multiple_of(k * page, page), page), :], s)

    def start_all(bb, col0, dst, s):
        lax.fori_loop(0, n_pages, lambda k, _: (copy(bb, k, col0, dst, s).start(), 0)[1], 0)

    def wait_all(bb, col0, dst, s):
        lax.fori_loop(0, n_pages, lambda k, _: (copy(bb, k, col0, dst, s).wait(), 0)[1], 0)

    def start_cmp(bb):
        start_all(bb, 0, kcbuf, sem.at[0])
        start_all(bb, KV, vcbuf, sem.at[1])

    @pl.when(b == 0)
    def _():
        start_cmp(0)

    start_all(b, 2 * KV, sbuf, sem.at[2])
    wait_all(b, 0, kcbuf, sem.at[0])
    wait_all(b, KV, vcbuf, sem.at[1])

    abk = _compress_ab(kcbuf, w1k_ref, n_sub)
    abv = _compress_ab(vcbuf, w1v_ref, n_sub)

    @pl.when(b + 1 < nb)
    def _():
        start_cmp(b + 1)

    kcmp = jnp.dot(_compress_hidden(abk, ck_ref[...]), w2k_ref[...], preferred_element_type=F32).astype(BF16)
    vcmp = jnp.dot(_compress_hidden(abv, cv_ref[...]), w2v_ref[...], preferred_element_type=F32).astype(BF16)

    q = q_ref[0].astype(F32)
    qr = qr_ref[0].astype(F32)
    ga = ga_ref[0]
    lane = lax.broadcasted_iota(jnp.int32, (t, LANES), 1)
    trow = lax.broadcasted_iota(jnp.int32, (g4 * t, 1), 0) % t
    qpos = past + trow
    qpos_t = past + lax.broadcasted_iota(jnp.int32, (t, 1), 0)

    def to_half(x, have, want):
        return x if have == want else pltpu.roll(x, HEAD_DIM, axis=1)

    def q_rows(qq, h):
        keep = (lane >= HEAD_DIM * h) & (lane < HEAD_DIM * (h + 1))
        rows = []
        for g in range(g4):
            hd = g4 * h + g
            tile = to_half(qq[:, LANES * (hd // 2):LANES * (hd // 2 + 1)], hd % 2, h)
            rows.append(jnp.where(keep, tile, 0.0))
        return jnp.concatenate(rows, axis=0).astype(BF16)

    wait_all(b, 2 * KV, sbuf, sem.at[2])

    new = new_ref[0]
    ksn = _pad_rows(new[:, 2 * KV:3 * KV], LANES).astype(BF16)
    vsn = _pad_rows(new[:, 3 * KV:4 * KV], LANES).astype(BF16)
    kws = wst_ref[0][:, :KV].astype(BF16)
    vws = wst_ref[0][:, KV:].astype(BF16)
    kwn = _pad_rows(wnew_ref[0][:, :KV], LANES).astype(BF16)
    vwn = _pad_rows(wnew_ref[0][:, KV:], LANES).astype(BF16)
    wbuf = wst_ref.shape[1]
    ncol = lax.broadcasted_iota(jnp.int32, (g4 * t, LANES), 1)
    new_ok = (ncol < t) & (ncol <= trow)
    n_chunks = past // tks
    blocks_per_chunk = tks // SEL_BLOCK
    out_tiles = [[None, None] for _ in range(WA // LANES)]

    for h in range(NSA_KV_HEADS):
        s = _nt(q_rows(q, h), kcmp)
        nidx = lax.broadcasted_iota(jnp.int32, s.shape, 1)
        p = _softmax_rows(s, (nidx * CMP_STRIDE + (CMP_BLOCK - 1) <= qpos) & (nidx < n_cmp))
        o_cmp = jnp.dot(p.astype(BF16), vcmp, preferred_element_type=F32)
        psum = p[:t]
        for g in range(1, g4):
            psum = psum + p[g * t:(g + 1) * t]
        imp = jnp.zeros((t, n_selp), F32)
        for part in _split3(psum):
            imp += jnp.dot(part, impt_ref[...], preferred_element_type=F32)
        blk = lax.broadcasted_iota(jnp.int32, (t, n_selp), 1)
        cur = qpos_t // SEL_BLOCK
        forced = (blk == 0) | (blk == cur) | (blk == cur - 1)
        score = jnp.where(blk > cur, -1.0, jnp.where(forced, FORCED_SCORE, imp))
        score = jnp.where(blk < n_sel, score, -3e38)
        selbias = _select_blocks_rows(score, blk)

        qrh = q_rows(qr, h)
        m = jnp.full((g4 * t, 1), NEG, F32)
        l = jnp.zeros((g4 * t, 1), F32)
        av = jnp.zeros((g4 * t, LANES), F32)
        for c in range(n_chunks):
            k2 = sbuf[c * tks:(c + 1) * tks, :KV].astype(BF16)
            v2 = sbuf[c * tks:(c + 1) * tks, KV:].astype(BF16)
            b0 = c * blocks_per_chunk
            tile = selbias[:, LANES * (b0 // LANES):LANES * (b0 // LANES + 1)].astype(BF16)
            bias = jnp.dot(tile, exp_ref[(b0 % LANES) // blocks_per_chunk], preferred_element_type=F32)
            m, l, av = _flash_rows(_nt(qrh, k2) + jnp.concatenate([bias] * g4, axis=0), v2, m, l, av)
        nb_blk = past // SEL_BLOCK
        bias_new = jnp.concatenate([selbias[:, nb_blk:nb_blk + 1]] * g4, axis=0)
        s = jnp.where(new_ok, _nt(qrh, ksn) + bias_new, NEG)
        m, l, av = _flash_rows(s, vsn, m, l, av)
        o_sel = av / jnp.maximum(l, 1e-30)

        s = jnp.concatenate([_nt(qrh, kws), _nt(qrh, kwn)], axis=1)
        widx = lax.broadcasted_iota(jnp.int32, s.shape, 1)
        kpos = past - wbuf + widx
        valid = (kpos <= qpos) & (kpos >= qpos - WINDOW) & (widx < wbuf + t)
        p = _softmax_rows(s, valid).astype(BF16)
        o_win = (jnp.dot(p[:, :wbuf], vws, preferred_element_type=F32)
                 + jnp.dot(p[:, wbuf:], vwn, preferred_element_type=F32))

        for g in range(g4):
            hd = g4 * h + g
            r = 3 * hd
            rs = slice(g * t, (g + 1) * t)
            o = ga[:, r:r + 1] * o_cmp[rs] + ga[:, r + 1:r + 2] * o_sel[rs] + ga[:, r + 2:r + 3] * o_win[rs]
            out_tiles[hd // 2][hd % 2] = to_half(o, h, hd % 2)

    for k, (lo, hi) in enumerate(out_tiles):
        o_ref[0, :, LANES * k:LANES * (k + 1)] = jnp.where(lane < HEAD_DIM, lo, hi).astype(o_ref.dtype)


def _nsa_sample(page_table, q3, qr3, ga3, cache, new3, win_state, wnew3, cw, page, tks):
    db, t, _ = q3.shape
    n_pages = page_table.shape[1]
    past = n_pages * page
    wbuf = win_state.shape[1]
    w1k, w1v, ck, cv, w2k, w2v, _ = cw
    n_sub = past // CMP_STRIDE
    n_sel = -(-(past + t) // SEL_BLOCK)
    n_selp = -(-n_sel // LANES) * LANES
    blocks_per_chunk = tks // SEL_BLOCK
    assert t == 8 and (past + t) // CMP_STRIDE == n_sub and past % tks == 0 and past % SEL_BLOCK == 0
    assert LANES % blocks_per_chunk == 0 and wbuf == WINDOW and t <= SEL_BLOCK
    impt = jnp.pad(_importance_matrix(n_sel, n_sub).T, ((0, 0), (0, n_selp - n_sel)))
    m = jnp.arange(LANES // blocks_per_chunk)[:, None, None]
    j = jnp.arange(LANES)[None, :, None]
    u = jnp.arange(tks)[None, None, :]
    expand = (j == blocks_per_chunk * m + u // SEL_BLOCK).astype(BF16)
    per_b = lambda r, w: pl.BlockSpec((1, r, w), lambda b, pt: (b, 0, 0))
    const = lambda a: pl.BlockSpec(a.shape, lambda b, pt: (0,) * a.ndim)
    grid_spec = pltpu.PrefetchScalarGridSpec(
        num_scalar_prefetch=1,
        grid=(db,),
        in_specs=[per_b(t, WA), per_b(t, WA), per_b(t, LANES), pl.BlockSpec(memory_space=pl.ANY),
                  per_b(t, 4 * KV), per_b(wbuf, 2 * KV), per_b(t, 2 * KV),
                  const(w1k), const(w1v), const(ck), const(cv), const(w2k), const(w2v), const(impt),
                  const(expand)],
        out_specs=per_b(t, WA),
        scratch_shapes=[pltpu.VMEM((past, KV), F32), pltpu.VMEM((past, KV), F32),
                        pltpu.VMEM((past, 2 * KV), F32), pltpu.SemaphoreType.DMA((3,))],
    )
    return pl.pallas_call(
        functools.partial(_nsa_sample_kernel, page=page, n_pages=n_pages, tks=tks, past=past, n_sel=n_sel),
        grid_spec=grid_spec,
        out_shape=jax.ShapeDtypeStruct((db, t, WA), BF16),
        compiler_params=_cparams(("arbitrary",)),
        name="nsa_sample",
    )(page_table.reshape(-1), q3, qr3, ga3, cache, new3, win_state, wnew3, w1k, w1v, ck, cv, w2k, w2v, impt,
      expand)


def _final_kernel(x_ref, y_ref, g_ref, o_ref):
    x = x_ref[...] + y_ref[...]
    o_ref[...] = x * lax.rsqrt(jnp.mean(x * x, axis=-1, keepdims=True) + EPS) * g_ref[...]


def _final_norm(x1, moe, row0, g, tm):
    n, d = x1.shape
    assert n % tm == 0 and row0 % tm == 0
    off = row0 // tm
    return pl.pallas_call(
        _final_kernel,
        grid=(n // tm,),
        in_specs=[pl.BlockSpec((tm, d), lambda i: (i, 0)), pl.BlockSpec((tm, d), lambda i: (i + off, 0)),
                  _full((1, d))],
        out_specs=pl.BlockSpec((tm, d), lambda i: (i, 0)),
        out_shape=jax.ShapeDtypeStruct((n, d), F32),
        compiler_params=_cparams(("parallel",)),
        name="final_norm",
    )(x1, moe, g.reshape(1, d))


MOE_ROWS = 256


def _dispatch_plan(experts, weights, bm):
    n_tok = experts.shape[0]
    n_asg = n_tok * EXPERT_TOP_K
    flat_e = experts.reshape(-1)
    flat_t = jnp.repeat(jnp.arange(n_tok, dtype=jnp.int32), EXPERT_TOP_K)
    order = jnp.argsort(flat_e)
    e_sorted = flat_e[order]
    counts = jnp.bincount(flat_e, length=N_EXPERTS)
    start = jnp.cumsum(counts) - counts
    padded = (counts + bm - 1) // bm * bm
    pend = jnp.cumsum(padded)
    dest = (pend - padded)[e_sorted] + jnp.arange(n_asg) - start[e_sorted]
    n_blocks = -(-n_asg // bm) + N_EXPERTS
    n_slots = n_blocks * bm
    slot_tok = jnp.zeros((n_slots,), jnp.int32).at[dest].set(flat_t[order])
    slot_w = jnp.zeros((n_slots,), F32).at[dest].set(weights.reshape(-1)[order])
    block_e = jnp.minimum(jnp.searchsorted(pend, jnp.arange(n_blocks) * bm, side='right'), N_EXPERTS - 1)
    return slot_tok, slot_w, block_e.astype(jnp.int32)


def kernel(x_prompt, x_sample, cache_nsa_kv, cache_diff_k, cache_diff_v, state_nsa_win_kv, page_table,
           norm_mix_g, w_in, nsa_cmp_pos, nsa_cmp_k_w1, nsa_cmp_k_w2, nsa_cmp_v_w1, nsa_cmp_v_w2,
           diff_lambda, diff_subln_g, w_proj_a, w_proj_b, w_out, norm_ffn_g,
           router_group_w, router_group_b, router_expert_w, router_expert_b,
           expert_w_gate, expert_w_up, expert_w_down, norm_final_g):
    depth = w_in.shape[0]
    bsz, seq, d = x_prompt.shape
    db, t, _ = x_sample.shape
    n_pool, page = cache_nsa_kv.shape[1:3]
    past = page_table.shape[1] * page
    wbuf = state_nsa_win_kv.shape[2]
    assert depth == 1 and bsz == 1
    l = 0
    lam_init = 0.8 - 0.6 * math.exp(-0.3 * l)
    w = _split_w_in(w_in[l], d)
    cw = _compress_weights(nsa_cmp_pos[l], nsa_cmp_k_w1[l], nsa_cmp_k_w2[l], nsa_cmp_v_w1[l], nsa_cmp_v_w2[l])
    wa, wb, wo = w_proj_a[l].astype(BF16), w_proj_b[l].astype(BF16), w_out[l].astype(BF16)
    rw = jnp.pad(jnp.concatenate([router_group_w[l], router_expert_w[l]], axis=1),
                 ((0, 0), (0, ROUTE_COLS - N_GROUPS - N_EXPERTS)))
    rb = jnp.pad(jnp.concatenate([router_group_b[l], router_expert_b[l]]),
                 (0, ROUTE_COLS - N_GROUPS - N_EXPERTS)).reshape(1, ROUTE_COLS)
    rwh = rw.astype(BF16)
    rwl = (rw - rwh.astype(F32)).astype(BF16)

    xp = x_prompt.reshape(seq, d)
    (nsa_p, win_p, dk_p, dv_p, gm_p, ks_b, kw_b, kb_b,
     qat, qart, qbt, vst, vwt, vbt, gat) = _project(xp, jnp.arange(seq), norm_mix_g[l], w, 512, True)
    kcmp, _, vcmpt = _compress(nsa_p, cw, 256)
    oat = _nsa_prompt(qat, qart, gat, kcmp, vcmpt, ks_b, vst, kw_b, vwt, 128, 512)
    obt = _diff_prompt(qbt, kb_b, vbt, diff_lambda[l], diff_subln_g[l], lam_init, 512, 512)
    x1p, h2p, re_p, rw_p = _merge(oat, obt, gm_p, xp, wa, wb, wo, norm_ffn_g[l], rwh, rwl, rb, 512, True)

    ns = db * t
    xs = x_sample.reshape(ns, d)
    pos_s = past + jnp.arange(ns) % t
    (nsa_s, win_s, dk_s, dv_s, gm_s, _, _, _, qa_s, qar_s, qb_s, ga_s) = _project(
        xs, pos_s, norm_mix_g[l], w, ns, False)
    r3 = lambda a: a.reshape(db, t, a.shape[-1])
    oa_s = _nsa_sample(page_table, r3(qa_s), r3(qar_s), r3(ga_s), cache_nsa_kv[l].reshape(n_pool * page, -1),
                       r3(nsa_s), state_nsa_win_kv[l].reshape(db, wbuf, -1), r3(win_s), cw, page, 2048)
    ob_s = _diff_sample(page_table, r3(qb_s), cache_diff_k[l].reshape(n_pool * page, -1),
                        cache_diff_v[l].reshape(n_pool * page, -1), r3(dk_s), r3(dv_s),
                        diff_lambda[l], diff_subln_g[l], lam_init, page, 16)
    x1s, h2s, re_s, rw_s = _merge(oa_s.reshape(ns, -1), ob_s.reshape(ns, -1), gm_s, xs, wa, wb, wo,
                                  norm_ffn_g[l], rwh, rwl, rb, ns, False)

    h2 = jnp.concatenate([h2p, h2s], axis=0)
    experts = jnp.concatenate([re_p[:, :EXPERT_TOP_K], re_s[:, :EXPERT_TOP_K]], axis=0)
    weights = jnp.concatenate([rw_p[:, :EXPERT_TOP_K], rw_s[:, :EXPERT_TOP_K]], axis=0)
    slot_tok, slot_w, block_e = _dispatch_plan(experts, weights, MOE_ROWS)
    ys = _expert_ffn(block_e, h2[slot_tok], slot_w[:, None], expert_w_gate[l], expert_w_up[l], expert_w_down[l],
                     MOE_ROWS)
    moe = jnp.zeros((seq + ns, d), F32).at[slot_tok].add(ys)
    y_prompt = _final_norm(x1p, moe, 0, norm_final_g, 512)
    y_sample = _final_norm(x1s, moe, seq, norm_final_g, ns)

    wn = min(WINDOW, seq)
    win_all = jnp.concatenate([state_nsa_win_kv[l].reshape(db, wbuf, -1), r3(win_s)], axis=1)
    kvs = (4, NSA_KV_HEADS, HEAD_DIM)
    dks = (DIFF_HEADS, 2, HEAD_DIM)
    dvs = (DIFF_HEADS, DIFF_VDIM)
    return (y_prompt.reshape(1, seq, d), y_sample.reshape(db, t, d),
            nsa_p.reshape((1, 1, seq) + kvs), nsa_s.reshape((1, db, t) + kvs),
            dk_p.reshape((1, 1, seq) + dks), dk_s.reshape((1, db, t) + dks),
            dv_p.reshape((1, 1, seq) + dvs), dv_s.reshape((1, db, t) + dvs),
            win_p[seq - wn:].reshape(1, 1, wn, 2, NSA_KV_HEADS, HEAD_DIM),
            win_all[:, -min(WINDOW, past + t):].reshape(1, db, -1, 2, NSA_KV_HEADS, HEAD_DIM))
```

```python
import functools
import math

import jax
import jax.numpy as jnp
from jax import lax
from jax.experimental import pallas as pl
from jax.experimental.pallas import tpu as pltpu

F32 = jnp.float32
BF16 = jnp.bfloat16

HEAD_DIM = 64
HALF = HEAD_DIM // 2
NSA_HEADS = 8
NSA_KV_HEADS = 2
NSA_GROUP = NSA_HEADS // NSA_KV_HEADS
CMP_STRIDE = 16
CMP_BLOCK = 2 * CMP_STRIDE
SEL_BLOCK = 64
SEL_PER_CMP = SEL_BLOCK // CMP_STRIDE
TOP_N = 16
WINDOW = 512
FORCED_SCORE = 1e4
DIFF_HEADS = 4
DIFF_VDIM = 2 * HEAD_DIM
N_GROUPS = 4
EXPERTS_PER_GROUP = 8
N_EXPERTS = N_GROUPS * EXPERTS_PER_GROUP
EXPERT_TOP_K = 2
ROPE_THETA = 10000.0
EPS = 1e-6
NEG = -1e30
SCALE = HEAD_DIM ** -0.5

LANES = 128
VMEM_LIMIT = 56 * 1024 * 1024

WA = NSA_HEADS * HEAD_DIM
KV = NSA_KV_HEADS * HEAD_DIM
WB = DIFF_HEADS * 2 * HEAD_DIM
WV = DIFF_HEADS * DIFF_VDIM
N_GATE = 3 * NSA_HEADS


def _cparams(sem):
    return pltpu.CompilerParams(dimension_semantics=sem, vmem_limit_bytes=VMEM_LIMIT)


def _full(shape):
    return pl.BlockSpec(shape, lambda *_: (0,) * len(shape))


def _swap_halves(t):
    lane = lax.broadcasted_iota(jnp.int32, t.shape, 1)
    fwd = pltpu.roll(t, LANES - HALF, axis=1)
    bwd = pltpu.roll(t, HALF, axis=1)
    return jnp.where(lane % HEAD_DIM < HALF, fwd, bwd)


def _rope_rows(t, cos, sin):
    outs = []
    for a in range(0, t.shape[1], LANES):
        x = t[:, a:a + LANES]
        outs.append(x * cos + _swap_halves(x) * sin)
    return outs[0] if len(outs) == 1 else jnp.concatenate(outs, axis=1)


def _rope_cols(t, cos, sin):
    outs = []
    for a in range(0, t.shape[0], HEAD_DIM):
        x1 = t[a:a + HALF]
        x2 = t[a + HALF:a + HEAD_DIM]
        outs.append(x1 * cos - x2 * sin)
        outs.append(x2 * cos + x1 * sin)
    return jnp.concatenate(outs, axis=0)


def _proj_kernel(x_ref, g_ref, wn_ref, wt_ref, cosn_ref, sinn_ref, cost_ref, sint_ref,
                 nsa_ref, win_ref, dk_ref, dv_ref, gm_ref, ksb_ref, kwb_ref, kbb_ref, *rest, transposed_q):
    x = x_ref[...]
    h = x * lax.rsqrt(jnp.mean(x * x, axis=-1, keepdims=True) + EPS) * g_ref[...]
    hb = h.astype(BF16)
    cosn = cosn_ref[...]
    sinn = sinn_ref[...]

    def mm(a, b):
        return jnp.dot(hb, wn_ref[:, a:b], preferred_element_type=F32)

    c = 0
    y = mm(c, c + 4 * KV)
    ks = _rope_rows(y[:, 2 * KV:3 * KV], cosn, sinn)
    nsa_ref[:, :2 * KV] = y[:, :2 * KV]
    nsa_ref[:, 2 * KV:3 * KV] = ks
    nsa_ref[:, 3 * KV:] = y[:, 3 * KV:]
    ksb_ref[...] = ks.astype(BF16)
    c += 4 * KV
    y = mm(c, c + 2 * KV)
    kw = _rope_rows(y[:, :KV], cosn, sinn)
    win_ref[:, :KV] = kw
    win_ref[:, KV:] = y[:, KV:]
    kwb_ref[...] = kw.astype(BF16)
    c += 2 * KV
    kb = _rope_rows(mm(c, c + WB), cosn, sinn)
    dk_ref[...] = kb
    kbb_ref[...] = kb.astype(BF16)
    c += WB
    dv_ref[...] = mm(c, c + WV)
    c += WV
    d_model = x.shape[1]
    gm_ref[...] = jax.nn.sigmoid(mm(c, c + 2 * d_model))
    c += 2 * d_model

    if transposed_q:
        qat_ref, qart_ref, qbt_ref, vst_ref, vwt_ref, vbt_ref, gat_ref = rest
        cost = cost_ref[...]
        sint = sint_ref[...]

        def mmt(a, b):
            return lax.dot_general(wt_ref[a:b, :], hb, (((1,), (1,)), ((), ())), preferred_element_type=F32)

        r = 0
        qa = mmt(r, r + WA) * SCALE
        qat_ref[...] = qa.astype(BF16)
        qart_ref[...] = _rope_cols(qa, cost, sint).astype(BF16)
        r += WA
        qbt_ref[...] = _rope_cols(mmt(r, r + WB) * SCALE, cost, sint).astype(BF16)
        r += WB
        vst_ref[...] = mmt(r, r + KV).astype(BF16)
        r += KV
        vwt_ref[...] = mmt(r, r + KV).astype(BF16)
        r += KV
        vbt_ref[...] = mmt(r, r + WV).astype(BF16)
        r += WV
        gat_ref[...] = jax.nn.sigmoid(mmt(r, r + 32))
    else:
        qa_ref, qar_ref, qb_ref, ga_ref = rest
        qa = mm(c, c + WA) * SCALE
        qa_ref[...] = qa.astype(BF16)
        qar_ref[...] = _rope_rows(qa, cosn, sinn).astype(BF16)
        c += WA
        qb_ref[...] = _rope_rows(mm(c, c + WB) * SCALE, cosn, sinn).astype(BF16)
        c += WB
        ga_ref[...] = jax.nn.sigmoid(mm(c, c + LANES))


def _split_w_in(w_in, d_model):
    sizes = [WA, KV, KV, KV, KV, KV, KV, N_GATE, WB, WB, WV, 2 * d_model]
    offs = [0]
    for s in sizes:
        offs.append(offs[-1] + s)
    names = ["qa", "kc", "vc", "ks", "vs", "kw", "vw", "ga", "qb", "kb", "vb", "gm"]
    return {n: w_in[:, offs[i]:offs[i + 1]] for i, n in enumerate(names)}


def _rope_tables(pos):
    inv = ROPE_THETA ** (-jnp.arange(HALF, dtype=F32) / HALF)
    ang = pos.astype(F32)[:, None] * inv[None, :]
    cos, sin = jnp.cos(ang), jnp.sin(ang)
    cosn = jnp.tile(cos, (1, LANES // HALF))
    sinn = jnp.tile(jnp.concatenate([-sin, sin], axis=1), (1, LANES // HEAD_DIM))
    return cosn, sinn, cos.T, sin.T


def _project(x, pos, norm_g, w, tm, transposed_q):
    n, d = x.shape
    assert n % tm == 0
    cosn, sinn, cost, sint = _rope_tables(pos)
    wn_parts = [w["kc"], w["vc"], w["ks"], w["vs"], w["kw"], w["vw"], w["kb"], w["vb"], w["gm"]]
    ga_pad = jnp.pad(w["ga"], ((0, 0), (0, LANES - N_GATE)))
    if transposed_q:
        wt = jnp.concatenate([w["qa"], w["qb"], w["vs"], w["vw"], w["vb"], ga_pad[:, :32]], axis=1).T.astype(BF16)
    else:
        wn_parts += [w["qa"], w["qb"], ga_pad]
        wt = jnp.zeros((8, d), BF16)
    wn = jnp.concatenate(wn_parts, axis=1).astype(BF16)

    row = lambda c: pl.BlockSpec((tm, c), lambda i: (i, 0))
    col = lambda r: pl.BlockSpec((r, tm), lambda i: (0, i))
    out_shape = [jax.ShapeDtypeStruct((n, 4 * KV), F32), jax.ShapeDtypeStruct((n, 2 * KV), F32),
                 jax.ShapeDtypeStruct((n, WB), F32), jax.ShapeDtypeStruct((n, WV), F32),
                 jax.ShapeDtypeStruct((n, 2 * d), F32), jax.ShapeDtypeStruct((n, KV), BF16),
                 jax.ShapeDtypeStruct((n, KV), BF16), jax.ShapeDtypeStruct((n, WB), BF16)]
    out_specs = [row(4 * KV), row(2 * KV), row(WB), row(WV), row(2 * d), row(KV), row(KV), row(WB)]
    if transposed_q:
        out_shape += [jax.ShapeDtypeStruct((WA, n), BF16), jax.ShapeDtypeStruct((WA, n), BF16),
                      jax.ShapeDtypeStruct((WB, n), BF16), jax.ShapeDtypeStruct((KV, n), BF16),
                      jax.ShapeDtypeStruct((KV, n), BF16), jax.ShapeDtypeStruct((WV, n), BF16),
                      jax.ShapeDtypeStruct((32, n), F32)]
        out_specs += [col(WA), col(WA), col(WB), col(KV), col(KV), col(WV), col(32)]
    else:
        out_shape += [jax.ShapeDtypeStruct((n, WA), BF16), jax.ShapeDtypeStruct((n, WA), BF16),
                      jax.ShapeDtypeStruct((n, WB), BF16), jax.ShapeDtypeStruct((n, LANES), F32)]
        out_specs += [row(WA), row(WA), row(WB), row(LANES)]
    return pl.pallas_call(
        functools.partial(_proj_kernel, transposed_q=transposed_q),
        grid=(n // tm,),
        in_specs=[row(d), _full((1, d)), _full(wn.shape), _full(wt.shape),
                  row(LANES), row(LANES), col(HALF), col(HALF)],
        out_specs=out_specs,
        out_shape=out_shape,
        compiler_params=_cparams(("parallel",)),
        name="proj",
    )(x, norm_g.reshape(1, d), wn, wt, cosn, sinn, cost, sint)


def _pad_head(qt, slot):
    z = jnp.zeros_like(qt)
    return jnp.concatenate([qt, z] if slot == 0 else [z, qt], axis=0)


def _flash_step(st, vt, m_prev, l_prev, acc_prev):
    m_new = jnp.maximum(m_prev, jnp.max(st, axis=0, keepdims=True))
    alpha = jnp.exp(m_prev - m_new)
    p = jnp.exp(st - m_new)
    l_new = alpha * l_prev + jnp.sum(p, axis=0, keepdims=True)
    acc_new = alpha * acc_prev + jnp.dot(vt, p.astype(BF16), preferred_element_type=F32)
    return m_new, l_new, acc_new


def _diff_lambda(lam_ref, lam_init):
    lv = lam_ref[...]
    a = jnp.sum(lv[0:1] * lv[1:2], axis=-1, keepdims=True)
    b = jnp.sum(lv[2:3] * lv[3:4], axis=-1, keepdims=True)
    return jnp.exp(a) - jnp.exp(b) + lam_init


def _diff_prompt_kernel(qt_ref, k_ref, vt_ref, lam_ref, g_ref, o_ref, m_ref, l_ref, acc_ref, *, tq, tk, lam_init):
    i = pl.program_id(0)
    j = pl.program_id(1)

    @pl.when(j == 0)
    def _():
        m_ref[...] = jnp.full_like(m_ref, NEG)
        l_ref[...] = jnp.zeros_like(l_ref)
        acc_ref[...] = jnp.zeros_like(acc_ref)

    @pl.when(j * tk < (i + 1) * tq)
    def _():
        kpos = j * tk + lax.broadcasted_iota(jnp.int32, (tk, tq), 0)
        qpos = i * tq + lax.broadcasted_iota(jnp.int32, (tk, tq), 1)
        bias = jnp.where(kpos <= qpos, 0.0, NEG)
        for h in range(DIFF_HEADS):
            k2 = k_ref[:, 2 * HEAD_DIM * h:2 * HEAD_DIM * (h + 1)]
            vt = vt_ref[DIFF_VDIM * h:DIFF_VDIM * (h + 1), :]
            for c in range(2):
                hc = 2 * h + c
                qt = _pad_head(qt_ref[HEAD_DIM * hc:HEAD_DIM * (hc + 1), :], c)
                st = jnp.dot(k2, qt, preferred_element_type=F32) + bias
                m, l, acc = _flash_step(st, vt, m_ref[hc:hc + 1], l_ref[hc:hc + 1], acc_ref[hc])
                m_ref[hc:hc + 1] = m
                l_ref[hc:hc + 1] = l
                acc_ref[hc] = acc

    @pl.when(j == pl.num_programs(1) - 1)
    def _():
        lam = _diff_lambda(lam_ref, lam_init)
        for h in range(DIFF_HEADS):
            o0 = acc_ref[2 * h] / l_ref[2 * h:2 * h + 1]
            o1 = acc_ref[2 * h + 1] / l_ref[2 * h + 1:2 * h + 2]
            a = o0 - lam * o1
            y = a * lax.rsqrt(jnp.mean(a * a, axis=0, keepdims=True) + EPS) * g_ref[...] * (1.0 - lam_init)
            o_ref[DIFF_VDIM * h:DIFF_VDIM * (h + 1), :] = y.astype(o_ref.dtype)


def _diff_prompt(qbt, kb, vbt, diff_lambda, subln_g, lam_init, tq, tk):
    n = kb.shape[0]
    assert n % tq == 0 and n % tk == 0 and tq % tk == 0
    last = lambda i, j: jnp.minimum(j, ((i + 1) * tq - 1) // tk)
    return pl.pallas_call(
        functools.partial(_diff_prompt_kernel, tq=tq, tk=tk, lam_init=lam_init),
        grid=(n // tq, n // tk),
        in_specs=[pl.BlockSpec((WB, tq), lambda i, j: (0, i)),
                  pl.BlockSpec((tk, WB), lambda i, j: (last(i, j), 0)),
                  pl.BlockSpec((WV, tk), lambda i, j: (0, last(i, j))),
                  _full(diff_lambda.shape), _full((DIFF_VDIM, 1))],
        out_specs=pl.BlockSpec((WV, tq), lambda i, j: (0, i)),
        out_shape=jax.ShapeDtypeStruct((WV, n), BF16),
        scratch_shapes=[pltpu.VMEM((2 * DIFF_HEADS, tq), F32), pltpu.VMEM((2 * DIFF_HEADS, tq), F32),
                        pltpu.VMEM((2 * DIFF_HEADS, DIFF_VDIM, tq), F32)],
        compiler_params=_cparams(("parallel", "arbitrary")),
        name="diff_prompt",
    )(qbt, kb, vbt, diff_lambda, subln_g.reshape(DIFF_VDIM, 1))


def _compress_weights(pos, k_w1, k_w2, v_w1, v_w2):
    hd = HEAD_DIM
    z = jnp.zeros((CMP_STRIDE, hd, hd), F32)

    def halves(w1):
        w3 = w1.reshape(CMP_BLOCK, hd, -1)
        return w3[:CMP_STRIDE], w3[CMP_STRIDE:]

    def expand(top, bot):
        rows = [[top, z, bot, z], [z, top, z, bot]]
        return jnp.concatenate([jnp.concatenate(r, axis=2) for r in rows], axis=1).astype(BF16)

    pf = pos.reshape(1, -1)
    ck, cv = pf @ k_w1, pf @ v_w1
    z2 = jnp.zeros((hd, hd), F32)
    w2k = jnp.block([[k_w2, z2], [z2, k_w2]]).astype(BF16)
    w2v = jnp.block([[v_w2, z2], [z2, v_w2]]).astype(BF16)
    return (expand(*halves(k_w1)), expand(*halves(v_w1)), jnp.concatenate([ck, ck], axis=1),
            jnp.concatenate([cv, cv], axis=1), w2k, w2v, w2v.T)


def _compress_ab(x_ref, w_ref, n_sub):
    acc = jnp.zeros((n_sub, 2 * KV), F32)
    for r in range(CMP_STRIDE):
        xr = x_ref[pl.ds(r, n_sub, stride=CMP_STRIDE), :].astype(BF16)
        acc += jnp.dot(xr, w_ref[r], preferred_element_type=F32)
    return acc


def _compress_hidden(ab, c):
    n_sub = ab.shape[0]
    nxt = pltpu.roll(ab[:, KV:], n_sub - 1, axis=0)
    return jax.nn.gelu(ab[:, :KV] + nxt + c).astype(BF16)


def _compress_ab_kernel(xk_ref, xv_ref, wk_ref, wv_ref, abk_ref, abv_ref, *, n_sub):
    abk_ref[...] = _compress_ab(xk_ref, wk_ref, n_sub)
    abv_ref[...] = _compress_ab(xv_ref, wv_ref, n_sub)


def _compress_mlp_kernel(abk_ref, abv_ref, ck_ref, cv_ref, w2k_ref, w2v_ref, w2vt_ref, kc_ref, vc_ref, vct_ref):
    gk = _compress_hidden(abk_ref[...], ck_ref[...])
    gv = _compress_hidden(abv_ref[...], cv_ref[...])
    kc_ref[...] = jnp.dot(gk, w2k_ref[...], preferred_element_type=F32).astype(BF16)
    vc_ref[...] = jnp.dot(gv, w2v_ref[...], preferred_element_type=F32).astype(BF16)
    vct_ref[...] = lax.dot_general(w2vt_ref[...], gv, (((1,), (1,)), ((), ())),
                                   preferred_element_type=F32).astype(BF16)


def _compress(kv, cw, sub_tile):
    w1k, w1v, ck, cv, w2k, w2v, w2vt = cw
    n_sub = kv.shape[0] // CMP_STRIDE
    assert n_sub % sub_tile == 0
    ab_shape = jax.ShapeDtypeStruct((n_sub, 2 * KV), F32)
    ab_spec = pl.BlockSpec((sub_tile, 2 * KV), lambda i: (i, 0))
    abk, abv = pl.pallas_call(
        functools.partial(_compress_ab_kernel, n_sub=sub_tile),
        grid=(n_sub // sub_tile,),
        in_specs=[pl.BlockSpec((sub_tile * CMP_STRIDE, KV), lambda i: (i, 0)),
                  pl.BlockSpec((sub_tile * CMP_STRIDE, KV), lambda i: (i, 1)), _full(w1k.shape), _full(w1v.shape)],
        out_specs=[ab_spec, ab_spec],
        out_shape=[ab_shape, ab_shape],
        compiler_params=_cparams(("parallel",)),
        name="compress_ab",
    )(kv, kv, w1k, w1v)
    return pl.pallas_call(
        _compress_mlp_kernel,
        out_shape=[jax.ShapeDtypeStruct((n_sub, KV), BF16), jax.ShapeDtypeStruct((n_sub, KV), BF16),
                   jax.ShapeDtypeStruct((KV, n_sub), BF16)],
        compiler_params=pltpu.CompilerParams(vmem_limit_bytes=VMEM_LIMIT),
        name="compress_mlp",
    )(abk, abv, ck, cv, w2k, w2v, w2vt)


def _importance_matrix(n_sel, n_cmp):
    j = jnp.arange(n_sel)[:, None]
    n = jnp.arange(n_cmp)[None, :]
    return ((n >= SEL_PER_CMP * j - 1) & (n <= SEL_PER_CMP * j + SEL_PER_CMP - 1)).astype(BF16)


def _split3(x):
    hi = x.astype(BF16)
    r = x - hi.astype(F32)
    mid = r.astype(BF16)
    lo = (r - mid.astype(F32)).astype(BF16)
    return hi, mid, lo


def _tile_lanes(x, k):
    return jnp.concatenate([x] * k, axis=1)


def _select_blocks(score, blk):
    big = jnp.int32(2 ** 30)

    def body(_, carry):
        sc, bias = carry
        mx = jnp.max(sc, axis=0, keepdims=True)
        first = jnp.min(jnp.where(sc == mx, blk, big), axis=0, keepdims=True)
        hit = blk == first
        return jnp.where(hit, -3e38, sc), jnp.where(hit, 0.0, bias)

    n_pick = min(TOP_N, score.shape[0])
    return lax.fori_loop(0, n_pick, body, (score, jnp.full(score.shape, NEG, F32)))[1]


def _softmax_cols(st, valid):
    sm = jnp.where(valid, st, NEG)
    m = jnp.max(sm, axis=0, keepdims=True)
    e = jnp.where(valid, jnp.exp(sm - m), 0.0)
    return e / jnp.maximum(jnp.sum(e, axis=0, keepdims=True), 1e-30)


def _nsa_prompt_kernel(qat_ref, qart_ref, gat_ref, kc_ref, vct_ref, imp_ref, ks_ref, vst_ref, kw_ref, vwt_ref,
                       o_ref, bias_ref, *, tq, tk, n_cmp):
    i = pl.program_id(0)
    g4 = NSA_GROUP
    ncp = kc_ref.shape[0]
    n_sel = imp_ref.shape[0]
    q0 = i * tq
    qlane = q0 + lax.broadcasted_iota(jnp.int32, (1, tq), 1)

    def heads_t(ref, h):
        return jnp.concatenate([ref[HEAD_DIM * (g4 * h + g):HEAD_DIM * (g4 * h + g + 1), :] for g in range(g4)],
                               axis=1)

    for h in range(NSA_KV_HEADS):
        qt = _pad_head(heads_t(qat_ref, h), h)
        st = jnp.dot(kc_ref[...], qt, preferred_element_type=F32)
        nrow = lax.broadcasted_iota(jnp.int32, (ncp, tq), 0)
        valid = (nrow * CMP_STRIDE + (CMP_BLOCK - 1) <= qlane) & (nrow < n_cmp)
        p = _softmax_cols(st, _tile_lanes(valid, g4))
        o_cmp = jnp.dot(vct_ref[HEAD_DIM * h:HEAD_DIM * (h + 1), :], p.astype(BF16), preferred_element_type=F32)
        psum = p[:, :tq]
        for g in range(1, g4):
            psum = psum + p[:, g * tq:(g + 1) * tq]
        imp = jnp.zeros((n_sel, tq), F32)
        for part in _split3(psum):
            imp += jnp.dot(imp_ref[...], part, preferred_element_type=F32)
        blk = lax.broadcasted_iota(jnp.int32, (n_sel, tq), 0)
        cur = qlane // SEL_BLOCK
        forced = (blk == 0) | (blk == cur) | (blk == cur - 1)
        score = jnp.where(blk > cur, -1.0, jnp.where(forced, FORCED_SCORE, imp))
        bias_ref[...] = _select_blocks(score, blk)

        qrt = _pad_head(heads_t(qart_ref, h), h)
        per_chunk = tk // SEL_BLOCK

        def chunk(c, carry):
            m, l, acc = carry
            k0 = pl.multiple_of(c * tk, tk)
            st = jnp.dot(ks_ref[pl.ds(k0, tk), :], qrt, preferred_element_type=F32)
            b8 = bias_ref[pl.ds(pl.multiple_of(c * per_chunk, per_chunk), per_chunk), :]
            bias = jnp.broadcast_to(b8[:, None, :], (per_chunk, SEL_BLOCK, tq)).reshape(tk, tq)
            kpos = k0 + lax.broadcasted_iota(jnp.int32, (tk, tq), 0)
            bias = bias + jnp.where(kpos <= qlane, 0.0, NEG)
            vt = vst_ref[HEAD_DIM * h:HEAD_DIM * (h + 1), pl.ds(k0, tk)]
            return _flash_step(st + _tile_lanes(bias, g4), vt, m, l, acc)

        n_chunks = (q0 + tq - 1) // tk + 1
        init = (jnp.full((1, g4 * tq), NEG, F32), jnp.zeros((1, g4 * tq), F32), jnp.zeros((HEAD_DIM, g4 * tq), F32))
        m, l, acc = lax.fori_loop(0, n_chunks, chunk, init)
        o_sel = acc / jnp.maximum(l, 1e-30)

        nw = WINDOW + tq
        w0 = pl.multiple_of(jnp.maximum(q0 - WINDOW, 0), LANES)
        st = jnp.dot(kw_ref[pl.ds(w0, nw), :], qrt, preferred_element_type=F32)
        kpos = w0 + lax.broadcasted_iota(jnp.int32, (nw, tq), 0)
        valid = (kpos <= qlane) & (kpos >= qlane - WINDOW)
        p = _softmax_cols(st, _tile_lanes(valid, g4))
        o_win = jnp.dot(vwt_ref[HEAD_DIM * h:HEAD_DIM * (h + 1), pl.ds(w0, nw)], p.astype(BF16),
                        preferred_element_type=F32)

        for g in range(g4):
            r = 3 * (g4 * h + g)
            sl = slice(g * tq, (g + 1) * tq)
            o = (gat_ref[r:r + 1, :] * o_cmp[:, sl] + gat_ref[r + 1:r + 2, :] * o_sel[:, sl]
                 + gat_ref[r + 2:r + 3, :] * o_win[:, sl])
            o_ref[HEAD_DIM * (g4 * h + g):HEAD_DIM * (g4 * h + g + 1), :] = o.astype(o_ref.dtype)


def _nsa_prompt(qat, qart, gat, kc, vct, ks, vst, kw, vwt, tq, tk):
    n = ks.shape[0]
    n_sub = kc.shape[0]
    n_cmp = n_sub - 1
    n_sel = n // SEL_BLOCK
    assert n % tq == 0 and n % tk == 0 and tk % SEL_BLOCK == 0 and n >= WINDOW + tq and tq % LANES == 0
    imp = _importance_matrix(n_sel, n_sub)
    col = lambda r: pl.BlockSpec((r, tq), lambda i: (0, i))
    return pl.pallas_call(
        functools.partial(_nsa_prompt_kernel, tq=tq, tk=tk, n_cmp=n_cmp),
        grid=(n // tq,),
        in_specs=[col(WA), col(WA), col(32), _full(kc.shape), _full(vct.shape), _full(imp.shape),
                  _full(ks.shape), _full(vst.shape), _full(kw.shape), _full(vwt.shape)],
        out_specs=col(WA),
        out_shape=jax.ShapeDtypeStruct((WA, n), BF16),
        scratch_shapes=[pltpu.VMEM((n_sel, tq), F32)],
        compiler_params=_cparams(("parallel",)),
        name="nsa_prompt",
    )(qat, qart, gat, kc, vct, imp, ks, vst, kw, vwt)


ROUTE_COLS = LANES


def _first_lane_of_max(v, lane):
    mx = jnp.max(v, axis=-1, keepdims=True)
    return mx, jnp.min(jnp.where(v == mx, lane, ROUTE_COLS), axis=-1, keepdims=True)


def _merge_kernel(oa_ref, ob_ref, gm_ref, x_ref, wa_ref, wb_ref, wo_ref, g_ref, rwh_ref, rwl_ref, rb_ref,
                  x1_ref, h2_ref, re_ref, rw_ref, *, transposed):
    d = x_ref.shape[1]
    dims = (((0,), (0,)), ((), ())) if transposed else (((1,), (0,)), ((), ()))
    ya = lax.dot_general(oa_ref[...], wa_ref[...], dims, preferred_element_type=F32)
    yb = lax.dot_general(ob_ref[...], wb_ref[...], dims, preferred_element_type=F32)
    mix = gm_ref[:, :d] * ya + gm_ref[:, d:] * yb
    x1 = x_ref[...] + jnp.dot(mix.astype(BF16), wo_ref[...], preferred_element_type=F32)
    x1_ref[...] = x1
    h2 = x1 * lax.rsqrt(jnp.mean(x1 * x1, axis=-1, keepdims=True) + EPS) * g_ref[...]
    h2_ref[...] = h2.astype(BF16)

    hi = h2.astype(BF16)
    lo = (h2 - hi.astype(F32)).astype(BF16)
    logits = (jnp.dot(hi, rwh_ref[...], preferred_element_type=F32)
              + jnp.dot(lo, rwh_ref[...], preferred_element_type=F32)
              + jnp.dot(hi, rwl_ref[...], preferred_element_type=F32)) + rb_ref[...]
    lane = lax.broadcasted_iota(jnp.int32, logits.shape, 1)
    is_g = lane < N_GROUPS
    gl = jnp.where(is_g, logits, NEG)
    gmx, grp = _first_lane_of_max(gl, lane)
    p_grp = 1.0 / jnp.sum(jnp.where(is_g, jnp.exp(gl - gmx), 0.0), axis=-1, keepdims=True)
    e_id = lane - N_GROUPS
    in_grp = (e_id >= 0) & (e_id < N_EXPERTS) & (e_id // EXPERTS_PER_GROUP == grp)
    el = jnp.where(in_grp, logits, NEG)
    emx = jnp.max(el, axis=-1, keepdims=True)
    ee = jnp.where(in_grp, jnp.exp(el - emx), -1.0)
    e1, i1 = _first_lane_of_max(ee, lane)
    e2, i2 = _first_lane_of_max(jnp.where(lane == i1, -1.0, ee), lane)
    inv = p_grp / (e1 + e2)
    re_ref[...] = jnp.where(lane == 0, i1 - N_GROUPS, jnp.where(lane == 1, i2 - N_GROUPS, 0))
    rw_ref[...] = jnp.where(lane == 0, e1 * inv, jnp.where(lane == 1, e2 * inv, 0.0))


def _merge(oa, ob, gm, x, wa, wb, wo, ffn_g, rwh, rwl, rb, tm, transposed):
    n, d = x.shape
    assert n % tm == 0
    row = lambda c: pl.BlockSpec((tm, c), lambda i: (i, 0))
    o_spec = pl.BlockSpec((WA, tm), lambda i: (0, i)) if transposed else row(WA)
    return pl.pallas_call(
        functools.partial(_merge_kernel, transposed=transposed),
        grid=(n // tm,),
        in_specs=[o_spec, o_spec, row(2 * d), row(d), _full(wa.shape), _full(wb.shape), _full(wo.shape),
                  _full((1, d)), _full(rwh.shape), _full(rwl.shape), _full((1, ROUTE_COLS))],
        out_specs=[row(d), row(d), row(ROUTE_COLS), row(ROUTE_COLS)],
        out_shape=[jax.ShapeDtypeStruct((n, d), F32), jax.ShapeDtypeStruct((n, d), BF16),
                   jax.ShapeDtypeStruct((n, ROUTE_COLS), jnp.int32), jax.ShapeDtypeStruct((n, ROUTE_COLS), F32)],
        compiler_params=_cparams(("parallel",)),
        name="merge_route",
    )(oa, ob, gm, x, wa, wb, wo, ffn_g.reshape(1, d), rwh, rwl, rb)


def _expert_kernel(be_ref, xs_ref, sw_ref, wg_ref, wu_ref, wd_ref, ys_ref):
    del be_ref
    xb = xs_ref[...]
    gate = jnp.dot(xb, wg_ref[0].astype(BF16), preferred_element_type=F32)
    up = jnp.dot(xb, wu_ref[0].astype(BF16), preferred_element_type=F32)
    act = (jax.nn.silu(gate) * up).astype(BF16)
    ys_ref[...] = jnp.dot(act, wd_ref[0].astype(BF16), preferred_element_type=F32) * sw_ref[...]


def _expert_ffn(block_e, xs, slot_w, w_gate, w_up, w_down, bm):
    n_slots, d = xs.shape
    ff = w_gate.shape[2]
    grid_spec = pltpu.PrefetchScalarGridSpec(
        num_scalar_prefetch=1,
        grid=(n_slots // bm,),
        in_specs=[pl.BlockSpec((bm, d), lambda b, be: (b, 0)),
                  pl.BlockSpec((bm, 1), lambda b, be: (b, 0)),
                  pl.BlockSpec((1, d, ff), lambda b, be: (be[b], 0, 0)),
                  pl.BlockSpec((1, d, ff), lambda b, be: (be[b], 0, 0)),
                  pl.BlockSpec((1, ff, d), lambda b, be: (be[b], 0, 0))],
        out_specs=pl.BlockSpec((bm, d), lambda b, be: (b, 0)),
    )
    return pl.pallas_call(
        _expert_kernel,
        grid_spec=grid_spec,
        out_shape=jax.ShapeDtypeStruct((n_slots, d), F32),
        compiler_params=_cparams(("arbitrary",)),
        name="expert_ffn",
    )(block_e, xs, slot_w, w_gate, w_up, w_down)


def _nt(a, b):
    return lax.dot_general(a, b, (((1,), (1,)), ((), ())), preferred_element_type=F32)


def _flash_rows(s, v, m_prev, l_prev, acc_prev, v_transposed=False):
    m_new = jnp.maximum(m_prev, jnp.max(s, axis=-1, keepdims=True))
    alpha = jnp.exp(m_prev - m_new)
    p = jnp.exp(s - m_new)
    l_new = alpha * l_prev + jnp.sum(p, axis=-1, keepdims=True)
    pb = p.astype(BF16)
    pv = _nt(pb, v) if v_transposed else jnp.dot(pb, v, preferred_element_type=F32)
    return m_new, l_new, alpha * acc_prev + pv


def _page_copies(pt_ref, first_page, n, src_hbm, dst, sem, slab):
    out = []
    for k in range(n):
        p0 = pl.multiple_of(pt_ref[first_page + k] * slab, slab)
        out.append(pltpu.make_async_copy(src_hbm.at[pl.ds(p0, slab), :], dst.at[pl.ds(k * slab, slab), :], sem))
    return out


def _diff_sample_kernel(pt_ref, q_ref, kc_hbm, vc_hbm, kn_ref, vn_ref, lam_ref, g_ref, o_ref,
                        kbuf, vbuf, sem, m_ref, l_ref, acc_ref, *, page, ppc, n_chunks, lam_init):
    b = pl.program_id(0)
    c = pl.program_id(1)
    step = b * n_chunks + c
    total = pl.num_programs(0) * n_chunks
    t = q_ref.shape[1]
    kslab = WB * page // LANES
    vslab = page * DIFF_HEADS

    def copies(s, slot):
        first = s * ppc
        return (_page_copies(pt_ref, first, ppc, kc_hbm, kbuf.at[slot], sem.at[0, slot], kslab)
                + _page_copies(pt_ref, first, ppc, vc_hbm, vbuf.at[slot], sem.at[1, slot], vslab))

    slot = step % 2

    @pl.when(step == 0)
    def _():
        for cp in copies(0, 0):
            cp.start()

    @pl.when(step + 1 < total)
    def _():
        for cp in copies(step + 1, 1 - slot):
            cp.start()

    @pl.when(c == 0)
    def _():
        m_ref[...] = jnp.full_like(m_ref, NEG)
        l_ref[...] = jnp.zeros_like(l_ref)
        acc_ref[...] = jnp.zeros_like(acc_ref)

    for cp in copies(step, slot):
        cp.wait()

    q = q_ref[0].astype(F32)
    lane = lax.broadcasted_iota(jnp.int32, (t, 2 * HEAD_DIM), 1)

    def q_pair(h):
        qh = q[:, 2 * HEAD_DIM * h:2 * HEAD_DIM * (h + 1)]
        return jnp.concatenate([jnp.where(lane < HEAD_DIM, qh, 0.0), jnp.where(lane >= HEAD_DIM, qh, 0.0)],
                               axis=0).astype(BF16)

    two_hd = 2 * HEAD_DIM
    for h in range(DIFF_HEADS):
        kt = jnp.concatenate([kbuf[slot, kslab * k + two_hd * h:kslab * k + two_hd * (h + 1), :]
                              for k in range(ppc)], axis=1).astype(BF16)
        v2 = vbuf[slot, pl.ds(h, ppc * page, stride=DIFF_HEADS), :].astype(BF16)
        s = jnp.dot(q_pair(h), kt, preferred_element_type=F32)
        m, l, acc = _flash_rows(s, v2, m_ref[h], l_ref[h], acc_ref[h])
        m_ref[h] = m
        l_ref[h] = l
        acc_ref[h] = acc

    @pl.when(c == n_chunks - 1)
    def _():
        lam = _diff_lambda(lam_ref, lam_init)
        trow = lax.broadcasted_iota(jnp.int32, (2 * t, t), 0) % t
        tcol = lax.broadcasted_iota(jnp.int32, (2 * t, t), 1)
        for h in range(DIFF_HEADS):
            kn = kn_ref[0][:, 2 * HEAD_DIM * h:2 * HEAD_DIM * (h + 1)].astype(BF16)
            vn = vn_ref[0][:, DIFF_VDIM * h:DIFF_VDIM * (h + 1)].astype(BF16)
            s = jnp.where(tcol <= trow, _nt(q_pair(h), kn), NEG)
            m, l, acc = _flash_rows(s, vn, m_ref[h], l_ref[h], acc_ref[h])
            o = acc / l
            a = o[:t] - lam * o[t:]
            y = a * lax.rsqrt(jnp.mean(a * a, axis=-1, keepdims=True) + EPS) * g_ref[...] * (1.0 - lam_init)
            o_ref[0, :, DIFF_VDIM * h:DIFF_VDIM * (h + 1)] = y.astype(o_ref.dtype)


def _diff_sample(page_table, q3, kcache, vcache, kn3, vn3, diff_lambda, subln_g, lam_init, page, ppc):
    db, t, _ = q3.shape
    n_pages = page_table.shape[1]
    assert n_pages % ppc == 0 and t == 8
    n_chunks = n_pages // ppc
    tks = ppc * page
    per_b = lambda w: pl.BlockSpec((1, t, w), lambda b, c, pt: (b, 0, 0))
    const = lambda shape: pl.BlockSpec(shape, lambda b, c, pt: (0,) * len(shape))
    grid_spec = pltpu.PrefetchScalarGridSpec(
        num_scalar_prefetch=1,
        grid=(db, n_chunks),
        in_specs=[per_b(WB), pl.BlockSpec(memory_space=pl.ANY), pl.BlockSpec(memory_space=pl.ANY),
                  per_b(WB), per_b(WV), const(diff_lambda.shape), const((1, DIFF_VDIM))],
        out_specs=per_b(WV),
        scratch_shapes=[pltpu.VMEM((2, tks * WB // LANES, LANES), F32),
                        pltpu.VMEM((2, tks * DIFF_HEADS, DIFF_VDIM), F32),
                        pltpu.SemaphoreType.DMA((2, 2)),
                        pltpu.VMEM((DIFF_HEADS, 2 * t, 1), F32), pltpu.VMEM((DIFF_HEADS, 2 * t, 1), F32),
                        pltpu.VMEM((DIFF_HEADS, 2 * t, DIFF_VDIM), F32)],
    )
    return pl.pallas_call(
        functools.partial(_diff_sample_kernel, page=page, ppc=ppc, n_chunks=n_chunks, lam_init=lam_init),
        grid_spec=grid_spec,
        out_shape=jax.ShapeDtypeStruct((db, t, WV), BF16),
        compiler_params=_cparams(("arbitrary", "arbitrary")),
        name="diff_sample",
    )(page_table.reshape(-1), q3, kcache, vcache, kn3, vn3, diff_lambda, subln_g.reshape(1, DIFF_VDIM))


def _softmax_rows(s, valid):
    sm = jnp.where(valid, s, NEG)
    m = jnp.max(sm, axis=-1, keepdims=True)
    e = jnp.where(valid, jnp.exp(sm - m), 0.0)
    return e / jnp.maximum(jnp.sum(e, axis=-1, keepdims=True), 1e-30)


def _select_blocks_rows(score, blk):
    big = jnp.int32(2 ** 30)

    def body(_, carry):
        sc, bias = carry
        mx = jnp.max(sc, axis=-1, keepdims=True)
        first = jnp.min(jnp.where(sc == mx, blk, big), axis=-1, keepdims=True)
        hit = blk == first
        return jnp.where(hit, -3e38, sc), jnp.where(hit, 0.0, bias)

    return lax.fori_loop(0, TOP_N, body, (score, jnp.full(score.shape, NEG, F32)))[1]


def _pad_rows(x, rows):
    return jnp.concatenate([x, jnp.zeros((rows - x.shape[0], x.shape[1]), x.dtype)], axis=0)


def _nsa_sample_kernel(pt_ref, q_ref, qr_ref, ga_ref, cache_hbm, new_ref, wst_ref, wnew_ref,
                       w1k_ref, w1v_ref, ck_ref, cv_ref, w2k_ref, w2v_ref, impt_ref, exp_ref, o_ref,
                       ring, kctok, vctok, sem, *, page, n_pages, ppc, past, n_sel):
    b = pl.program_id(0)
    nb = pl.num_programs(0)
    t = q_ref.shape[1]
    g4 = NSA_GROUP
    n_sub = past // CMP_STRIDE
    n_cmp = n_sub - 1
    n_selp = impt_ref.shape[1]
    slab = 4 * KV * page // LANES
    half = slab // 2
    n_chunks = n_pages // ppc
    n_jobs = 2 * n_chunks
    tks = ppc * page

    def copies(bb, jj, slot):
        first = bb * n_pages + (jj % n_chunks) * ppc
        out = []
        for k in range(ppc):
            p0 = pl.multiple_of(pt_ref[first + k] * slab + half * (jj // n_chunks), half)
            out.append(pltpu.make_async_copy(cache_hbm.at[pl.ds(p0, half), :],
                                             ring.at[slot, pl.ds(k * half, half), :], sem.at[slot]))
        return out

    def begin(jj):
        slot = jj % 2
        if jj + 1 < n_jobs:
            for cp in copies(b, jj + 1, 1 - slot):
                cp.start()
        else:
            @pl.when(b + 1 < nb)
            def _():
                for cp in copies(b + 1, 0, 1 - slot):
                    cp.start()
        for cp in copies(b, jj, slot):
            cp.wait()
        return slot

    @pl.when(b == 0)
    def _():
        for cp in copies(0, 0, 0):
            cp.start()

    for jj in range(n_chunks):
        slot = begin(jj)
        for k in range(ppc):
            r0 = (jj * ppc + k) * page
            kctok[r0:r0 + page, :] = ring[slot, k * half:k * half + KV, :].T
            vctok[r0:r0 + page, :] = ring[slot, k * half + KV:(k + 1) * half, :].T
    abk = _compress_ab(kctok, w1k_ref, n_sub)
    abv = _compress_ab(vctok, w1v_ref, n_sub)
    kcmp = jnp.dot(_compress_hidden(abk, ck_ref[...]), w2k_ref[...], preferred_element_type=F32).astype(BF16)
    vcmp = jnp.dot(_compress_hidden(abv, cv_ref[...]), w2v_ref[...], preferred_element_type=F32).astype(BF16)

    q = q_ref[0].astype(F32)
    qr = qr_ref[0].astype(F32)
    ga = ga_ref[0]
    lane = lax.broadcasted_iota(jnp.int32, (t, LANES), 1)
    trow = lax.broadcasted_iota(jnp.int32, (g4 * t, 1), 0) % t
    qpos = past + trow
    qpos_t = past + lax.broadcasted_iota(jnp.int32, (t, 1), 0)

    def to_half(x, have, want):
        return x if have == want else pltpu.roll(x, HEAD_DIM, axis=1)

    def q_rows(qq, h):
        keep = (lane >= HEAD_DIM * h) & (lane < HEAD_DIM * (h + 1))
        rows = []
        for g in range(g4):
            hd = g4 * h + g
            tile = to_half(qq[:, LANES * (hd // 2):LANES * (hd // 2 + 1)], hd % 2, h)
            rows.append(jnp.where(keep, tile, 0.0))
        return jnp.concatenate(rows, axis=0).astype(BF16)

    new = new_ref[0]
    ksn = _pad_rows(new[:, 2 * KV:3 * KV], LANES).astype(BF16)
    vsn = _pad_rows(new[:, 3 * KV:4 * KV], LANES).astype(BF16)
    kwst = wst_ref[0][:KV, :].astype(BF16)
    vwst = wst_ref[0][KV:, :].astype(BF16)
    kwn = _pad_rows(wnew_ref[0][:, :KV], LANES).astype(BF16)
    vwn = _pad_rows(wnew_ref[0][:, KV:], LANES).astype(BF16)
    wbuf = wst_ref.shape[2]
    ncol = lax.broadcasted_iota(jnp.int32, (g4 * t, LANES), 1)
    new_ok = (ncol < t) & (ncol <= trow)
    blocks_per_chunk = tks // SEL_BLOCK
    out_tiles = [[None, None] for _ in range(WA // LANES)]
    o_cmps, selbiases, qrhs = [], [], []

    for h in range(NSA_KV_HEADS):
        s = _nt(q_rows(q, h), kcmp)
        nidx = lax.broadcasted_iota(jnp.int32, s.shape, 1)
        p = _softmax_rows(s, (nidx * CMP_STRIDE + (CMP_BLOCK - 1) <= qpos) & (nidx < n_cmp))
        o_cmp = jnp.dot(p.astype(BF16), vcmp, preferred_element_type=F32)
        psum = p[:t]
        for g in range(1, g4):
            psum = psum + p[g * t:(g + 1) * t]
        imp = jnp.zeros((t, n_selp), F32)
        for part in _split3(psum):
            imp += jnp.dot(part, impt_ref[...], preferred_element_type=F32)
        blk = lax.broadcasted_iota(jnp.int32, (t, n_selp), 1)
        cur = qpos_t // SEL_BLOCK
        forced = (blk == 0) | (blk == cur) | (blk == cur - 1)
        score = jnp.where(blk > cur, -1.0, jnp.where(forced, FORCED_SCORE, imp))
        score = jnp.where(blk < n_sel, score, -3e38)
        selbiases.append(_select_blocks_rows(score, blk))
        o_cmps.append(o_cmp)
        qrhs.append(q_rows(qr, h))

    stats = [(jnp.full((g4 * t, 1), NEG, F32), jnp.zeros((g4 * t, 1), F32), jnp.zeros((g4 * t, LANES), F32))
             for _ in range(NSA_KV_HEADS)]
    for c in range(n_chunks):
        slot = begin(n_chunks + c)
        kst = jnp.concatenate([ring[slot, k * half:k * half + KV, :] for k in range(ppc)], axis=1).astype(BF16)
        vst = jnp.concatenate([ring[slot, k * half + KV:(k + 1) * half, :] for k in range(ppc)],
                              axis=1).astype(BF16)
        b0 = c * blocks_per_chunk
        for h in range(NSA_KV_HEADS):
            tile = selbiases[h][:, LANES * (b0 // LANES):LANES * (b0 // LANES + 1)].astype(BF16)
            bias = jnp.dot(tile, exp_ref[(b0 % LANES) // blocks_per_chunk], preferred_element_type=F32)
            s = jnp.dot(qrhs[h], kst, preferred_element_type=F32) + jnp.concatenate([bias] * g4, axis=0)
            stats[h] = _flash_rows(s, vst, *stats[h], v_transposed=True)

    for h in range(NSA_KV_HEADS):
        qrh, selbias, o_cmp = qrhs[h], selbiases[h], o_cmps[h]
        nb_blk = past // SEL_BLOCK
        bias_new = jnp.concatenate([selbias[:, nb_blk:nb_blk + 1]] * g4, axis=0)
        s = jnp.where(new_ok, _nt(qrh, ksn) + bias_new, NEG)
        m, l, av = _flash_rows(s, vsn, *stats[h])
        o_sel = av / jnp.maximum(l, 1e-30)

        s = jnp.concatenate([jnp.dot(qrh, kwst, preferred_element_type=F32), _nt(qrh, kwn)], axis=1)
        widx = lax.broadcasted_iota(jnp.int32, s.shape, 1)
        kpos = past - wbuf + widx
        valid = (kpos <= qpos) & (kpos >= qpos - WINDOW) & (widx < wbuf + t)
        p = _softmax_rows(s, valid).astype(BF16)
        o_win = _nt(p[:, :wbuf], vwst) + jnp.dot(p[:, wbuf:], vwn, preferred_element_type=F32)

        for g in range(g4):
            hd = g4 * h + g
            r = 3 * hd
            rs = slice(g * t, (g + 1) * t)
            o = ga[:, r:r + 1] * o_cmp[rs] + ga[:, r + 1:r + 2] * o_sel[rs] + ga[:, r + 2:r + 3] * o_win[rs]
            out_tiles[hd // 2][hd % 2] = to_half(o, h, hd % 2)

    for k, (lo, hi) in enumerate(out_tiles):
        o_ref[0, :, LANES * k:LANES * (k + 1)] = jnp.where(lane < HEAD_DIM, lo, hi).astype(o_ref.dtype)


def _nsa_sample(page_table, q3, qr3, ga3, cache, new3, win_state, wnew3, cw, page, ppc):
    db, t, _ = q3.shape
    n_pages = page_table.shape[1]
    past = n_pages * page
    wbuf = win_state.shape[2]
    w1k, w1v, ck, cv, w2k, w2v, _ = cw
    n_sub = past // CMP_STRIDE
    n_sel = -(-(past + t) // SEL_BLOCK)
    n_selp = -(-n_sel // LANES) * LANES
    tks = ppc * page
    blocks_per_chunk = tks // SEL_BLOCK
    assert t == 8 and (past + t) // CMP_STRIDE == n_sub and n_pages % ppc == 0 and past % SEL_BLOCK == 0
    assert LANES % blocks_per_chunk == 0 and wbuf == WINDOW and t <= SEL_BLOCK and page == LANES
    impt = jnp.pad(_importance_matrix(n_sel, n_sub).T, ((0, 0), (0, n_selp - n_sel)))
    m = jnp.arange(LANES // blocks_per_chunk)[:, None, None]
    j = jnp.arange(LANES)[None, :, None]
    u = jnp.arange(tks)[None, None, :]
    expand = (j == blocks_per_chunk * m + u // SEL_BLOCK).astype(BF16)
    per_b = lambda r, w: pl.BlockSpec((1, r, w), lambda b, pt: (b, 0, 0))
    const = lambda a: pl.BlockSpec(a.shape, lambda b, pt: (0,) * a.ndim)
    grid_spec = pltpu.PrefetchScalarGridSpec(
        num_scalar_prefetch=1,
        grid=(db,),
        in_specs=[per_b(t, WA), per_b(t, WA), per_b(t, LANES), pl.BlockSpec(memory_space=pl.ANY),
                  per_b(t, 4 * KV), per_b(2 * KV, wbuf), per_b(t, 2 * KV),
                  const(w1k), const(w1v), const(ck), const(cv), const(w2k), const(w2v), const(impt),
                  const(expand)],
        out_specs=per_b(t, WA),
        scratch_shapes=[pltpu.VMEM((2, ppc * 2 * KV, LANES), F32), pltpu.VMEM((past, KV), F32),
                        pltpu.VMEM((past, KV), F32), pltpu.SemaphoreType.DMA((2,))],
    )
    return pl.pallas_call(
        functools.partial(_nsa_sample_kernel, page=page, n_pages=n_pages, ppc=ppc, past=past, n_sel=n_sel),
        grid_spec=grid_spec,
        out_shape=jax.ShapeDtypeStruct((db, t, WA), BF16),
        compiler_params=_cparams(("arbitrary",)),
        name="nsa_sample",
    )(page_table.reshape(-1), q3, qr3, ga3, cache, new3, win_state, wnew3, w1k, w1v, ck, cv, w2k, w2v, impt,
      expand)


def _final_kernel(x_ref, y_ref, g_ref, o_ref):
    x = x_ref[...] + y_ref[...]
    o_ref[...] = x * lax.rsqrt(jnp.mean(x * x, axis=-1, keepdims=True) + EPS) * g_ref[...]


def _final_norm(x1, moe, row0, g, tm):
    n, d = x1.shape
    assert n % tm == 0 and row0 % tm == 0
    off = row0 // tm
    return pl.pallas_call(
        _final_kernel,
        grid=(n // tm,),
        in_specs=[pl.BlockSpec((tm, d), lambda i: (i, 0)), pl.BlockSpec((tm, d), lambda i: (i + off, 0)),
                  _full((1, d))],
        out_specs=pl.BlockSpec((tm, d), lambda i: (i, 0)),
        out_shape=jax.ShapeDtypeStruct((n, d), F32),
        compiler_params=_cparams(("parallel",)),
        name="final_norm",
    )(x1, moe, g.reshape(1, d))


MOE_ROWS = 256


def _dispatch_plan(experts, weights, bm):
    n_tok = experts.shape[0]
    n_asg = n_tok * EXPERT_TOP_K
    flat_e = experts.reshape(-1)
    flat_t = jnp.repeat(jnp.arange(n_tok, dtype=jnp.int32), EXPERT_TOP_K)
    order = jnp.argsort(flat_e)
    e_sorted = flat_e[order]
    counts = jnp.bincount(flat_e, length=N_EXPERTS)
    start = jnp.cumsum(counts) - counts
    padded = (counts + bm - 1) // bm * bm
    pend = jnp.cumsum(padded)
    dest = (pend - padded)[e_sorted] + jnp.arange(n_asg) - start[e_sorted]
    n_blocks = -(-n_asg // bm) + N_EXPERTS
    n_slots = n_blocks * bm
    slot_tok = jnp.zeros((n_slots,), jnp.int32).at[dest].set(flat_t[order])
    slot_w = jnp.zeros((n_slots,), F32).at[dest].set(weights.reshape(-1)[order])
    block_e = jnp.minimum(jnp.searchsorted(pend, jnp.arange(n_blocks) * bm, side='right'), N_EXPERTS - 1)
    return slot_tok, slot_w, block_e.astype(jnp.int32)


def kernel(x_prompt, x_sample, cache_nsa_kv, cache_diff_k, cache_diff_v, state_nsa_win_kv, page_table,
           norm_mix_g, w_in, nsa_cmp_pos, nsa_cmp_k_w1, nsa_cmp_k_w2, nsa_cmp_v_w1, nsa_cmp_v_w2,
           diff_lambda, diff_subln_g, w_proj_a, w_proj_b, w_out, norm_ffn_g,
           router_group_w, router_group_b, router_expert_w, router_expert_b,
           expert_w_gate, expert_w_up, expert_w_down, norm_final_g):
    depth = w_in.shape[0]
    bsz, seq, d = x_prompt.shape
    db, t, _ = x_sample.shape
    n_pool, page = cache_nsa_kv.shape[1:3]
    past = page_table.shape[1] * page
    wbuf = state_nsa_win_kv.shape[2]
    assert depth == 1 and bsz == 1
    l = 0
    lam_init = 0.8 - 0.6 * math.exp(-0.3 * l)
    w = _split_w_in(w_in[l], d)
    cw = _compress_weights(nsa_cmp_pos[l], nsa_cmp_k_w1[l], nsa_cmp_k_w2[l], nsa_cmp_v_w1[l], nsa_cmp_v_w2[l])
    wa, wb, wo = w_proj_a[l].astype(BF16), w_proj_b[l].astype(BF16), w_out[l].astype(BF16)
    rw = jnp.pad(jnp.concatenate([router_group_w[l], router_expert_w[l]], axis=1),
                 ((0, 0), (0, ROUTE_COLS - N_GROUPS - N_EXPERTS)))
    rb = jnp.pad(jnp.concatenate([router_group_b[l], router_expert_b[l]]),
                 (0, ROUTE_COLS - N_GROUPS - N_EXPERTS)).reshape(1, ROUTE_COLS)
    rwh = rw.astype(BF16)
    rwl = (rw - rwh.astype(F32)).astype(BF16)

    xp = x_prompt.reshape(seq, d)
    (nsa_p, win_p, dk_p, dv_p, gm_p, ks_b, kw_b, kb_b,
     qat, qart, qbt, vst, vwt, vbt, gat) = _project(xp, jnp.arange(seq), norm_mix_g[l], w, 512, True)
    kcmp, _, vcmpt = _compress(nsa_p, cw, min(256, seq // CMP_STRIDE))
    oat = _nsa_prompt(qat, qart, gat, kcmp, vcmpt, ks_b, vst, kw_b, vwt, 128, 512)
    obt = _diff_prompt(qbt, kb_b, vbt, diff_lambda[l], diff_subln_g[l], lam_init, 512, 512)
    x1p, h2p, re_p, rw_p = _merge(oat, obt, gm_p, xp, wa, wb, wo, norm_ffn_g[l], rwh, rwl, rb, 512, True)

    ns = db * t
    xs = x_sample.reshape(ns, d)
    pos_s = past + jnp.arange(ns) % t
    (nsa_s, win_s, dk_s, dv_s, gm_s, _, _, _, qa_s, qar_s, qb_s, ga_s) = _project(
        xs, pos_s, norm_mix_g[l], w, ns, False)
    r3 = lambda a: a.reshape(db, t, a.shape[-1])
    slabs = lambda c: c.transpose(0, 2, 3, 4, 1).reshape(-1, page)
    state_t = state_nsa_win_kv[l].transpose(0, 2, 3, 4, 1).reshape(db, 2 * KV, wbuf)
    oa_s = _nsa_sample(page_table, r3(qa_s), r3(qar_s), r3(ga_s), slabs(cache_nsa_kv[l]),
                       r3(nsa_s), state_t, r3(win_s), cw, page, 16)
    ob_s = _diff_sample(page_table, r3(qb_s), slabs(cache_diff_k[l]),
                        cache_diff_v[l].reshape(-1, DIFF_VDIM), r3(dk_s), r3(dv_s),
                        diff_lambda[l], diff_subln_g[l], lam_init, page, 16)
    x1s, h2s, re_s, rw_s = _merge(oa_s.reshape(ns, -1), ob_s.reshape(ns, -1), gm_s, xs, wa, wb, wo,
                                  norm_ffn_g[l], rwh, rwl, rb, ns, False)

    h2 = jnp.concatenate([h2p, h2s], axis=0)
    experts = jnp.concatenate([re_p[:, :EXPERT_TOP_K], re_s[:, :EXPERT_TOP_K]], axis=0)
    weights = jnp.concatenate([rw_p[:, :EXPERT_TOP_K], rw_s[:, :EXPERT_TOP_K]], axis=0)
    slot_tok, slot_w, block_e = _dispatch_plan(experts, weights, MOE_ROWS)
    ys = _expert_ffn(block_e, h2[slot_tok], slot_w[:, None], expert_w_gate[l], expert_w_up[l], expert_w_down[l],
                     MOE_ROWS)
    moe = jnp.zeros((seq + ns, d), F32).at[slot_tok].add(ys)
    y_prompt = _final_norm(x1p, moe, 0, norm_final_g, 512)
    y_sample = _final_norm(x1s, moe, seq, norm_final_g, ns)

    wn = min(WINDOW, seq)
    win_all_t = jnp.concatenate([state_t, r3(win_s).transpose(0, 2, 1)], axis=2)[:, :, -min(WINDOW, past + t):]
    win_all = win_all_t.reshape(db, 2, NSA_KV_HEADS, HEAD_DIM, -1).transpose(0, 4, 1, 2, 3)
    kvs = (4, NSA_KV_HEADS, HEAD_DIM)
    dks = (DIFF_HEADS, 2, HEAD_DIM)
    dvs = (DIFF_HEADS, DIFF_VDIM)
    return (y_prompt.reshape(1, seq, d), y_sample.reshape(db, t, d),
            nsa_p.reshape((1, 1, seq) + kvs), nsa_s.reshape((1, db, t) + kvs),
            dk_p.reshape((1, 1, seq) + dks), dk_s.reshape((1, db, t) + dks),
            dv_p.reshape((1, 1, seq) + dvs), dv_s.reshape((1, db, t) + dvs),
            win_p[seq - wn:].reshape(1, 1, wn, 2, NSA_KV_HEADS, HEAD_DIM),
            win_all[None])
```

```python
import functools
import math

import jax
import jax.numpy as jnp
from jax import lax
from jax.experimental import pallas as pl
from jax.experimental.pallas import tpu as pltpu

F32 = jnp.float32
BF16 = jnp.bfloat16

HEAD_DIM = 64
HALF = HEAD_DIM // 2
NSA_HEADS = 8
NSA_KV_HEADS = 2
NSA_GROUP = NSA_HEADS // NSA_KV_HEADS
CMP_STRIDE = 16
CMP_BLOCK = 2 * CMP_STRIDE
SEL_BLOCK = 64
SEL_PER_CMP = SEL_BLOCK // CMP_STRIDE
TOP_N = 16
WINDOW = 512
FORCED_SCORE = 1e4
DIFF_HEADS = 4
DIFF_VDIM = 2 * HEAD_DIM
N_GROUPS = 4
EXPERTS_PER_GROUP = 8
N_EXPERTS = N_GROUPS * EXPERTS_PER_GROUP
EXPERT_TOP_K = 2
ROPE_THETA = 10000.0
EPS = 1e-6
NEG = -1e30
SCALE = HEAD_DIM ** -0.5 * math.log2(math.e)

LANES = 128
VMEM_LIMIT = 56 * 1024 * 1024

WA = NSA_HEADS * HEAD_DIM
KV = NSA_KV_HEADS * HEAD_DIM
WB = DIFF_HEADS * 2 * HEAD_DIM
WV = DIFF_HEADS * DIFF_VDIM
N_GATE = 3 * NSA_HEADS


def _cparams(sem, flags=None):
    return pltpu.CompilerParams(dimension_semantics=sem, vmem_limit_bytes=VMEM_LIMIT, flags=flags)


def _full(shape):
    return pl.BlockSpec(shape, lambda *_: (0,) * len(shape))


def _swap_halves(t):
    lane = lax.broadcasted_iota(jnp.int32, t.shape, 1)
    fwd = pltpu.roll(t, LANES - HALF, axis=1)
    bwd = pltpu.roll(t, HALF, axis=1)
    return jnp.where(lane % HEAD_DIM < HALF, fwd, bwd)


def _rope_rows(t, cos, sin):
    outs = []
    for a in range(0, t.shape[1], LANES):
        x = t[:, a:a + LANES]
        outs.append(x * cos + _swap_halves(x) * sin)
    return outs[0] if len(outs) == 1 else jnp.concatenate(outs, axis=1)


def _rope_cols(t, cos, sin):
    outs = []
    for a in range(0, t.shape[0], HEAD_DIM):
        x1 = t[a:a + HALF]
        x2 = t[a + HALF:a + HEAD_DIM]
        outs.append(x1 * cos - x2 * sin)
        outs.append(x2 * cos + x1 * sin)
    return jnp.concatenate(outs, axis=0)


def _proj_kernel(x_ref, g_ref, wn_ref, wt_ref, cosn_ref, sinn_ref, cost_ref, sint_ref,
                 nsa_ref, win_ref, dk_ref, dv_ref, gm_ref, ksb_ref, kwb_ref, kbb_ref, *rest, transposed_q):
    x = x_ref[...]
    h = x * lax.rsqrt(jnp.mean(x * x, axis=-1, keepdims=True) + EPS) * g_ref[...]
    hb = h.astype(BF16)
    cosn = cosn_ref[...]
    sinn = sinn_ref[...]

    def mm(a, b):
        return jnp.dot(hb, wn_ref[:, a:b], preferred_element_type=F32)

    c = 0
    y = mm(c, c + 4 * KV)
    ks = _rope_rows(y[:, 2 * KV:3 * KV], cosn, sinn)
    nsa_ref[:, :2 * KV] = y[:, :2 * KV]
    nsa_ref[:, 2 * KV:3 * KV] = ks
    nsa_ref[:, 3 * KV:] = y[:, 3 * KV:]
    ksb_ref[...] = ks.astype(BF16)
    c += 4 * KV
    y = mm(c, c + 2 * KV)
    kw = _rope_rows(y[:, :KV], cosn, sinn)
    win_ref[:, :KV] = kw
    win_ref[:, KV:] = y[:, KV:]
    kwb_ref[...] = kw.astype(BF16)
    c += 2 * KV
    kb = _rope_rows(mm(c, c + WB), cosn, sinn)
    dk_ref[...] = kb
    kbb_ref[...] = kb.astype(BF16)
    c += WB
    dv_ref[...] = mm(c, c + WV)
    c += WV
    d_model = x.shape[1]
    gm_ref[...] = jax.nn.sigmoid(mm(c, c + 2 * d_model))
    c += 2 * d_model

    if transposed_q:
        qat_ref, qart_ref, qbt_ref, vst_ref, vwt_ref, vbt_ref, gat_ref = rest
        cost = cost_ref[...]
        sint = sint_ref[...]

        def mmt(a, b):
            return lax.dot_general(wt_ref[a:b, :], hb, (((1,), (1,)), ((), ())), preferred_element_type=F32)

        r = 0
        qa = mmt(r, r + WA) * SCALE
        qat_ref[...] = qa.astype(BF16)
        qart_ref[...] = _rope_cols(qa, cost, sint).astype(BF16)
        r += WA
        qbt_ref[...] = _rope_cols(mmt(r, r + WB) * SCALE, cost, sint).astype(BF16)
        r += WB
        vst_ref[...] = mmt(r, r + KV).astype(BF16)
        r += KV
        vwt_ref[...] = mmt(r, r + KV).astype(BF16)
        r += KV
        vbt_ref[...] = mmt(r, r + WV).astype(BF16)
        r += WV
        gat_ref[...] = jax.nn.sigmoid(mmt(r, r + 32))
    else:
        qa_ref, qar_ref, qb_ref, ga_ref = rest
        qa = mm(c, c + WA) * SCALE
        qa_ref[...] = qa.astype(BF16)
        qar_ref[...] = _rope_rows(qa, cosn, sinn).astype(BF16)
        c += WA
        qb_ref[...] = _rope_rows(mm(c, c + WB) * SCALE, cosn, sinn).astype(BF16)
        c += WB
        ga_ref[...] = jax.nn.sigmoid(mm(c, c + LANES))


def _split_w_in(w_in, d_model):
    sizes = [WA, KV, KV, KV, KV, KV, KV, N_GATE, WB, WB, WV, 2 * d_model]
    offs = [0]
    for s in sizes:
        offs.append(offs[-1] + s)
    names = ["qa", "kc", "vc", "ks", "vs", "kw", "vw", "ga", "qb", "kb", "vb", "gm"]
    return {n: w_in[:, offs[i]:offs[i + 1]] for i, n in enumerate(names)}


def _rope_tables(pos):
    inv = ROPE_THETA ** (-jnp.arange(HALF, dtype=F32) / HALF)
    ang = pos.astype(F32)[:, None] * inv[None, :]
    cos, sin = jnp.cos(ang), jnp.sin(ang)
    cosn = jnp.tile(cos, (1, LANES // HALF))
    sinn = jnp.tile(jnp.concatenate([-sin, sin], axis=1), (1, LANES // HEAD_DIM))
    return cosn, sinn, cos.T, sin.T


def _project(x, pos, norm_g, w, tm, transposed_q):
    n, d = x.shape
    assert n % tm == 0
    cosn, sinn, cost, sint = _rope_tables(pos)
    wn_parts = [w["kc"], w["vc"], w["ks"], w["vs"], w["kw"], w["vw"], w["kb"], w["vb"], w["gm"]]
    ga_pad = jnp.pad(w["ga"], ((0, 0), (0, LANES - N_GATE)))
    if transposed_q:
        wt = jnp.concatenate([w["qa"], w["qb"], w["vs"], w["vw"], w["vb"], ga_pad[:, :32]], axis=1).T.astype(BF16)
    else:
        wn_parts += [w["qa"], w["qb"], ga_pad]
        wt = jnp.zeros((8, d), BF16)
    wn = jnp.concatenate(wn_parts, axis=1).astype(BF16)

    row = lambda c: pl.BlockSpec((tm, c), lambda i: (i, 0))
    col = lambda r: pl.BlockSpec((r, tm), lambda i: (0, i))
    out_shape = [jax.ShapeDtypeStruct((n, 4 * KV), F32), jax.ShapeDtypeStruct((n, 2 * KV), F32),
                 jax.ShapeDtypeStruct((n, WB), F32), jax.ShapeDtypeStruct((n, WV), F32),
                 jax.ShapeDtypeStruct((n, 2 * d), F32), jax.ShapeDtypeStruct((n, KV), BF16),
                 jax.ShapeDtypeStruct((n, KV), BF16), jax.ShapeDtypeStruct((n, WB), BF16)]
    out_specs = [row(4 * KV), row(2 * KV), row(WB), row(WV), row(2 * d), row(KV), row(KV), row(WB)]
    if transposed_q:
        out_shape += [jax.ShapeDtypeStruct((WA, n), BF16), jax.ShapeDtypeStruct((WA, n), BF16),
                      jax.ShapeDtypeStruct((WB, n), BF16), jax.ShapeDtypeStruct((KV, n), BF16),
                      jax.ShapeDtypeStruct((KV, n), BF16), jax.ShapeDtypeStruct((WV, n), BF16),
                      jax.ShapeDtypeStruct((32, n), F32)]
        out_specs += [col(WA), col(WA), col(WB), col(KV), col(KV), col(WV), col(32)]
    else:
        out_shape += [jax.ShapeDtypeStruct((n, WA), BF16), jax.ShapeDtypeStruct((n, WA), BF16),
                      jax.ShapeDtypeStruct((n, WB), BF16), jax.ShapeDtypeStruct((n, LANES), F32)]
        out_specs += [row(WA), row(WA), row(WB), row(LANES)]
    return pl.pallas_call(
        functools.partial(_proj_kernel, transposed_q=transposed_q),
        grid=(n // tm,),
        in_specs=[row(d), _full((1, d)), _full(wn.shape), _full(wt.shape),
                  row(LANES), row(LANES), col(HALF), col(HALF)],
        out_specs=out_specs,
        out_shape=out_shape,
        compiler_params=_cparams(("parallel",)),
        name="proj",
    )(x, norm_g.reshape(1, d), wn, wt, cosn, sinn, cost, sint)


def _pad_head(qt, slot):
    z = jnp.zeros_like(qt)
    return jnp.concatenate([qt, z] if slot == 0 else [z, qt], axis=0)


SUM_ROWS = 16


def _with_ones(vt):
    return jnp.concatenate([vt, jnp.ones((SUM_ROWS, vt.shape[1]), vt.dtype)], axis=0)


def _flash_step(st, vt1, m_prev, acc_prev):
    m_new = jnp.maximum(m_prev, jnp.max(st, axis=0, keepdims=True))
    p = jnp.exp2(st - m_new).astype(BF16)
    return m_new, jnp.exp2(m_prev - m_new) * acc_prev + jnp.dot(vt1, p, preferred_element_type=F32)


def _diff_lambda(lam_ref, lam_init):
    lv = lam_ref[...]
    a = jnp.sum(lv[0:1] * lv[1:2], axis=-1, keepdims=True)
    b = jnp.sum(lv[2:3] * lv[3:4], axis=-1, keepdims=True)
    return jnp.exp(a) - jnp.exp(b) + lam_init


DIFF_AHEAD = 3


def _diff_prompt_kernel(qt_ref, k_ref, vt_ref, lam_ref, g_ref, o_ref, m_ref, acc_ref, *s_refs, tq, tk, sub,
                        lam_init):
    i = pl.program_id(0)
    j = pl.program_id(1)
    first_diag = i * tq // tk

    @pl.when(j == 0)
    def _():
        m_ref[...] = jnp.full_like(m_ref, NEG)
        acc_ref[...] = jnp.zeros_like(acc_ref)

    n_maps = 2 * DIFF_HEADS
    subs = [slice(r, r + sub) for r in range(0, tk, sub)]

    def step(causal):
        def score(hc, out):
            h, c = divmod(hc, 2)
            qt = _pad_head(qt_ref[HEAD_DIM * hc:HEAD_DIM * (hc + 1), :], c)
            m_new = m_ref[hc:hc + 1]
            for rows in subs:
                st = jnp.dot(k_ref[rows, 2 * HEAD_DIM * h:2 * HEAD_DIM * (h + 1)], qt, preferred_element_type=F32)
                if causal:
                    kpos = j * tk + rows.start + lax.broadcasted_iota(jnp.int32, (sub, tq), 0)
                    qpos = i * tq + lax.broadcasted_iota(jnp.int32, (sub, tq), 1)
                    st = jnp.where(kpos <= qpos, st, NEG)
                s_refs[hc % len(s_refs)][rows, :] = st
                m_new = jnp.maximum(m_new, jnp.max(st, axis=0, keepdims=True))
                yield
            out.append(m_new)

        def accumulate(hc, m_new):
            h = hc // 2
            acc = jnp.exp2(m_ref[hc:hc + 1] - m_new) * acc_ref[hc]
            for rows in subs:
                p = jnp.exp2(s_refs[hc % len(s_refs)][rows, :] - m_new).astype(BF16)
                vt1 = _with_ones(vt_ref[DIFF_VDIM * h:DIFF_VDIM * (h + 1), rows])
                acc = acc + jnp.dot(vt1, p, preferred_element_type=F32)
                yield
            m_ref[hc:hc + 1] = m_new
            acc_ref[hc] = acc

        m_new = []
        for hc in range(DIFF_AHEAD):
            for _ in score(hc, m_new):
                pass
        for hc in range(n_maps):
            nxt = score(hc + DIFF_AHEAD, m_new) if hc + DIFF_AHEAD < n_maps else iter(())
            for _ in accumulate(hc, m_new[hc]):
                next(nxt, None)
            for _ in nxt:
                pass

    pl.when(j < first_diag)(functools.partial(step, False))
    pl.when((j >= first_diag) & (j * tk < (i + 1) * tq))(functools.partial(step, True))

    @pl.when(j == pl.num_programs(1) - 1)
    def _():
        lam = _diff_lambda(lam_ref, lam_init)
        for h in range(DIFF_HEADS):
            a0, a1 = acc_ref[2 * h], acc_ref[2 * h + 1]
            o0 = a0[:DIFF_VDIM] / a0[DIFF_VDIM:DIFF_VDIM + 1]
            o1 = a1[:DIFF_VDIM] / a1[DIFF_VDIM:DIFF_VDIM + 1]
            a = o0 - lam * o1
            y = a * lax.rsqrt(jnp.mean(a * a, axis=0, keepdims=True) + EPS) * g_ref[...] * (1.0 - lam_init)
            o_ref[DIFF_VDIM * h:DIFF_VDIM * (h + 1), :] = y.astype(o_ref.dtype)


def _diff_prompt(qbt, kb, vbt, diff_lambda, subln_g, lam_init, tq, tk):
    n = kb.shape[0]
    assert n % tq == 0 and n % tk == 0 and tq % tk == 0
    last = lambda i, j: jnp.minimum(j, ((i + 1) * tq - 1) // tk)
    return pl.pallas_call(
        functools.partial(_diff_prompt_kernel, tq=tq, tk=tk, sub=min(tk, 2 * LANES), lam_init=lam_init),
        grid=(n // tq, n // tk),
        in_specs=[pl.BlockSpec((WB, tq), lambda i, j: (0, i)),
                  pl.BlockSpec((tk, WB), lambda i, j: (last(i, j), 0)),
                  pl.BlockSpec((WV, tk), lambda i, j: (0, last(i, j))),
                  _full(diff_lambda.shape), _full((DIFF_VDIM, 1))],
        out_specs=pl.BlockSpec((WV, tq), lambda i, j: (0, i)),
        out_shape=jax.ShapeDtypeStruct((WV, n), BF16),
        scratch_shapes=[pltpu.VMEM((2 * DIFF_HEADS, tq), F32),
                        pltpu.VMEM((2 * DIFF_HEADS, DIFF_VDIM + SUM_ROWS, tq), F32),
                        ] + [pltpu.VMEM((tk, tq), F32)] * (DIFF_AHEAD + 1),
        compiler_params=_cparams(("parallel", "arbitrary")),
        name="diff_prompt",
    )(qbt, kb, vbt, diff_lambda, subln_g.reshape(DIFF_VDIM, 1))


def _compress_weights(pos, k_w1, k_w2, v_w1, v_w2):
    hd = HEAD_DIM
    z = jnp.zeros((CMP_STRIDE, hd, hd), F32)

    def halves(w1):
        w3 = w1.reshape(CMP_BLOCK, hd, -1)
        return w3[:CMP_STRIDE], w3[CMP_STRIDE:]

    def expand(top, bot):
        rows = [[top, z, bot, z], [z, top, z, bot]]
        return jnp.concatenate([jnp.concatenate(r, axis=2) for r in rows], axis=1).astype(BF16)

    pf = pos.reshape(1, -1)
    ck, cv = pf @ k_w1, pf @ v_w1
    z2 = jnp.zeros((hd, hd), F32)
    w2k = jnp.block([[k_w2, z2], [z2, k_w2]]).astype(BF16)
    w2v = jnp.block([[v_w2, z2], [z2, v_w2]]).astype(BF16)
    return (expand(*halves(k_w1)), expand(*halves(v_w1)), jnp.concatenate([ck, ck], axis=1),
            jnp.concatenate([cv, cv], axis=1), w2k, w2v, w2v.T)


def _compress_ab(x_ref, w_ref, n_sub):
    acc = jnp.zeros((n_sub, 2 * KV), F32)
    for r in range(CMP_STRIDE):
        xr = x_ref[pl.ds(r, n_sub, stride=CMP_STRIDE), :].astype(BF16)
        acc += jnp.dot(xr, w_ref[r], preferred_element_type=F32)
    return acc


def _compress_hidden(ab, c):
    n_sub = ab.shape[0]
    nxt = pltpu.roll(ab[:, KV:], n_sub - 1, axis=0)
    return jax.nn.gelu(ab[:, :KV] + nxt + c).astype(BF16)


def _compress_ab_kernel(xk_ref, xv_ref, wk_ref, wv_ref, abk_ref, abv_ref, *, n_sub):
    abk_ref[...] = _compress_ab(xk_ref, wk_ref, n_sub)
    abv_ref[...] = _compress_ab(xv_ref, wv_ref, n_sub)


def _compress_mlp_kernel(abk_ref, abv_ref, ck_ref, cv_ref, w2k_ref, w2v_ref, w2vt_ref, kc_ref, vc_ref, vct_ref):
    gk = _compress_hidden(abk_ref[...], ck_ref[...])
    gv = _compress_hidden(abv_ref[...], cv_ref[...])
    kc_ref[...] = jnp.dot(gk, w2k_ref[...], preferred_element_type=F32).astype(BF16)
    vc_ref[...] = jnp.dot(gv, w2v_ref[...], preferred_element_type=F32).astype(BF16)
    vct_ref[...] = lax.dot_general(w2vt_ref[...], gv, (((1,), (1,)), ((), ())),
                                   preferred_element_type=F32).astype(BF16)


def _compress(kv, cw, sub_tile):
    w1k, w1v, ck, cv, w2k, w2v, w2vt = cw
    n_sub = kv.shape[0] // CMP_STRIDE
    assert n_sub % sub_tile == 0
    ab_shape = jax.ShapeDtypeStruct((n_sub, 2 * KV), F32)
    ab_spec = pl.BlockSpec((sub_tile, 2 * KV), lambda i: (i, 0))
    abk, abv = pl.pallas_call(
        functools.partial(_compress_ab_kernel, n_sub=sub_tile),
        grid=(n_sub // sub_tile,),
        in_specs=[pl.BlockSpec((sub_tile * CMP_STRIDE, KV), lambda i: (i, 0)),
                  pl.BlockSpec((sub_tile * CMP_STRIDE, KV), lambda i: (i, 1)), _full(w1k.shape), _full(w1v.shape)],
        out_specs=[ab_spec, ab_spec],
        out_shape=[ab_shape, ab_shape],
        compiler_params=_cparams(("parallel",)),
        name="compress_ab",
    )(kv, kv, w1k, w1v)
    return pl.pallas_call(
        _compress_mlp_kernel,
        out_shape=[jax.ShapeDtypeStruct((n_sub, KV), BF16), jax.ShapeDtypeStruct((n_sub, KV), BF16),
                   jax.ShapeDtypeStruct((KV, n_sub), BF16)],
        compiler_params=pltpu.CompilerParams(vmem_limit_bytes=VMEM_LIMIT),
        name="compress_mlp",
    )(abk, abv, ck, cv, w2k, w2v, w2vt)


def _importance_matrix(n_sel, n_cmp):
    j = jnp.arange(n_sel)[:, None]
    n = jnp.arange(n_cmp)[None, :]
    return ((n >= SEL_PER_CMP * j - 1) & (n <= SEL_PER_CMP * j + SEL_PER_CMP - 1)).astype(BF16)


def _split3(x):
    hi = x.astype(BF16)
    r = x - hi.astype(F32)
    mid = r.astype(BF16)
    lo = (r - mid.astype(F32)).astype(BF16)
    return hi, mid, lo


def _tile_lanes(x, k):
    return jnp.concatenate([x] * k, axis=1)


def _select_blocks(score, blk):
    big = jnp.int32(2 ** 30)

    def body(_, carry):
        sc, bias = carry
        mx = jnp.max(sc, axis=0, keepdims=True)
        first = jnp.min(jnp.where(sc == mx, blk, big), axis=0, keepdims=True)
        hit = blk == first
        return jnp.where(hit, -3e38, sc), jnp.where(hit, 0.0, bias)

    n_pick = min(TOP_N, score.shape[0])
    return lax.fori_loop(0, n_pick, body, (score, jnp.full(score.shape, NEG, F32)))[1]


def _softmax_cols(st, valid):
    sm = jnp.where(valid, st, NEG)
    m = jnp.max(sm, axis=0, keepdims=True)
    e = jnp.where(valid, jnp.exp2(sm - m), 0.0)
    return e / jnp.maximum(jnp.sum(e, axis=0, keepdims=True), 1e-30)


NSA_GROUP_CHUNKS = 4
NSA_AHEAD = 3


def _nsa_prompt_kernel(qat_ref, qart_ref, gat_ref, kc_ref, vct_ref, imp_ref, hot_ref, ks_ref, vst_ref, kw_ref,
                       vwt_ref, o_ref, bias_ref, ocmp_ref, m_ref, acc_ref, *s_refs, tq, tk, sub, n_cmp):
    i = pl.program_id(0)
    g4 = NSA_GROUP
    ncp = kc_ref.shape[0]
    n_sel = imp_ref.shape[0]
    q0 = i * tq
    qlane = q0 + lax.broadcasted_iota(jnp.int32, (1, tq), 1)

    def heads_t(ref, h):
        return jnp.concatenate([ref[HEAD_DIM * (g4 * h + g):HEAD_DIM * (g4 * h + g + 1), :] for g in range(g4)],
                               axis=1)

    for h in range(NSA_KV_HEADS):
        qt = _pad_head(heads_t(qat_ref, h), h)
        st = jnp.dot(kc_ref[...], qt, preferred_element_type=F32)
        nrow = lax.broadcasted_iota(jnp.int32, (ncp, tq), 0)
        valid = (nrow * CMP_STRIDE + (CMP_BLOCK - 1) <= qlane) & (nrow < n_cmp)
        p = _softmax_cols(st, _tile_lanes(valid, g4))
        ocmp_ref[h] = jnp.dot(vct_ref[HEAD_DIM * h:HEAD_DIM * (h + 1), :], p.astype(BF16),
                              preferred_element_type=F32)
        psum = p[:, :tq]
        for g in range(1, g4):
            psum = psum + p[:, g * tq:(g + 1) * tq]
        imp = jnp.zeros((n_sel, tq), F32)
        for part in _split3(psum):
            imp += jnp.dot(imp_ref[...], part, preferred_element_type=F32)
        blk = lax.broadcasted_iota(jnp.int32, (n_sel, tq), 0)
        cur = qlane // SEL_BLOCK
        forced = (blk == 0) | (blk == cur) | (blk == cur - 1)
        score = jnp.where(blk > cur, -1.0, jnp.where(forced, FORCED_SCORE, imp))
        bias_ref[h] = _select_blocks(score, blk)

    qrts = [_pad_head(heads_t(qart_ref, h), h) for h in range(NSA_KV_HEADS)]
    per_chunk = tk // SEL_BLOCK
    zpad = jnp.zeros((LANES - per_chunk, g4 * tq), F32)
    subs = [slice(r, r + sub) for r in range(0, tk, sub)]
    n_key_chunks = ks_ref.shape[0] // tk
    n_maps = NSA_GROUP_CHUNKS * NSA_KV_HEADS
    m_ref[...] = jnp.full_like(m_ref, NEG)
    acc_ref[...] = jnp.zeros_like(acc_ref)

    def group(c_base, masked):
        m_hist = [[m_ref[h:h + 1]] for h in range(NSA_KV_HEADS)]

        def where(k):
            cc, h = divmod(k, NSA_KV_HEADS)
            c = c_base + cc
            cl = jnp.minimum(c, n_key_chunks - 1) if masked else c
            return h, c, cl, pl.multiple_of(cl * tk, tk)

        def score(k):
            h, c, cl, k0 = where(k)
            b8 = bias_ref[h, pl.ds(pl.multiple_of(cl * per_chunk, per_chunk), per_chunk), :]
            qx = jnp.concatenate([qrts[h], jnp.concatenate([_tile_lanes(b8, g4), zpad], axis=0).astype(BF16)],
                                 axis=0)
            m_run = m_hist[h][-1]
            for rows in subs:
                kx = jnp.concatenate([ks_ref[pl.ds(k0 + rows.start, sub), :], hot_ref[rows, :]], axis=1)
                st = jnp.dot(kx, qx, preferred_element_type=F32)
                if masked:
                    kpos = c * tk + rows.start + lax.broadcasted_iota(jnp.int32, (sub, tq), 0)
                    st = jnp.where(_tile_lanes(kpos <= qlane, g4), st, NEG)
                s_refs[k % len(s_refs)][rows, :] = st
                m_run = jnp.maximum(m_run, jnp.max(st, axis=0, keepdims=True))
                yield
            m_hist[h].append(m_run)

        def accumulate(k):
            h, _, _, k0 = where(k)
            cc = k // NSA_KV_HEADS
            m_new = m_hist[h][cc + 1]
            acc = jnp.exp2(m_hist[h][cc] - m_new) * acc_ref[h]
            for rows in subs:
                p = jnp.exp2(s_refs[k % len(s_refs)][rows, :] - m_new).astype(BF16)
                vt1 = _with_ones(vst_ref[HEAD_DIM * h:HEAD_DIM * (h + 1), pl.ds(k0 + rows.start, sub)])
                acc = acc + jnp.dot(vt1, p, preferred_element_type=F32)
                yield
            acc_ref[h] = acc

        for k in range(NSA_AHEAD):
            for _ in score(k):
                pass
        for k in range(n_maps):
            nxt = score(k + NSA_AHEAD) if k + NSA_AHEAD < n_maps else iter(())
            for _ in accumulate(k):
                next(nxt, None)
            for _ in nxt:
                pass
        for h in range(NSA_KV_HEADS):
            m_ref[h:h + 1] = m_hist[h][-1]

    n_plain = (q0 // tk) // NSA_GROUP_CHUNKS

    def plain(gi, carry):
        group(gi * NSA_GROUP_CHUNKS, False)
        return carry

    lax.fori_loop(0, n_plain, plain, 0)
    group(n_plain * NSA_GROUP_CHUNKS, True)

    for h in range(NSA_KV_HEADS):
        qrt = qrts[h]
        o_cmp = ocmp_ref[h]
        acc = acc_ref[h]
        o_sel = acc[:HEAD_DIM] / jnp.maximum(acc[HEAD_DIM:HEAD_DIM + 1], 1e-30)

        nw = WINDOW + tq
        w0 = pl.multiple_of(jnp.maximum(q0 - WINDOW, 0), LANES)
        st = jnp.dot(kw_ref[pl.ds(w0, nw), :], qrt, preferred_element_type=F32)
        kpos = w0 + lax.broadcasted_iota(jnp.int32, (nw, tq), 0)
        valid = (kpos <= qlane) & (kpos >= qlane - WINDOW)
        p = _softmax_cols(st, _tile_lanes(valid, g4))
        o_win = jnp.dot(vwt_ref[HEAD_DIM * h:HEAD_DIM * (h + 1), pl.ds(w0, nw)], p.astype(BF16),
                        preferred_element_type=F32)

        for g in range(g4):
            r = 3 * (g4 * h + g)
            sl = slice(g * tq, (g + 1) * tq)
            o = (gat_ref[r:r + 1, :] * o_cmp[:, sl] + gat_ref[r + 1:r + 2, :] * o_sel[:, sl]
                 + gat_ref[r + 2:r + 3, :] * o_win[:, sl])
            o_ref[HEAD_DIM * (g4 * h + g):HEAD_DIM * (g4 * h + g + 1), :] = o.astype(o_ref.dtype)


def _nsa_prompt(qat, qart, gat, kc, vct, ks, vst, kw, vwt, tq, tk):
    n = ks.shape[0]
    n_sub = kc.shape[0]
    n_cmp = n_sub - 1
    n_sel = n // SEL_BLOCK
    assert n % tq == 0 and n % tk == 0 and tk % SEL_BLOCK == 0 and n >= WINDOW + tq and tq % LANES == 0
    assert tk % tq == 0 and tk // SEL_BLOCK <= LANES
    imp = _importance_matrix(n_sel, n_sub)
    hot = (jnp.arange(tk)[:, None] // SEL_BLOCK == jnp.arange(LANES)[None, :]).astype(BF16)
    col = lambda r: pl.BlockSpec((r, tq), lambda i: (0, i))
    return pl.pallas_call(
        functools.partial(_nsa_prompt_kernel, tq=tq, tk=tk, sub=min(tk, 2 * LANES), n_cmp=n_cmp),
        grid=(n // tq,),
        in_specs=[col(WA), col(WA), col(32), _full(kc.shape), _full(vct.shape), _full(imp.shape), _full(hot.shape),
                  _full(ks.shape), _full(vst.shape), _full(kw.shape), _full(vwt.shape)],
        out_specs=col(WA),
        out_shape=jax.ShapeDtypeStruct((WA, n), BF16),
        scratch_shapes=[pltpu.VMEM((NSA_KV_HEADS, n_sel, tq), F32),
                        pltpu.VMEM((NSA_KV_HEADS, HEAD_DIM, NSA_GROUP * tq), F32),
                        pltpu.VMEM((NSA_KV_HEADS, NSA_GROUP * tq), F32),
                        pltpu.VMEM((NSA_KV_HEADS, HEAD_DIM + SUM_ROWS, NSA_GROUP * tq), F32),
                        ] + [pltpu.VMEM((tk, NSA_GROUP * tq), F32)] * (NSA_AHEAD + 1),
        compiler_params=_cparams(("parallel",)),
        name="nsa_prompt",
    )(qat, qart, gat, kc, vct, imp, hot, ks, vst, kw, vwt)


ROUTE_COLS = LANES


def _first_lane_of_max(v, lane):
    mx = jnp.max(v, axis=-1, keepdims=True)
    return mx, jnp.min(jnp.where(v == mx, lane, ROUTE_COLS), axis=-1, keepdims=True)


def _merge_kernel(oa_ref, ob_ref, gm_ref, x_ref, wa_ref, wb_ref, wo_ref, g_ref, rwh_ref, rwl_ref, rb_ref,
                  x1_ref, h2_ref, re_ref, rw_ref, *, transposed):
    d = x_ref.shape[1]
    dims = (((0,), (0,)), ((), ())) if transposed else (((1,), (0,)), ((), ()))
    ya = lax.dot_general(oa_ref[...], wa_ref[...], dims, preferred_element_type=F32)
    yb = lax.dot_general(ob_ref[...], wb_ref[...], dims, preferred_element_type=F32)
    mix = gm_ref[:, :d] * ya + gm_ref[:, d:] * yb
    x1 = x_ref[...] + jnp.dot(mix.astype(BF16), wo_ref[...], preferred_element_type=F32)
    x1_ref[...] = x1
    h2 = x1 * lax.rsqrt(jnp.mean(x1 * x1, axis=-1, keepdims=True) + EPS) * g_ref[...]
    h2_ref[...] = h2

    hi = h2.astype(BF16)
    lo = (h2 - hi.astype(F32)).astype(BF16)
    logits = (jnp.dot(hi, rwh_ref[...], preferred_element_type=F32)
              + jnp.dot(lo, rwh_ref[...], preferred_element_type=F32)
              + jnp.dot(hi, rwl_ref[...], preferred_element_type=F32)) + rb_ref[...]
    lane = lax.broadcasted_iota(jnp.int32, logits.shape, 1)
    is_g = lane < N_GROUPS
    gl = jnp.where(is_g, logits, NEG)
    gmx, grp = _first_lane_of_max(gl, lane)
    p_grp = 1.0 / jnp.sum(jnp.where(is_g, jnp.exp(gl - gmx), 0.0), axis=-1, keepdims=True)
    e_id = lane - N_GROUPS
    in_grp = (e_id >= 0) & (e_id < N_EXPERTS) & (e_id // EXPERTS_PER_GROUP == grp)
    el = jnp.where(in_grp, logits, NEG)
    emx = jnp.max(el, axis=-1, keepdims=True)
    ee = jnp.where(in_grp, jnp.exp(el - emx), -1.0)
    e1, i1 = _first_lane_of_max(ee, lane)
    e2, i2 = _first_lane_of_max(jnp.where(lane == i1, -1.0, ee), lane)
    inv = p_grp / (e1 + e2)
    re_ref[...] = jnp.where(lane == 0, i1 - N_GROUPS, jnp.where(lane == 1, i2 - N_GROUPS, 0))
    rw_ref[...] = jnp.where(lane == 0, e1 * inv, jnp.where(lane == 1, e2 * inv, 0.0))


def _merge(oa, ob, gm, x, wa, wb, wo, ffn_g, rwh, rwl, rb, tm, transposed):
    n, d = x.shape
    assert n % tm == 0
    row = lambda c: pl.BlockSpec((tm, c), lambda i: (i, 0))
    o_spec = pl.BlockSpec((WA, tm), lambda i: (0, i)) if transposed else row(WA)
    return pl.pallas_call(
        functools.partial(_merge_kernel, transposed=transposed),
        grid=(n // tm,),
        in_specs=[o_spec, o_spec, row(2 * d), row(d), _full(wa.shape), _full(wb.shape), _full(wo.shape),
                  _full((1, d)), _full(rwh.shape), _full(rwl.shape), _full((1, ROUTE_COLS))],
        out_specs=[row(d), row(d), row(ROUTE_COLS), row(ROUTE_COLS)],
        out_shape=[jax.ShapeDtypeStruct((n, d), F32), jax.ShapeDtypeStruct((n, d), F32),
                   jax.ShapeDtypeStruct((n, ROUTE_COLS), jnp.int32), jax.ShapeDtypeStruct((n, ROUTE_COLS), F32)],
        compiler_params=_cparams(("parallel",)),
        name="merge_route",
    )(oa, ob, gm, x, wa, wb, wo, ffn_g.reshape(1, d), rwh, rwl, rb)


def _route_plan_kernel(re_ref, dest_ref, be_ref, cnt_ref, start_ref, carry_ref, tri_ref, *, bm, n_blocks):
    phase = pl.program_id(0)
    tile = pl.program_id(1)
    tm = re_ref.shape[0]
    lane = lax.broadcasted_iota(jnp.int32, (tm, ROUTE_COLS), 1)
    hot = [lane == re_ref[:, k:k + 1] for k in range(EXPERT_TOP_K)]
    both = sum(h.astype(F32) for h in hot)

    @pl.when((phase == 0) & (tile == 0))
    def _():
        cnt_ref[...] = jnp.zeros_like(cnt_ref)
        r = lax.broadcasted_iota(jnp.int32, (tm, tm), 0)
        c = lax.broadcasted_iota(jnp.int32, (tm, tm), 1)
        tri_ref[...] = (c < r).astype(BF16)

    @pl.when(phase == 0)
    def _():
        cnt_ref[...] += jnp.sum(both, axis=0, keepdims=True)

    @pl.when((phase == 1) & (tile == 0))
    def _():
        cnt = jnp.broadcast_to(cnt_ref[...], (8, ROUTE_COLS))
        padded = jnp.ceil(cnt / bm) * bm
        l8 = lax.broadcasted_iota(jnp.int32, (8, ROUTE_COLS), 1)
        end = padded
        shift = 1
        while shift < N_EXPERTS:
            end = end + jnp.where(l8 >= shift, pltpu.roll(end, shift, axis=1), 0.0)
            shift *= 2
        start_ref[...] = (end - padded)[:1]
        carry_ref[...] = jnp.zeros_like(carry_ref)
        nbp = be_ref.shape[0]
        first = (lax.broadcasted_iota(jnp.int32, (nbp, ROUTE_COLS), 0) * bm).astype(F32)
        lb = lax.broadcasted_iota(jnp.int32, (nbp, ROUTE_COLS), 1)
        hits = jnp.where((lb < N_EXPERTS) & (jnp.broadcast_to(end[:1], (nbp, ROUTE_COLS)) <= first), 1.0, 0.0)
        be = jnp.minimum(jnp.sum(hits, axis=1, keepdims=True), N_EXPERTS - 1.0)
        be_ref[...] = jnp.broadcast_to(be, (nbp, ROUTE_COLS)).astype(jnp.int32)

    @pl.when(phase == 1)
    def _():
        before = jnp.dot(tri_ref[...], both.astype(BF16), preferred_element_type=F32) + carry_ref[...]
        slot = before + start_ref[...]
        dest = [jnp.sum(jnp.where(h, slot, 0.0), axis=1, keepdims=True) for h in hot]
        dest_ref[...] = jnp.where(lane == 0, dest[0], jnp.where(lane == 1, dest[1], 0.0)).astype(jnp.int32)
        carry_ref[...] += jnp.sum(both, axis=0, keepdims=True)


def _route_plan(experts, bm, tm):
    n = experts.shape[0]
    assert n % tm == 0
    n_blocks = -(-n * EXPERT_TOP_K // bm) + N_EXPERTS
    nbp = -(-n_blocks // 8) * 8
    dest, be = pl.pallas_call(
        functools.partial(_route_plan_kernel, bm=bm, n_blocks=n_blocks),
        grid=(2, n // tm),
        in_specs=[pl.BlockSpec((tm, ROUTE_COLS), lambda p, t: (t, 0))],
        out_specs=[pl.BlockSpec((tm, ROUTE_COLS), lambda p, t: (t * p, 0)),
                   pl.BlockSpec((nbp, ROUTE_COLS), lambda p, t: (0, 0))],
        out_shape=[jax.ShapeDtypeStruct((n, ROUTE_COLS), jnp.int32),
                   jax.ShapeDtypeStruct((nbp, ROUTE_COLS), jnp.int32)],
        scratch_shapes=[pltpu.VMEM((1, ROUTE_COLS), F32), pltpu.VMEM((1, ROUTE_COLS), F32),
                        pltpu.VMEM((1, ROUTE_COLS), F32), pltpu.VMEM((tm, tm), BF16)],
        compiler_params=_cparams(("arbitrary", "arbitrary")),
        name="route_plan",
    )(experts)
    return dest[:, :EXPERT_TOP_K].reshape(-1), be[:n_blocks, 0], n_blocks


def _dispatch_kernel(dest_ref, h_ref, xs_in, xs_hbm, sem, *, tok0):
    del xs_in
    tm = h_ref.shape[0]
    base = (tok0 + pl.program_id(0) * tm) * EXPERT_TOP_K

    def copy(r, k):
        return pltpu.make_async_copy(h_ref.at[pl.ds(r, 1), :],
                                     xs_hbm.at[pl.ds(dest_ref[base + EXPERT_TOP_K * r + k], 1), :], sem)

    def start(r, c):
        for k in range(EXPERT_TOP_K):
            copy(r, k).start()
        return c

    def wait(r, c):
        for k in range(EXPERT_TOP_K):
            copy(r, k).wait()
        return c

    lax.fori_loop(0, tm, start, 0, unroll=4)
    lax.fori_loop(0, tm, wait, 0, unroll=4)


def _dispatch_rows(dest, h, xs, tok0, tm):
    n, d = h.shape
    assert n % tm == 0
    grid_spec = pltpu.PrefetchScalarGridSpec(
        num_scalar_prefetch=1,
        grid=(n // tm,),
        in_specs=[pl.BlockSpec((tm, d), lambda i, dst: (i, 0)), pl.BlockSpec(memory_space=pl.ANY)],
        out_specs=pl.BlockSpec(memory_space=pl.ANY),
        scratch_shapes=[pltpu.SemaphoreType.DMA(())],
    )
    return pl.pallas_call(
        functools.partial(_dispatch_kernel, tok0=tok0),
        grid_spec=grid_spec,
        out_shape=jax.ShapeDtypeStruct(xs.shape, xs.dtype),
        input_output_aliases={2: 0},
        compiler_params=_cparams(("arbitrary",)),
        name="moe_dispatch",
    )(dest, h, xs)


def _expert_kernel(be_ref, xs_ref, wg_ref, wu_ref, wd_ref, ys_ref):
    del be_ref
    xb = xs_ref[...].astype(BF16)
    gate = jnp.dot(xb, wg_ref[0].astype(BF16), preferred_element_type=F32)
    up = jnp.dot(xb, wu_ref[0].astype(BF16), preferred_element_type=F32)
    act = (jax.nn.silu(gate) * up).astype(BF16)
    ys_ref[...] = jnp.dot(act, wd_ref[0].astype(BF16), preferred_element_type=F32)


def _expert_ffn(block_e, xs, w_gate, w_up, w_down, bm):
    n_slots, d = xs.shape
    ff = w_gate.shape[2]
    grid_spec = pltpu.PrefetchScalarGridSpec(
        num_scalar_prefetch=1,
        grid=(n_slots // bm,),
        in_specs=[pl.BlockSpec((bm, d), lambda b, be: (b, 0)),
                  pl.BlockSpec((1, d, ff), lambda b, be: (be[b], 0, 0)),
                  pl.BlockSpec((1, d, ff), lambda b, be: (be[b], 0, 0)),
                  pl.BlockSpec((1, ff, d), lambda b, be: (be[b], 0, 0))],
        out_specs=pl.BlockSpec((bm, d), lambda b, be: (b, 0)),
    )
    return pl.pallas_call(
        _expert_kernel,
        grid_spec=grid_spec,
        out_shape=jax.ShapeDtypeStruct((n_slots, d), F32),
        compiler_params=_cparams(("arbitrary",)),
        name="expert_ffn",
    )(block_e, xs, w_gate, w_up, w_down)


def _nt(a, b):
    return lax.dot_general(a, b, (((1,), (1,)), ((), ())), preferred_element_type=F32)


def _flash_rows(s, v, m_prev, l_prev, acc_prev, v_transposed=False):
    m_new = jnp.maximum(m_prev, jnp.max(s, axis=-1, keepdims=True))
    alpha = jnp.exp2(m_prev - m_new)
    p = jnp.exp2(s - m_new)
    l_new = alpha * l_prev + jnp.sum(p, axis=-1, keepdims=True)
    pb = p.astype(BF16)
    pv = _nt(pb, v) if v_transposed else jnp.dot(pb, v, preferred_element_type=F32)
    return m_new, l_new, alpha * acc_prev + pv


def _page_copies(pt_ref, first_page, n, src_hbm, dst, sem, slab):
    out = []
    for k in range(n):
        p0 = pl.multiple_of(pt_ref[first_page + k] * slab, slab)
        out.append(pltpu.make_async_copy(src_hbm.at[pl.ds(p0, slab), :], dst.at[pl.ds(k * slab, slab), :], sem))
    return out


def _diff_sample_kernel(pt_ref, q_ref, kc_hbm, vc_hbm, kn_ref, vn_ref, lam_ref, g_ref, o_ref,
                        kbuf, vbuf, sem, m_ref, l_ref, acc_ref, *, page, ppc, n_chunks, lam_init):
    b = pl.program_id(0)
    c = pl.program_id(1)
    step = b * n_chunks + c
    total = pl.num_programs(0) * n_chunks
    t = q_ref.shape[1]
    kslab = WB * page // LANES
    vslab = page * DIFF_HEADS

    def copies(s, slot):
        first = s * ppc
        return (_page_copies(pt_ref, first, ppc, kc_hbm, kbuf.at[slot], sem.at[0, slot], kslab)
                + _page_copies(pt_ref, first, ppc, vc_hbm, vbuf.at[slot], sem.at[1, slot], vslab))

    slot = step % 2

    @pl.when(step == 0)
    def _():
        for cp in copies(0, 0):
            cp.start()

    @pl.when(step + 1 < total)
    def _():
        for cp in copies(step + 1, 1 - slot):
            cp.start()

    @pl.when(c == 0)
    def _():
        m_ref[...] = jnp.full_like(m_ref, NEG)
        l_ref[...] = jnp.zeros_like(l_ref)
        acc_ref[...] = jnp.zeros_like(acc_ref)

    for cp in copies(step, slot):
        cp.wait()

    q = q_ref[0].astype(F32)
    lane = lax.broadcasted_iota(jnp.int32, (t, 2 * HEAD_DIM), 1)

    def q_pair(h):
        qh = q[:, 2 * HEAD_DIM * h:2 * HEAD_DIM * (h + 1)]
        return jnp.concatenate([jnp.where(lane < HEAD_DIM, qh, 0.0), jnp.where(lane >= HEAD_DIM, qh, 0.0)],
                               axis=0).astype(BF16)

    two_hd = 2 * HEAD_DIM
    for h in range(DIFF_HEADS):
        kt = jnp.concatenate([kbuf[slot, kslab * k + two_hd * h:kslab * k + two_hd * (h + 1), :]
                              for k in range(ppc)], axis=1).astype(BF16)
        v2 = vbuf[slot, pl.ds(h, ppc * page, stride=DIFF_HEADS), :].astype(BF16)
        s = jnp.dot(q_pair(h), kt, preferred_element_type=F32)
        m, l, acc = _flash_rows(s, v2, m_ref[h], l_ref[h], acc_ref[h])
        m_ref[h] = m
        l_ref[h] = l
        acc_ref[h] = acc

    @pl.when(c == n_chunks - 1)
    def _():
        lam = _diff_lambda(lam_ref, lam_init)
        trow = lax.broadcasted_iota(jnp.int32, (2 * t, t), 0) % t
        tcol = lax.broadcasted_iota(jnp.int32, (2 * t, t), 1)
        for h in range(DIFF_HEADS):
            kn = kn_ref[0][:, 2 * HEAD_DIM * h:2 * HEAD_DIM * (h + 1)].astype(BF16)
            vn = vn_ref[0][:, DIFF_VDIM * h:DIFF_VDIM * (h + 1)].astype(BF16)
            s = jnp.where(tcol <= trow, _nt(q_pair(h), kn), NEG)
            m, l, acc = _flash_rows(s, vn, m_ref[h], l_ref[h], acc_ref[h])
            o = acc / l
            a = o[:t] - lam * o[t:]
            y = a * lax.rsqrt(jnp.mean(a * a, axis=-1, keepdims=True) + EPS) * g_ref[...] * (1.0 - lam_init)
            o_ref[0, :, DIFF_VDIM * h:DIFF_VDIM * (h + 1)] = y.astype(o_ref.dtype)


def _diff_sample(page_table, q3, kcache, vcache, kn3, vn3, diff_lambda, subln_g, lam_init, page, ppc):
    db, t, _ = q3.shape
    n_pages = page_table.shape[1]
    assert n_pages % ppc == 0 and t == 8
    n_chunks = n_pages // ppc
    tks = ppc * page
    per_b = lambda w: pl.BlockSpec((1, t, w), lambda b, c, pt: (b, 0, 0))
    const = lambda shape: pl.BlockSpec(shape, lambda b, c, pt: (0,) * len(shape))
    grid_spec = pltpu.PrefetchScalarGridSpec(
        num_scalar_prefetch=1,
        grid=(db, n_chunks),
        in_specs=[per_b(WB), pl.BlockSpec(memory_space=pl.ANY), pl.BlockSpec(memory_space=pl.ANY),
                  per_b(WB), per_b(WV), const(diff_lambda.shape), const((1, DIFF_VDIM))],
        out_specs=per_b(WV),
        scratch_shapes=[pltpu.VMEM((2, tks * WB // LANES, LANES), F32),
                        pltpu.VMEM((2, tks * DIFF_HEADS, DIFF_VDIM), F32),
                        pltpu.SemaphoreType.DMA((2, 2)),
                        pltpu.VMEM((DIFF_HEADS, 2 * t, 1), F32), pltpu.VMEM((DIFF_HEADS, 2 * t, 1), F32),
                        pltpu.VMEM((DIFF_HEADS, 2 * t, DIFF_VDIM), F32)],
    )
    return pl.pallas_call(
        functools.partial(_diff_sample_kernel, page=page, ppc=ppc, n_chunks=n_chunks, lam_init=lam_init),
        grid_spec=grid_spec,
        out_shape=jax.ShapeDtypeStruct((db, t, WV), BF16),
        compiler_params=_cparams(("arbitrary", "arbitrary")),
        name="diff_sample",
    )(page_table.reshape(-1), q3, kcache, vcache, kn3, vn3, diff_lambda, subln_g.reshape(1, DIFF_VDIM))


def _softmax_rows(s, valid):
    sm = jnp.where(valid, s, NEG)
    m = jnp.max(sm, axis=-1, keepdims=True)
    e = jnp.where(valid, jnp.exp2(sm - m), 0.0)
    return e / jnp.maximum(jnp.sum(e, axis=-1, keepdims=True), 1e-30)


def _select_blocks_rows(score, blk):
    big = jnp.int32(2 ** 30)

    def body(_, carry):
        sc, bias = carry
        mx = jnp.max(sc, axis=-1, keepdims=True)
        first = jnp.min(jnp.where(sc == mx, blk, big), axis=-1, keepdims=True)
        hit = blk == first
        return jnp.where(hit, -3e38, sc), jnp.where(hit, 0.0, bias)

    return lax.fori_loop(0, TOP_N, body, (score, jnp.full(score.shape, NEG, F32)))[1]


def _pad_rows(x, rows):
    return jnp.concatenate([x, jnp.zeros((rows - x.shape[0], x.shape[1]), x.dtype)], axis=0)


def _nsa_sample_kernel(pt_ref, q_ref, qr_ref, ga_ref, cache_hbm, new_ref, wst_ref, wnew_ref,
                       w1k_ref, w1v_ref, ck_ref, cv_ref, w2k_ref, w2v_ref, impt_ref, exp_ref, o_ref,
                       ring, kctok, vctok, sem, *, page, n_pages, ppc, past, n_sel):
    b = pl.program_id(0)
    nb = pl.num_programs(0)
    t = q_ref.shape[1]
    g4 = NSA_GROUP
    n_sub = past // CMP_STRIDE
    n_cmp = n_sub - 1
    n_selp = impt_ref.shape[1]
    slab = 4 * KV * page // LANES
    half = slab // 2
    n_chunks = n_pages // ppc
    n_jobs = 2 * n_chunks
    tks = ppc * page

    def copies(bb, jj, slot):
        first = bb * n_pages + (jj % n_chunks) * ppc
        out = []
        for k in range(ppc):
            p0 = pl.multiple_of(pt_ref[first + k] * slab + half * (jj // n_chunks), half)
            out.append(pltpu.make_async_copy(cache_hbm.at[pl.ds(p0, half), :],
                                             ring.at[slot, pl.ds(k * half, half), :], sem.at[slot]))
        return out

    def begin(jj):
        slot = jj % 2
        if jj + 1 < n_jobs:
            for cp in copies(b, jj + 1, 1 - slot):
                cp.start()
        else:
            @pl.when(b + 1 < nb)
            def _():
                for cp in copies(b + 1, 0, 1 - slot):
                    cp.start()
        for cp in copies(b, jj, slot):
            cp.wait()
        return slot

    @pl.when(b == 0)
    def _():
        for cp in copies(0, 0, 0):
            cp.start()

    for jj in range(n_chunks):
        slot = begin(jj)
        for k in range(ppc):
            r0 = (jj * ppc + k) * page
            kctok[r0:r0 + page, :] = ring[slot, k * half:k * half + KV, :].T
            vctok[r0:r0 + page, :] = ring[slot, k * half + KV:(k + 1) * half, :].T
    abk = _compress_ab(kctok, w1k_ref, n_sub)
    abv = _compress_ab(vctok, w1v_ref, n_sub)
    kcmp = jnp.dot(_compress_hidden(abk, ck_ref[...]), w2k_ref[...], preferred_element_type=F32).astype(BF16)
    vcmp = jnp.dot(_compress_hidden(abv, cv_ref[...]), w2v_ref[...], preferred_element_type=F32).astype(BF16)

    q = q_ref[0].astype(F32)
    qr = qr_ref[0].astype(F32)
    ga = ga_ref[0]
    lane = lax.broadcasted_iota(jnp.int32, (t, LANES), 1)
    trow = lax.broadcasted_iota(jnp.int32, (g4 * t, 1), 0) % t
    qpos = past + trow
    qpos_t = past + lax.broadcasted_iota(jnp.int32, (t, 1), 0)

    def to_half(x, have, want):
        return x if have == want else pltpu.roll(x, HEAD_DIM, axis=1)

    def q_rows(qq, h):
        keep = (lane >= HEAD_DIM * h) & (lane < HEAD_DIM * (h + 1))
        rows = []
        for g in range(g4):
            hd = g4 * h + g
            tile = to_half(qq[:, LANES * (hd // 2):LANES * (hd // 2 + 1)], hd % 2, h)
            rows.append(jnp.where(keep, tile, 0.0))
        return jnp.concatenate(rows, axis=0).astype(BF16)

    new = new_ref[0]
    ksn = _pad_rows(new[:, 2 * KV:3 * KV], LANES).astype(BF16)
    vsn = _pad_rows(new[:, 3 * KV:4 * KV], LANES).astype(BF16)
    kwst = wst_ref[0][:KV, :].astype(BF16)
    vwst = wst_ref[0][KV:, :].astype(BF16)
    kwn = _pad_rows(wnew_ref[0][:, :KV], LANES).astype(BF16)
    vwn = _pad_rows(wnew_ref[0][:, KV:], LANES).astype(BF16)
    wbuf = wst_ref.shape[2]
    ncol = lax.broadcasted_iota(jnp.int32, (g4 * t, LANES), 1)
    new_ok = (ncol < t) & (ncol <= trow)
    blocks_per_chunk = tks // SEL_BLOCK
    out_tiles = [[None, None] for _ in range(WA // LANES)]
    o_cmps, selbiases, qrhs = [], [], []

    for h in range(NSA_KV_HEADS):
        s = _nt(q_rows(q, h), kcmp)
        nidx = lax.broadcasted_iota(jnp.int32, s.shape, 1)
        p = _softmax_rows(s, (nidx * CMP_STRIDE + (CMP_BLOCK - 1) <= qpos) & (nidx < n_cmp))
        o_cmp = jnp.dot(p.astype(BF16), vcmp, preferred_element_type=F32)
        psum = p[:t]
        for g in range(1, g4):
            psum = psum + p[g * t:(g + 1) * t]
        imp = jnp.zeros((t, n_selp), F32)
        for part in _split3(psum):
            imp += jnp.dot(part, impt_ref[...], preferred_element_type=F32)
        blk = lax.broadcasted_iota(jnp.int32, (t, n_selp), 1)
        cur = qpos_t // SEL_BLOCK
        forced = (blk == 0) | (blk == cur) | (blk == cur - 1)
        score = jnp.where(blk > cur, -1.0, jnp.where(forced, FORCED_SCORE, imp))
        score = jnp.where(blk < n_sel, score, -3e38)
        selbiases.append(_select_blocks_rows(score, blk))
        o_cmps.append(o_cmp)
        qrhs.append(q_rows(qr, h))

    stats = [(jnp.full((g4 * t, 1), NEG, F32), jnp.zeros((g4 * t, 1), F32), jnp.zeros((g4 * t, LANES), F32))
             for _ in range(NSA_KV_HEADS)]
    for c in range(n_chunks):
        slot = begin(n_chunks + c)
        kst = jnp.concatenate([ring[slot, k * half:k * half + KV, :] for k in range(ppc)], axis=1).astype(BF16)
        vst = jnp.concatenate([ring[slot, k * half + KV:(k + 1) * half, :] for k in range(ppc)],
                              axis=1).astype(BF16)
        b0 = c * blocks_per_chunk
        for h in range(NSA_KV_HEADS):
            tile = selbiases[h][:, LANES * (b0 // LANES):LANES * (b0 // LANES + 1)].astype(BF16)
            bias = jnp.dot(tile, exp_ref[(b0 % LANES) // blocks_per_chunk], preferred_element_type=F32)
            s = jnp.dot(qrhs[h], kst, preferred_element_type=F32) + jnp.concatenate([bias] * g4, axis=0)
            stats[h] = _flash_rows(s, vst, *stats[h], v_transposed=True)

    for h in range(NSA_KV_HEADS):
        qrh, selbias, o_cmp = qrhs[h], selbiases[h], o_cmps[h]
        nb_blk = past // SEL_BLOCK
        bias_new = jnp.concatenate([selbias[:, nb_blk:nb_blk + 1]] * g4, axis=0)
        s = jnp.where(new_ok, _nt(qrh, ksn) + bias_new, NEG)
        m, l, av = _flash_rows(s, vsn, *stats[h])
        o_sel = av / jnp.maximum(l, 1e-30)

        s = jnp.concatenate([jnp.dot(qrh, kwst, preferred_element_type=F32), _nt(qrh, kwn)], axis=1)
        widx = lax.broadcasted_iota(jnp.int32, s.shape, 1)
        kpos = past - wbuf + widx
        valid = (kpos <= qpos) & (kpos >= qpos - WINDOW) & (widx < wbuf + t)
        p = _softmax_rows(s, valid).astype(BF16)
        o_win = _nt(p[:, :wbuf], vwst) + jnp.dot(p[:, wbuf:], vwn, preferred_element_type=F32)

        for g in range(g4):
            hd = g4 * h + g
            r = 3 * hd
            rs = slice(g * t, (g + 1) * t)
            o = ga[:, r:r + 1] * o_cmp[rs] + ga[:, r + 1:r + 2] * o_sel[rs] + ga[:, r + 2:r + 3] * o_win[rs]
            out_tiles[hd // 2][hd % 2] = to_half(o, h, hd % 2)

    for k, (lo, hi) in enumerate(out_tiles):
        o_ref[0, :, LANES * k:LANES * (k + 1)] = jnp.where(lane < HEAD_DIM, lo, hi).astype(o_ref.dtype)


def _nsa_sample(page_table, q3, qr3, ga3, cache, new3, win_state, wnew3, cw, page, ppc):
    db, t, _ = q3.shape
    n_pages = page_table.shape[1]
    past = n_pages * page
    wbuf = win_state.shape[2]
    w1k, w1v, ck, cv, w2k, w2v, _ = cw
    n_sub = past // CMP_STRIDE
    n_sel = -(-(past + t) // SEL_BLOCK)
    n_selp = -(-n_sel // LANES) * LANES
    tks = ppc * page
    blocks_per_chunk = tks // SEL_BLOCK
    assert t == 8 and (past + t) // CMP_STRIDE == n_sub and n_pages % ppc == 0 and past % SEL_BLOCK == 0
    assert LANES % blocks_per_chunk == 0 and wbuf == WINDOW and t <= SEL_BLOCK and page == LANES
    impt = jnp.pad(_importance_matrix(n_sel, n_sub).T, ((0, 0), (0, n_selp - n_sel)))
    m = jnp.arange(LANES // blocks_per_chunk)[:, None, None]
    j = jnp.arange(LANES)[None, :, None]
    u = jnp.arange(tks)[None, None, :]
    expand = (j == blocks_per_chunk * m + u // SEL_BLOCK).astype(BF16)
    per_b = lambda r, w: pl.BlockSpec((1, r, w), lambda b, pt: (b, 0, 0))
    const = lambda a: pl.BlockSpec(a.shape, lambda b, pt: (0,) * a.ndim)
    grid_spec = pltpu.PrefetchScalarGridSpec(
        num_scalar_prefetch=1,
        grid=(db,),
        in_specs=[per_b(t, WA), per_b(t, WA), per_b(t, LANES), pl.BlockSpec(memory_space=pl.ANY),
                  per_b(t, 4 * KV), per_b(2 * KV, wbuf), per_b(t, 2 * KV),
                  const(w1k), const(w1v), const(ck), const(cv), const(w2k), const(w2v), const(impt),
                  const(expand)],
        out_specs=per_b(t, WA),
        scratch_shapes=[pltpu.VMEM((2, ppc * 2 * KV, LANES), F32), pltpu.VMEM((past, KV), F32),
                        pltpu.VMEM((past, KV), F32), pltpu.SemaphoreType.DMA((2,))],
    )
    return pl.pallas_call(
        functools.partial(_nsa_sample_kernel, page=page, n_pages=n_pages, ppc=ppc, past=past, n_sel=n_sel),
        grid_spec=grid_spec,
        out_shape=jax.ShapeDtypeStruct((db, t, WA), BF16),
        compiler_params=_cparams(("arbitrary",)),
        name="nsa_sample",
    )(page_table.reshape(-1), q3, qr3, ga3, cache, new3, win_state, wnew3, w1k, w1v, ck, cv, w2k, w2v, impt,
      expand)


def _combine_kernel(dest_ref, x_ref, w_ref, g_ref, ys_hbm, o_ref, buf, sem, *, tok0):
    i = pl.program_id(0)
    n = pl.num_programs(0)
    tm = x_ref.shape[0]

    def copy(tile, slot, r, k):
        src = dest_ref[(tok0 + tile * tm + r) * EXPERT_TOP_K + k]
        return pltpu.make_async_copy(ys_hbm.at[pl.ds(src, 1), :], buf.at[slot, k, pl.ds(r, 1), :], sem.at[slot])

    def start_tile(tile, slot):
        def body(r, c):
            for k in range(EXPERT_TOP_K):
                copy(tile, slot, r, k).start()
            return c
        lax.fori_loop(0, tm, body, 0, unroll=4)

    slot = i % 2

    @pl.when(i == 0)
    def _():
        start_tile(0, 0)

    @pl.when(i + 1 < n)
    def _():
        start_tile(i + 1, 1 - slot)

    def wait(r, c):
        for k in range(EXPERT_TOP_K):
            copy(i, slot, r, k).wait()
        return c

    lax.fori_loop(0, tm, wait, 0, unroll=4)
    x = x_ref[...]
    for k in range(EXPERT_TOP_K):
        x = x + w_ref[:, k:k + 1] * buf[slot, k]
    o_ref[...] = x * lax.rsqrt(jnp.mean(x * x, axis=-1, keepdims=True) + EPS) * g_ref[...]


def _combine_norm(dest, x1, weights, ys, g, tok0, tm):
    n, d = x1.shape
    assert n % tm == 0
    grid_spec = pltpu.PrefetchScalarGridSpec(
        num_scalar_prefetch=1,
        grid=(n // tm,),
        in_specs=[pl.BlockSpec((tm, d), lambda i, dst: (i, 0)), pl.BlockSpec((tm, ROUTE_COLS), lambda i, dst: (i, 0)),
                  pl.BlockSpec((1, d), lambda i, dst: (0, 0)), pl.BlockSpec(memory_space=pl.ANY)],
        out_specs=pl.BlockSpec((tm, d), lambda i, dst: (i, 0)),
        scratch_shapes=[pltpu.VMEM((2, EXPERT_TOP_K, tm, d), F32), pltpu.SemaphoreType.DMA((2,))],
    )
    return pl.pallas_call(
        functools.partial(_combine_kernel, tok0=tok0),
        grid_spec=grid_spec,
        out_shape=jax.ShapeDtypeStruct((n, d), F32),
        compiler_params=_cparams(("arbitrary",)),
        name="moe_combine_norm",
    )(dest, x1, weights, g.reshape(1, d), ys)


MOE_ROWS = 256


def kernel(x_prompt, x_sample, cache_nsa_kv, cache_diff_k, cache_diff_v, state_nsa_win_kv, page_table,
           norm_mix_g, w_in, nsa_cmp_pos, nsa_cmp_k_w1, nsa_cmp_k_w2, nsa_cmp_v_w1, nsa_cmp_v_w2,
           diff_lambda, diff_subln_g, w_proj_a, w_proj_b, w_out, norm_ffn_g,
           router_group_w, router_group_b, router_expert_w, router_expert_b,
           expert_w_gate, expert_w_up, expert_w_down, norm_final_g):
    depth = w_in.shape[0]
    bsz, seq, d = x_prompt.shape
    db, t, _ = x_sample.shape
    n_pool, page = cache_nsa_kv.shape[1:3]
    past = page_table.shape[1] * page
    wbuf = state_nsa_win_kv.shape[2]
    assert depth == 1 and bsz == 1
    l = 0
    lam_init = 0.8 - 0.6 * math.exp(-0.3 * l)
    w = _split_w_in(w_in[l], d)
    cw = _compress_weights(nsa_cmp_pos[l], nsa_cmp_k_w1[l], nsa_cmp_k_w2[l], nsa_cmp_v_w1[l], nsa_cmp_v_w2[l])
    wa, wb, wo = w_proj_a[l].astype(BF16), w_proj_b[l].astype(BF16), w_out[l].astype(BF16)
    rw = jnp.pad(jnp.concatenate([router_group_w[l], router_expert_w[l]], axis=1),
                 ((0, 0), (0, ROUTE_COLS - N_GROUPS - N_EXPERTS)))
    rb = jnp.pad(jnp.concatenate([router_group_b[l], router_expert_b[l]]),
                 (0, ROUTE_COLS - N_GROUPS - N_EXPERTS)).reshape(1, ROUTE_COLS)
    rwh = rw.astype(BF16)
    rwl = (rw - rwh.astype(F32)).astype(BF16)

    xp = x_prompt.reshape(seq, d)
    (nsa_p, win_p, dk_p, dv_p, gm_p, ks_b, kw_b, kb_b,
     qat, qart, qbt, vst, vwt, vbt, gat) = _project(xp, jnp.arange(seq), norm_mix_g[l], w, 512, True)
    kcmp, _, vcmpt = _compress(nsa_p, cw, min(256, seq // CMP_STRIDE))
    oat = _nsa_prompt(qat, qart, gat, kcmp, vcmpt, ks_b, vst, kw_b, vwt, 128, 512)
    obt = _diff_prompt(qbt, kb_b, vbt, diff_lambda[l], diff_subln_g[l], lam_init, 512, 512)
    x1p, h2p, re_p, rw_p = _merge(oat, obt, gm_p, xp, wa, wb, wo, norm_ffn_g[l], rwh, rwl, rb, 512, True)

    ns = db * t
    xs = x_sample.reshape(ns, d)
    pos_s = past + jnp.arange(ns) % t
    (nsa_s, win_s, dk_s, dv_s, gm_s, _, _, _, qa_s, qar_s, qb_s, ga_s) = _project(
        xs, pos_s, norm_mix_g[l], w, ns, False)
    r3 = lambda a: a.reshape(db, t, a.shape[-1])
    slabs = lambda c: c.transpose(0, 2, 3, 4, 1).reshape(-1, page)
    state_t = state_nsa_win_kv[l].transpose(0, 2, 3, 4, 1).reshape(db, 2 * KV, wbuf)
    oa_s = _nsa_sample(page_table, r3(qa_s), r3(qar_s), r3(ga_s), slabs(cache_nsa_kv[l]),
                       r3(nsa_s), state_t, r3(win_s), cw, page, 16)
    ob_s = _diff_sample(page_table, r3(qb_s), slabs(cache_diff_k[l]),
                        cache_diff_v[l].reshape(-1, DIFF_VDIM), r3(dk_s), r3(dv_s),
                        diff_lambda[l], diff_subln_g[l], lam_init, page, 16)
    x1s, h2s, re_s, rw_s = _merge(oa_s.reshape(ns, -1), ob_s.reshape(ns, -1), gm_s, xs, wa, wb, wo,
                                  norm_ffn_g[l], rwh, rwl, rb, ns, False)

    n_all = seq + ns
    plan_tile = max(tm for tm in range(8, 1025, 8) if n_all % tm == 0)
    dest, block_e, n_blocks = _route_plan(jnp.concatenate([re_p, re_s], axis=0), MOE_ROWS, plan_tile)
    xs = jnp.zeros((n_blocks * MOE_ROWS, d), F32)
    xs = _dispatch_rows(dest, h2p, xs, 0, 256)
    xs = _dispatch_rows(dest, h2s, xs, seq, ns)
    ys = _expert_ffn(block_e, xs, expert_w_gate[l], expert_w_up[l], expert_w_down[l], MOE_ROWS)
    y_prompt = _combine_norm(dest, x1p, rw_p, ys, norm_final_g, 0, 256)
    y_sample = _combine_norm(dest, x1s, rw_s, ys, norm_final_g, seq, ns)

    wn = min(WINDOW, seq)
    win_all_t = jnp.concatenate([state_t, r3(win_s).transpose(0, 2, 1)], axis=2)[:, :, -min(WINDOW, past + t):]
    win_all = win_all_t.reshape(db, 2, NSA_KV_HEADS, HEAD_DIM, -1).transpose(0, 4, 1, 2, 3)
    kvs = (4, NSA_KV_HEADS, HEAD_DIM)
    dks = (DIFF_HEADS, 2, HEAD_DIM)
    dvs = (DIFF_HEADS, DIFF_VDIM)
    return (y_prompt.reshape(1, seq, d), y_sample.reshape(db, t, d),
            nsa_p.reshape((1, 1, seq) + kvs), nsa_s.reshape((1, db, t) + kvs),
            dk_p.reshape((1, 1, seq) + dks), dk_s.reshape((1, db, t) + dks),
            dv_p.reshape((1, 1, seq) + dvs), dv_s.reshape((1, db, t) + dvs),
            win_p[seq - wn:].reshape(1, 1, wn, 2, NSA_KV_HEADS, HEAD_DIM),
            win_all[None])
```

```python
import functools
import math

import jax
import jax.numpy as jnp
from jax import lax
from jax.experimental import pallas as pl
from jax.experimental.pallas import tpu as pltpu

F32 = jnp.float32
BF16 = jnp.bfloat16

HEAD_DIM = 64
HALF = HEAD_DIM // 2
NSA_HEADS = 8
NSA_KV_HEADS = 2
NSA_GROUP = NSA_HEADS // NSA_KV_HEADS
CMP_STRIDE = 16
CMP_BLOCK = 2 * CMP_STRIDE
SEL_BLOCK = 64
SEL_PER_CMP = SEL_BLOCK // CMP_STRIDE
TOP_N = 16
WINDOW = 512
FORCED_SCORE = 1e4
DIFF_HEADS = 4
DIFF_VDIM = 2 * HEAD_DIM
N_GROUPS = 4
EXPERTS_PER_GROUP = 8
N_EXPERTS = N_GROUPS * EXPERTS_PER_GROUP
EXPERT_TOP_K = 2
ROPE_THETA = 10000.0
EPS = 1e-6
NEG = -1e30
SCALE = HEAD_DIM ** -0.5 * math.log2(math.e)

LANES = 128
VMEM_LIMIT = 56 * 1024 * 1024

WA = NSA_HEADS * HEAD_DIM
KV = NSA_KV_HEADS * HEAD_DIM
WB = DIFF_HEADS * 2 * HEAD_DIM
WV = DIFF_HEADS * DIFF_VDIM
N_GATE = 3 * NSA_HEADS


def _cparams(sem, flags=None):
    return pltpu.CompilerParams(dimension_semantics=sem, vmem_limit_bytes=VMEM_LIMIT, flags=flags)


def _full(shape):
    return pl.BlockSpec(shape, lambda *_: (0,) * len(shape))


def _swap_halves(t):
    lane = lax.broadcasted_iota(jnp.int32, t.shape, 1)
    fwd = pltpu.roll(t, LANES - HALF, axis=1)
    bwd = pltpu.roll(t, HALF, axis=1)
    return jnp.where(lane % HEAD_DIM < HALF, fwd, bwd)


def _rope_rows(t, cos, sin):
    outs = []
    for a in range(0, t.shape[1], LANES):
        x = t[:, a:a + LANES]
        outs.append(x * cos + _swap_halves(x) * sin)
    return outs[0] if len(outs) == 1 else jnp.concatenate(outs, axis=1)


def _rope_cols(t, cos, sin):
    outs = []
    for a in range(0, t.shape[0], HEAD_DIM):
        x1 = t[a:a + HALF]
        x2 = t[a + HALF:a + HEAD_DIM]
        outs.append(x1 * cos - x2 * sin)
        outs.append(x2 * cos + x1 * sin)
    return jnp.concatenate(outs, axis=0)


def _proj_kernel(x_ref, g_ref, wn_ref, wt_ref, cosn_ref, sinn_ref, cost_ref, sint_ref,
                 nsa_ref, win_ref, dk_ref, dv_ref, gm_ref, ksb_ref, kwb_ref, kbb_ref, *rest, transposed_q):
    x = x_ref[...]
    h = x * lax.rsqrt(jnp.mean(x * x, axis=-1, keepdims=True) + EPS) * g_ref[...]
    hb = h.astype(BF16)
    cosn = cosn_ref[...]
    sinn = sinn_ref[...]

    def mm(a, b):
        return jnp.dot(hb, wn_ref[:, a:b], preferred_element_type=F32)

    c = 0
    y = mm(c, c + 4 * KV)
    ks = _rope_rows(y[:, 2 * KV:3 * KV], cosn, sinn)
    nsa_ref[:, :2 * KV] = y[:, :2 * KV]
    nsa_ref[:, 2 * KV:3 * KV] = ks
    nsa_ref[:, 3 * KV:] = y[:, 3 * KV:]
    ksb_ref[...] = ks.astype(BF16)
    c += 4 * KV
    y = mm(c, c + 2 * KV)
    kw = _rope_rows(y[:, :KV], cosn, sinn)
    win_ref[:, :KV] = kw
    win_ref[:, KV:] = y[:, KV:]
    kwb_ref[...] = kw.astype(BF16)
    c += 2 * KV
    kb = _rope_rows(mm(c, c + WB), cosn, sinn)
    dk_ref[...] = kb
    kbb_ref[...] = kb.astype(BF16)
    c += WB
    dv_ref[...] = mm(c, c + WV)
    c += WV
    d_model = x.shape[1]
    gm_ref[...] = jax.nn.sigmoid(mm(c, c + 2 * d_model))
    c += 2 * d_model

    if transposed_q:
        qat_ref, qart_ref, qbt_ref, vst_ref, vwt_ref, vbt_ref, gat_ref = rest
        cost = cost_ref[...]
        sint = sint_ref[...]

        def mmt(a, b):
            return lax.dot_general(wt_ref[a:b, :], hb, (((1,), (1,)), ((), ())), preferred_element_type=F32)

        r = 0
        qa = mmt(r, r + WA) * SCALE
        qat_ref[...] = qa.astype(BF16)
        qart_ref[...] = _rope_cols(qa, cost, sint).astype(BF16)
        r += WA
        qbt_ref[...] = _rope_cols(mmt(r, r + WB) * SCALE, cost, sint).astype(BF16)
        r += WB
        vst_ref[...] = mmt(r, r + KV).astype(BF16)
        r += KV
        vwt_ref[...] = mmt(r, r + KV).astype(BF16)
        r += KV
        vbt_ref[...] = mmt(r, r + WV).astype(BF16)
        r += WV
        gat_ref[...] = jax.nn.sigmoid(mmt(r, r + 32))
    else:
        qa_ref, qar_ref, qb_ref, ga_ref = rest
        qa = mm(c, c + WA) * SCALE
        qa_ref[...] = qa.astype(BF16)
        qar_ref[...] = _rope_rows(qa, cosn, sinn).astype(BF16)
        c += WA
        qb_ref[...] = _rope_rows(mm(c, c + WB) * SCALE, cosn, sinn).astype(BF16)
        c += WB
        ga_ref[...] = jax.nn.sigmoid(mm(c, c + LANES))


def _split_w_in(w_in, d_model):
    sizes = [WA, KV, KV, KV, KV, KV, KV, N_GATE, WB, WB, WV, 2 * d_model]
    offs = [0]
    for s in sizes:
        offs.append(offs[-1] + s)
    names = ["qa", "kc", "vc", "ks", "vs", "kw", "vw", "ga", "qb", "kb", "vb", "gm"]
    return {n: w_in[:, offs[i]:offs[i + 1]] for i, n in enumerate(names)}


def _rope_tables(pos):
    inv = ROPE_THETA ** (-jnp.arange(HALF, dtype=F32) / HALF)
    ang = pos.astype(F32)[:, None] * inv[None, :]
    cos, sin = jnp.cos(ang), jnp.sin(ang)
    cosn = jnp.tile(cos, (1, LANES // HALF))
    sinn = jnp.tile(jnp.concatenate([-sin, sin], axis=1), (1, LANES // HEAD_DIM))
    return cosn, sinn, cos.T, sin.T


def _project(x, pos, norm_g, w, tm, transposed_q):
    n, d = x.shape
    assert n % tm == 0
    cosn, sinn, cost, sint = _rope_tables(pos)
    wn_parts = [w["kc"], w["vc"], w["ks"], w["vs"], w["kw"], w["vw"], w["kb"], w["vb"], w["gm"]]
    ga_pad = jnp.pad(w["ga"], ((0, 0), (0, LANES - N_GATE)))
    if transposed_q:
        wt = jnp.concatenate([w["qa"], w["qb"], w["vs"], w["vw"], w["vb"], ga_pad[:, :32]], axis=1).T.astype(BF16)
    else:
        wn_parts += [w["qa"], w["qb"], ga_pad]
        wt = jnp.zeros((8, d), BF16)
    wn = jnp.concatenate(wn_parts, axis=1).astype(BF16)

    row = lambda c: pl.BlockSpec((tm, c), lambda i: (i, 0))
    col = lambda r: pl.BlockSpec((r, tm), lambda i: (0, i))
    out_shape = [jax.ShapeDtypeStruct((n, 4 * KV), F32), jax.ShapeDtypeStruct((n, 2 * KV), F32),
                 jax.ShapeDtypeStruct((n, WB), F32), jax.ShapeDtypeStruct((n, WV), F32),
                 jax.ShapeDtypeStruct((n, 2 * d), F32), jax.ShapeDtypeStruct((n, KV), BF16),
                 jax.ShapeDtypeStruct((n, KV), BF16), jax.ShapeDtypeStruct((n, WB), BF16)]
    out_specs = [row(4 * KV), row(2 * KV), row(WB), row(WV), row(2 * d), row(KV), row(KV), row(WB)]
    if transposed_q:
        out_shape += [jax.ShapeDtypeStruct((WA, n), BF16), jax.ShapeDtypeStruct((WA, n), BF16),
                      jax.ShapeDtypeStruct((WB, n), BF16), jax.ShapeDtypeStruct((KV, n), BF16),
                      jax.ShapeDtypeStruct((KV, n), BF16), jax.ShapeDtypeStruct((WV, n), BF16),
                      jax.ShapeDtypeStruct((32, n), F32)]
        out_specs += [col(WA), col(WA), col(WB), col(KV), col(KV), col(WV), col(32)]
    else:
        out_shape += [jax.ShapeDtypeStruct((n, WA), BF16), jax.ShapeDtypeStruct((n, WA), BF16),
                      jax.ShapeDtypeStruct((n, WB), BF16), jax.ShapeDtypeStruct((n, LANES), F32)]
        out_specs += [row(WA), row(WA), row(WB), row(LANES)]
    return pl.pallas_call(
        functools.partial(_proj_kernel, transposed_q=transposed_q),
        grid=(n // tm,),
        in_specs=[row(d), _full((1, d)), _full(wn.shape), _full(wt.shape),
                  row(LANES), row(LANES), col(HALF), col(HALF)],
        out_specs=out_specs,
        out_shape=out_shape,
        compiler_params=_cparams(("parallel",)),
        name="proj",
    )(x, norm_g.reshape(1, d), wn, wt, cosn, sinn, cost, sint)


def _pad_head(qt, slot):
    z = jnp.zeros_like(qt)
    return jnp.concatenate([qt, z] if slot == 0 else [z, qt], axis=0)


SUM_ROWS = 16


def _with_ones(vt):
    return jnp.concatenate([vt, jnp.ones((SUM_ROWS, vt.shape[1]), vt.dtype)], axis=0)


def _flash_step(st, vt1, m_prev, acc_prev):
    m_new = jnp.maximum(m_prev, jnp.max(st, axis=0, keepdims=True))
    p = jnp.exp2(st - m_new).astype(BF16)
    return m_new, jnp.exp2(m_prev - m_new) * acc_prev + jnp.dot(vt1, p, preferred_element_type=F32)


def _diff_lambda(lam_ref, lam_init):
    lv = lam_ref[...]
    a = jnp.sum(lv[0:1] * lv[1:2], axis=-1, keepdims=True)
    b = jnp.sum(lv[2:3] * lv[3:4], axis=-1, keepdims=True)
    return jnp.exp(a) - jnp.exp(b) + lam_init


DIFF_AHEAD = 3


def _diff_prompt_kernel(it_ref, jt_ref, qt_ref, k_ref, vt_ref, lam_ref, g_ref, o_ref, m_ref, acc_ref, *s_refs,
                        tq, tk, sub, lam_init):
    i = it_ref[pl.program_id(0)]
    j = jt_ref[pl.program_id(0)]
    first_diag = i * tq // tk

    @pl.when(j == 0)
    def _():
        m_ref[...] = jnp.full_like(m_ref, NEG)
        acc_ref[...] = jnp.zeros_like(acc_ref)

    n_maps = 2 * DIFF_HEADS
    subs = [slice(r, r + sub) for r in range(0, tk, sub)]

    def step(causal):
        def score(hc, out):
            h, c = divmod(hc, 2)
            qt = _pad_head(qt_ref[HEAD_DIM * hc:HEAD_DIM * (hc + 1), :], c)
            m_new = m_ref[hc:hc + 1]
            for rows in subs:
                st = jnp.dot(k_ref[rows, 2 * HEAD_DIM * h:2 * HEAD_DIM * (h + 1)], qt, preferred_element_type=F32)
                if causal:
                    kpos = j * tk + rows.start + lax.broadcasted_iota(jnp.int32, (sub, tq), 0)
                    qpos = i * tq + lax.broadcasted_iota(jnp.int32, (sub, tq), 1)
                    st = jnp.where(kpos <= qpos, st, NEG)
                s_refs[hc % len(s_refs)][rows, :] = st
                m_new = jnp.maximum(m_new, jnp.max(st, axis=0, keepdims=True))
                yield
            out.append(m_new)

        def accumulate(hc, m_new):
            h = hc // 2
            acc = jnp.exp2(m_ref[hc:hc + 1] - m_new) * acc_ref[hc]
            for rows in subs:
                p = jnp.exp2(s_refs[hc % len(s_refs)][rows, :] - m_new).astype(BF16)
                vt1 = _with_ones(vt_ref[DIFF_VDIM * h:DIFF_VDIM * (h + 1), rows])
                acc = acc + jnp.dot(vt1, p, preferred_element_type=F32)
                yield
            m_ref[hc:hc + 1] = m_new
            acc_ref[hc] = acc

        m_new = []
        for hc in range(DIFF_AHEAD):
            for _ in score(hc, m_new):
                pass
        for hc in range(n_maps):
            nxt = score(hc + DIFF_AHEAD, m_new) if hc + DIFF_AHEAD < n_maps else iter(())
            for _ in accumulate(hc, m_new[hc]):
                next(nxt, None)
            for _ in nxt:
                pass

    pl.when(j < first_diag)(functools.partial(step, False))
    pl.when(j >= first_diag)(functools.partial(step, True))

    @pl.when(j == ((i + 1) * tq - 1) // tk)
    def _():
        lam = _diff_lambda(lam_ref, lam_init)
        for h in range(DIFF_HEADS):
            a0, a1 = acc_ref[2 * h], acc_ref[2 * h + 1]
            o0 = a0[:DIFF_VDIM] / a0[DIFF_VDIM:DIFF_VDIM + 1]
            o1 = a1[:DIFF_VDIM] / a1[DIFF_VDIM:DIFF_VDIM + 1]
            a = o0 - lam * o1
            y = a * lax.rsqrt(jnp.mean(a * a, axis=0, keepdims=True) + EPS) * g_ref[...] * (1.0 - lam_init)
            o_ref[DIFF_VDIM * h:DIFF_VDIM * (h + 1), :] = y.astype(o_ref.dtype)


def _diff_prompt(qbt, kb, vbt, diff_lambda, subln_g, lam_init, tq, tk):
    n = kb.shape[0]
    assert n % tq == 0 and n % tk == 0 and tq % tk == 0
    pairs = [(i, j) for i in range(n // tq) for j in range(((i + 1) * tq - 1) // tk + 1)]
    it = jnp.asarray([p[0] for p in pairs], jnp.int32)
    jt = jnp.asarray([p[1] for p in pairs], jnp.int32)
    const = lambda shape: pl.BlockSpec(shape, lambda s, it, jt: (0,) * len(shape))
    grid_spec = pltpu.PrefetchScalarGridSpec(
        num_scalar_prefetch=2,
        grid=(len(pairs),),
        in_specs=[pl.BlockSpec((WB, tq), lambda s, it, jt: (0, it[s])),
                  pl.BlockSpec((tk, WB), lambda s, it, jt: (jt[s], 0)),
                  pl.BlockSpec((WV, tk), lambda s, it, jt: (0, jt[s])),
                  const(diff_lambda.shape), const((DIFF_VDIM, 1))],
        out_specs=pl.BlockSpec((WV, tq), lambda s, it, jt: (0, it[s])),
        scratch_shapes=[pltpu.VMEM((2 * DIFF_HEADS, tq), F32),
                        pltpu.VMEM((2 * DIFF_HEADS, DIFF_VDIM + SUM_ROWS, tq), F32),
                        ] + [pltpu.VMEM((tk, tq), F32)] * (DIFF_AHEAD + 1),
    )
    return pl.pallas_call(
        functools.partial(_diff_prompt_kernel, tq=tq, tk=tk, sub=min(tk, 2 * LANES), lam_init=lam_init),
        grid_spec=grid_spec,
        out_shape=jax.ShapeDtypeStruct((WV, n), BF16),
        compiler_params=_cparams(("arbitrary",)),
        name="diff_prompt",
    )(it, jt, qbt, kb, vbt, diff_lambda, subln_g.reshape(DIFF_VDIM, 1))


def _compress_weights(pos, k_w1, k_w2, v_w1, v_w2):
    hd = HEAD_DIM
    z = jnp.zeros((CMP_STRIDE, hd, hd), F32)

    def halves(w1):
        w3 = w1.reshape(CMP_BLOCK, hd, -1)
        return w3[:CMP_STRIDE], w3[CMP_STRIDE:]

    def expand(top, bot):
        rows = [[top, z, bot, z], [z, top, z, bot]]
        w = jnp.concatenate([jnp.concatenate(r, axis=2) for r in rows], axis=1)
        return w.reshape(CMP_STRIDE // 2, 2 * KV, 2 * KV).astype(BF16)

    pf = pos.reshape(1, -1)
    ck, cv = pf @ k_w1, pf @ v_w1
    z2 = jnp.zeros((hd, hd), F32)
    w2k = jnp.block([[k_w2, z2], [z2, k_w2]]).astype(BF16)
    w2v = jnp.block([[v_w2, z2], [z2, v_w2]]).astype(BF16)
    return (expand(*halves(k_w1)), expand(*halves(v_w1)), jnp.concatenate([ck, ck], axis=1),
            jnp.concatenate([cv, cv], axis=1), w2k, w2v, w2v.T)


def _compress_ab(x_ref, w_ref, n_sub, pitch=CMP_STRIDE):
    acc = jnp.zeros((n_sub, 2 * KV), F32)
    for r in range(0, CMP_STRIDE, 2):
        xr = jnp.concatenate([x_ref[pl.ds(r + d, n_sub, stride=pitch), :] for d in range(2)], axis=1)
        acc += jnp.dot(xr.astype(BF16), w_ref[r // 2], preferred_element_type=F32)
    return acc


def _compress_hidden(ab, c):
    n_sub = ab.shape[0]
    nxt = pltpu.roll(ab[:, KV:], n_sub - 1, axis=0)
    return jax.nn.gelu(ab[:, :KV] + nxt + c).astype(BF16)


def _compress_ab_kernel(xk_ref, xv_ref, wk_ref, wv_ref, abk_ref, abv_ref, *, n_sub):
    abk_ref[...] = _compress_ab(xk_ref, wk_ref, n_sub)
    abv_ref[...] = _compress_ab(xv_ref, wv_ref, n_sub)


def _compress_mlp_kernel(abk_ref, abv_ref, ck_ref, cv_ref, w2k_ref, w2v_ref, w2vt_ref, kc_ref, vc_ref, vct_ref):
    gk = _compress_hidden(abk_ref[...], ck_ref[...])
    gv = _compress_hidden(abv_ref[...], cv_ref[...])
    kc_ref[...] = jnp.dot(gk, w2k_ref[...], preferred_element_type=F32).astype(BF16)
    vc_ref[...] = jnp.dot(gv, w2v_ref[...], preferred_element_type=F32).astype(BF16)
    vct_ref[...] = lax.dot_general(w2vt_ref[...], gv, (((1,), (1,)), ((), ())),
                                   preferred_element_type=F32).astype(BF16)


def _compress(kv, cw, sub_tile):
    w1k, w1v, ck, cv, w2k, w2v, w2vt = cw
    n_sub = kv.shape[0] // CMP_STRIDE
    assert n_sub % sub_tile == 0
    ab_shape = jax.ShapeDtypeStruct((n_sub, 2 * KV), F32)
    ab_spec = pl.BlockSpec((sub_tile, 2 * KV), lambda i: (i, 0))
    abk, abv = pl.pallas_call(
        functools.partial(_compress_ab_kernel, n_sub=sub_tile),
        grid=(n_sub // sub_tile,),
        in_specs=[pl.BlockSpec((sub_tile * CMP_STRIDE, KV), lambda i: (i, 0)),
                  pl.BlockSpec((sub_tile * CMP_STRIDE, KV), lambda i: (i, 1)), _full(w1k.shape), _full(w1v.shape)],
        out_specs=[ab_spec, ab_spec],
        out_shape=[ab_shape, ab_shape],
        compiler_params=_cparams(("parallel",)),
        name="compress_ab",
    )(kv, kv, w1k, w1v)
    return pl.pallas_call(
        _compress_mlp_kernel,
        out_shape=[jax.ShapeDtypeStruct((n_sub, KV), BF16), jax.ShapeDtypeStruct((n_sub, KV), BF16),
                   jax.ShapeDtypeStruct((KV, n_sub), BF16)],
        compiler_params=pltpu.CompilerParams(vmem_limit_bytes=VMEM_LIMIT),
        name="compress_mlp",
    )(abk, abv, ck, cv, w2k, w2v, w2vt)


def _importance_matrix(n_sel, n_cmp):
    j = jnp.arange(n_sel)[:, None]
    n = jnp.arange(n_cmp)[None, :]
    return ((n >= SEL_PER_CMP * j - 1) & (n <= SEL_PER_CMP * j + SEL_PER_CMP - 1)).astype(BF16)


def _split3(x):
    hi = x.astype(BF16)
    r = x - hi.astype(F32)
    mid = r.astype(BF16)
    lo = (r - mid.astype(F32)).astype(BF16)
    return hi, mid, lo


def _tile_lanes(x, k):
    return jnp.concatenate([x] * k, axis=1)


def _select_blocks(score, blk):
    big = jnp.int32(2 ** 30)

    def body(_, carry):
        sc, bias = carry
        mx = jnp.max(sc, axis=0, keepdims=True)
        first = jnp.min(jnp.where(sc == mx, blk, big), axis=0, keepdims=True)
        hit = blk == first
        return jnp.where(hit, -3e38, sc), jnp.where(hit, 0.0, bias)

    n_pick = min(TOP_N, score.shape[0])
    return lax.fori_loop(0, n_pick, body, (score, jnp.full(score.shape, NEG, F32)))[1]


def _softmax_cols(st, valid):
    sm = jnp.where(valid, st, NEG)
    m = jnp.max(sm, axis=0, keepdims=True)
    e = jnp.where(valid, jnp.exp2(sm - m), 0.0)
    return e / jnp.maximum(jnp.sum(e, axis=0, keepdims=True), 1e-30)


NSA_GROUP_CHUNKS = 4
NSA_AHEAD = 3


def _nsa_prompt_kernel(qat_ref, qart_ref, gat_ref, kc_ref, vct_ref, imp_ref, hot_ref, ks_ref, vst_ref, kw_ref,
                       vwt_ref, o_ref, bias_ref, ocmp_ref, m_ref, acc_ref, *s_refs, tq, tk, sub, n_cmp):
    i = pl.program_id(0)
    g4 = NSA_GROUP
    ncp = kc_ref.shape[0]
    n_sel = imp_ref.shape[0]
    q0 = i * tq
    qlane = q0 + lax.broadcasted_iota(jnp.int32, (1, tq), 1)

    def heads_t(ref, h):
        return jnp.concatenate([ref[HEAD_DIM * (g4 * h + g):HEAD_DIM * (g4 * h + g + 1), :] for g in range(g4)],
                               axis=1)

    for h in range(NSA_KV_HEADS):
        qt = _pad_head(heads_t(qat_ref, h), h)
        st = jnp.dot(kc_ref[...], qt, preferred_element_type=F32)
        nrow = lax.broadcasted_iota(jnp.int32, (ncp, tq), 0)
        valid = (nrow * CMP_STRIDE + (CMP_BLOCK - 1) <= qlane) & (nrow < n_cmp)
        p = _softmax_cols(st, _tile_lanes(valid, g4))
        ocmp_ref[h] = jnp.dot(vct_ref[HEAD_DIM * h:HEAD_DIM * (h + 1), :], p.astype(BF16),
                              preferred_element_type=F32)
        psum = p[:, :tq]
        for g in range(1, g4):
            psum = psum + p[:, g * tq:(g + 1) * tq]
        imp = jnp.zeros((n_sel, tq), F32)
        for part in _split3(psum):
            imp += jnp.dot(imp_ref[...], part, preferred_element_type=F32)
        blk = lax.broadcasted_iota(jnp.int32, (n_sel, tq), 0)
        cur = qlane // SEL_BLOCK
        forced = (blk == 0) | (blk == cur) | (blk == cur - 1)
        score = jnp.where(blk > cur, -1.0, jnp.where(forced, FORCED_SCORE, imp))
        bias_ref[h] = _select_blocks(score, blk)

    qrts = [_pad_head(heads_t(qart_ref, h), h) for h in range(NSA_KV_HEADS)]
    per_chunk = tk // SEL_BLOCK
    zpad = jnp.zeros((LANES - per_chunk, g4 * tq), F32)
    subs = [slice(r, r + sub) for r in range(0, tk, sub)]
    n_key_chunks = ks_ref.shape[0] // tk
    n_maps = NSA_GROUP_CHUNKS * NSA_KV_HEADS
    m_ref[...] = jnp.full_like(m_ref, NEG)
    acc_ref[...] = jnp.zeros_like(acc_ref)

    def group(c_base, masked):
        m_hist = [[m_ref[h:h + 1]] for h in range(NSA_KV_HEADS)]

        def where(k):
            cc, h = divmod(k, NSA_KV_HEADS)
            c = c_base + cc
            cl = jnp.minimum(c, n_key_chunks - 1) if masked else c
            return h, c, cl, pl.multiple_of(cl * tk, tk)

        def score(k):
            h, c, cl, k0 = where(k)
            b8 = bias_ref[h, pl.ds(pl.multiple_of(cl * per_chunk, per_chunk), per_chunk), :]
            qx = jnp.concatenate([qrts[h], jnp.concatenate([_tile_lanes(b8, g4), zpad], axis=0).astype(BF16)],
                                 axis=0)
            m_run = m_hist[h][-1]
            for rows in subs:
                kx = jnp.concatenate([ks_ref[pl.ds(k0 + rows.start, sub), :], hot_ref[rows, :]], axis=1)
                st = jnp.dot(kx, qx, preferred_element_type=F32)
                if masked:
                    kpos = c * tk + rows.start + lax.broadcasted_iota(jnp.int32, (sub, tq), 0)
                    st = jnp.where(_tile_lanes(kpos <= qlane, g4), st, NEG)
                s_refs[k % len(s_refs)][rows, :] = st
                m_run = jnp.maximum(m_run, jnp.max(st, axis=0, keepdims=True))
                yield
            m_hist[h].append(m_run)

        def accumulate(k):
            h, _, _, k0 = where(k)
            cc = k // NSA_KV_HEADS
            m_new = m_hist[h][cc + 1]
            acc = jnp.exp2(m_hist[h][cc] - m_new) * acc_ref[h]
            for rows in subs:
                p = jnp.exp2(s_refs[k % len(s_refs)][rows, :] - m_new).astype(BF16)
                vt1 = _with_ones(vst_ref[HEAD_DIM * h:HEAD_DIM * (h + 1), pl.ds(k0 + rows.start, sub)])
                acc = acc + jnp.dot(vt1, p, preferred_element_type=F32)
                yield
            acc_ref[h] = acc

        for k in range(NSA_AHEAD):
            for _ in score(k):
                pass
        for k in range(n_maps):
            nxt = score(k + NSA_AHEAD) if k + NSA_AHEAD < n_maps else iter(())
            for _ in accumulate(k):
                next(nxt, None)
            for _ in nxt:
                pass
        for h in range(NSA_KV_HEADS):
            m_ref[h:h + 1] = m_hist[h][-1]

    n_plain = (q0 // tk) // NSA_GROUP_CHUNKS

    def plain(gi, carry):
        group(gi * NSA_GROUP_CHUNKS, False)
        return carry

    lax.fori_loop(0, n_plain, plain, 0)
    group(n_plain * NSA_GROUP_CHUNKS, True)

    for h in range(NSA_KV_HEADS):
        qrt = qrts[h]
        o_cmp = ocmp_ref[h]
        acc = acc_ref[h]
        o_sel = acc[:HEAD_DIM] / jnp.maximum(acc[HEAD_DIM:HEAD_DIM + 1], 1e-30)

        nw = WINDOW + tq
        w0 = pl.multiple_of(jnp.maximum(q0 - WINDOW, 0), LANES)
        st = jnp.dot(kw_ref[pl.ds(w0, nw), :], qrt, preferred_element_type=F32)
        kpos = w0 + lax.broadcasted_iota(jnp.int32, (nw, tq), 0)
        valid = (kpos <= qlane) & (kpos >= qlane - WINDOW)
        p = _softmax_cols(st, _tile_lanes(valid, g4))
        o_win = jnp.dot(vwt_ref[HEAD_DIM * h:HEAD_DIM * (h + 1), pl.ds(w0, nw)], p.astype(BF16),
                        preferred_element_type=F32)

        for g in range(g4):
            r = 3 * (g4 * h + g)
            sl = slice(g * tq, (g + 1) * tq)
            o = (gat_ref[r:r + 1, :] * o_cmp[:, sl] + gat_ref[r + 1:r + 2, :] * o_sel[:, sl]
                 + gat_ref[r + 2:r + 3, :] * o_win[:, sl])
            o_ref[HEAD_DIM * (g4 * h + g):HEAD_DIM * (g4 * h + g + 1), :] = o.astype(o_ref.dtype)


def _nsa_prompt(qat, qart, gat, kc, vct, ks, vst, kw, vwt, tq, tk):
    n = ks.shape[0]
    n_sub = kc.shape[0]
    n_cmp = n_sub - 1
    n_sel = n // SEL_BLOCK
    assert n % tq == 0 and n % tk == 0 and tk % SEL_BLOCK == 0 and n >= WINDOW + tq and tq % LANES == 0
    assert tk % tq == 0 and tk // SEL_BLOCK <= LANES
    imp = _importance_matrix(n_sel, n_sub)
    hot = (jnp.arange(tk)[:, None] // SEL_BLOCK == jnp.arange(LANES)[None, :]).astype(BF16)
    col = lambda r: pl.BlockSpec((r, tq), lambda i: (0, i))
    return pl.pallas_call(
        functools.partial(_nsa_prompt_kernel, tq=tq, tk=tk, sub=min(tk, 2 * LANES), n_cmp=n_cmp),
        grid=(n // tq,),
        in_specs=[col(WA), col(WA), col(32), _full(kc.shape), _full(vct.shape), _full(imp.shape), _full(hot.shape),
                  _full(ks.shape), _full(vst.shape), _full(kw.shape), _full(vwt.shape)],
        out_specs=col(WA),
        out_shape=jax.ShapeDtypeStruct((WA, n), BF16),
        scratch_shapes=[pltpu.VMEM((NSA_KV_HEADS, n_sel, tq), F32),
                        pltpu.VMEM((NSA_KV_HEADS, HEAD_DIM, NSA_GROUP * tq), F32),
                        pltpu.VMEM((NSA_KV_HEADS, NSA_GROUP * tq), F32),
                        pltpu.VMEM((NSA_KV_HEADS, HEAD_DIM + SUM_ROWS, NSA_GROUP * tq), F32),
                        ] + [pltpu.VMEM((tk, NSA_GROUP * tq), F32)] * (NSA_AHEAD + 1),
        compiler_params=_cparams(("parallel",)),
        name="nsa_prompt",
    )(qat, qart, gat, kc, vct, imp, hot, ks, vst, kw, vwt)


ROUTE_COLS = LANES


def _first_lane_of_max(v, lane):
    mx = jnp.max(v, axis=-1, keepdims=True)
    return mx, jnp.min(jnp.where(v == mx, lane, ROUTE_COLS), axis=-1, keepdims=True)


def _merge_kernel(oa_ref, ob_ref, gm_ref, x_ref, wa_ref, wb_ref, wo_ref, g_ref, rwh_ref, rwl_ref, rb_ref,
                  x1_ref, h2_ref, re_ref, rw_ref, *, transposed):
    d = x_ref.shape[1]
    dims = (((0,), (0,)), ((), ())) if transposed else (((1,), (0,)), ((), ()))
    ya = lax.dot_general(oa_ref[...], wa_ref[...], dims, preferred_element_type=F32)
    yb = lax.dot_general(ob_ref[...], wb_ref[...], dims, preferred_element_type=F32)
    mix = gm_ref[:, :d] * ya + gm_ref[:, d:] * yb
    x1 = x_ref[...] + jnp.dot(mix.astype(BF16), wo_ref[...], preferred_element_type=F32)
    x1_ref[...] = x1
    h2 = x1 * lax.rsqrt(jnp.mean(x1 * x1, axis=-1, keepdims=True) + EPS) * g_ref[...]
    h2_ref[...] = h2

    hi = h2.astype(BF16)
    lo = (h2 - hi.astype(F32)).astype(BF16)
    logits = (jnp.dot(hi, rwh_ref[...], preferred_element_type=F32)
              + jnp.dot(lo, rwh_ref[...], preferred_element_type=F32)
              + jnp.dot(hi, rwl_ref[...], preferred_element_type=F32)) + rb_ref[...]
    lane = lax.broadcasted_iota(jnp.int32, logits.shape, 1)
    is_g = lane < N_GROUPS
    gl = jnp.where(is_g, logits, NEG)
    gmx, grp = _first_lane_of_max(gl, lane)
    p_grp = 1.0 / jnp.sum(jnp.where(is_g, jnp.exp(gl - gmx), 0.0), axis=-1, keepdims=True)
    e_id = lane - N_GROUPS
    in_grp = (e_id >= 0) & (e_id < N_EXPERTS) & (e_id // EXPERTS_PER_GROUP == grp)
    el = jnp.where(in_grp, logits, NEG)
    emx = jnp.max(el, axis=-1, keepdims=True)
    ee = jnp.where(in_grp, jnp.exp(el - emx), -1.0)
    e1, i1 = _first_lane_of_max(ee, lane)
    e2, i2 = _first_lane_of_max(jnp.where(lane == i1, -1.0, ee), lane)
    inv = p_grp / (e1 + e2)
    re_ref[...] = jnp.where(lane == 0, i1 - N_GROUPS, jnp.where(lane == 1, i2 - N_GROUPS, 0))
    rw_ref[...] = jnp.where(lane == 0, e1 * inv, jnp.where(lane == 1, e2 * inv, 0.0))


def _merge(oa, ob, gm, x, wa, wb, wo, ffn_g, rwh, rwl, rb, tm, transposed):
    n, d = x.shape
    assert n % tm == 0
    row = lambda c: pl.BlockSpec((tm, c), lambda i: (i, 0))
    o_spec = pl.BlockSpec((WA, tm), lambda i: (0, i)) if transposed else row(WA)
    return pl.pallas_call(
        functools.partial(_merge_kernel, transposed=transposed),
        grid=(n // tm,),
        in_specs=[o_spec, o_spec, row(2 * d), row(d), _full(wa.shape), _full(wb.shape), _full(wo.shape),
                  _full((1, d)), _full(rwh.shape), _full(rwl.shape), _full((1, ROUTE_COLS))],
        out_specs=[row(d), row(d), row(ROUTE_COLS), row(ROUTE_COLS)],
        out_shape=[jax.ShapeDtypeStruct((n, d), F32), jax.ShapeDtypeStruct((n, d), F32),
                   jax.ShapeDtypeStruct((n, ROUTE_COLS), jnp.int32), jax.ShapeDtypeStruct((n, ROUTE_COLS), F32)],
        compiler_params=_cparams(("parallel",)),
        name="merge_route",
    )(oa, ob, gm, x, wa, wb, wo, ffn_g.reshape(1, d), rwh, rwl, rb)


def _route_plan_kernel(re_ref, dest_ref, be_ref, cnt_ref, start_ref, carry_ref, tri_ref, *, bm, n_blocks):
    phase = pl.program_id(0)
    tile = pl.program_id(1)
    tm = re_ref.shape[0]
    lane = lax.broadcasted_iota(jnp.int32, (tm, ROUTE_COLS), 1)
    hot = [lane == re_ref[:, k:k + 1] for k in range(EXPERT_TOP_K)]
    both = sum(h.astype(F32) for h in hot)

    @pl.when((phase == 0) & (tile == 0))
    def _():
        cnt_ref[...] = jnp.zeros_like(cnt_ref)
        r = lax.broadcasted_iota(jnp.int32, (tm, tm), 0)
        c = lax.broadcasted_iota(jnp.int32, (tm, tm), 1)
        tri_ref[...] = (c < r).astype(BF16)

    @pl.when(phase == 0)
    def _():
        cnt_ref[...] += jnp.sum(both, axis=0, keepdims=True)

    @pl.when((phase == 1) & (tile == 0))
    def _():
        cnt = jnp.broadcast_to(cnt_ref[...], (8, ROUTE_COLS))
        padded = jnp.ceil(cnt / bm) * bm
        l8 = lax.broadcasted_iota(jnp.int32, (8, ROUTE_COLS), 1)
        end = padded
        shift = 1
        while shift < N_EXPERTS:
            end = end + jnp.where(l8 >= shift, pltpu.roll(end, shift, axis=1), 0.0)
            shift *= 2
        start_ref[...] = (end - padded)[:1]
        carry_ref[...] = jnp.zeros_like(carry_ref)
        nbp = be_ref.shape[0]
        first = (lax.broadcasted_iota(jnp.int32, (nbp, ROUTE_COLS), 0) * bm).astype(F32)
        lb = lax.broadcasted_iota(jnp.int32, (nbp, ROUTE_COLS), 1)
        hits = jnp.where((lb < N_EXPERTS) & (jnp.broadcast_to(end[:1], (nbp, ROUTE_COLS)) <= first), 1.0, 0.0)
        be = jnp.minimum(jnp.sum(hits, axis=1, keepdims=True), N_EXPERTS - 1.0)
        be_ref[...] = jnp.broadcast_to(be, (nbp, ROUTE_COLS)).astype(jnp.int32)

    @pl.when(phase == 1)
    def _():
        before = jnp.dot(tri_ref[...], both.astype(BF16), preferred_element_type=F32) + carry_ref[...]
        slot = before + start_ref[...]
        dest = [jnp.sum(jnp.where(h, slot, 0.0), axis=1, keepdims=True) for h in hot]
        dest_ref[...] = jnp.where(lane == 0, dest[0], jnp.where(lane == 1, dest[1], 0.0)).astype(jnp.int32)
        carry_ref[...] += jnp.sum(both, axis=0, keepdims=True)


def _route_plan(experts, bm, tm):
    n = experts.shape[0]
    assert n % tm == 0
    n_blocks = -(-n * EXPERT_TOP_K // bm) + N_EXPERTS
    nbp = -(-n_blocks // 8) * 8
    dest, be = pl.pallas_call(
        functools.partial(_route_plan_kernel, bm=bm, n_blocks=n_blocks),
        grid=(2, n // tm),
        in_specs=[pl.BlockSpec((tm, ROUTE_COLS), lambda p, t: (t, 0))],
        out_specs=[pl.BlockSpec((tm, ROUTE_COLS), lambda p, t: (t * p, 0)),
                   pl.BlockSpec((nbp, ROUTE_COLS), lambda p, t: (0, 0))],
        out_shape=[jax.ShapeDtypeStruct((n, ROUTE_COLS), jnp.int32),
                   jax.ShapeDtypeStruct((nbp, ROUTE_COLS), jnp.int32)],
        scratch_shapes=[pltpu.VMEM((1, ROUTE_COLS), F32), pltpu.VMEM((1, ROUTE_COLS), F32),
                        pltpu.VMEM((1, ROUTE_COLS), F32), pltpu.VMEM((tm, tm), BF16)],
        compiler_params=_cparams(("arbitrary", "arbitrary")),
        name="route_plan",
    )(experts)
    return dest[:, :EXPERT_TOP_K].reshape(-1), be[:n_blocks, 0], n_blocks


def _dispatch_kernel(dest_ref, h_ref, xs_in, xs_hbm, sem, *, tok0):
    del xs_in
    tm = h_ref.shape[0]
    base = (tok0 + pl.program_id(0) * tm) * EXPERT_TOP_K

    def copy(r, k):
        return pltpu.make_async_copy(h_ref.at[pl.ds(r, 1), :],
                                     xs_hbm.at[pl.ds(dest_ref[base + EXPERT_TOP_K * r + k], 1), :], sem)

    def start(r, c):
        for k in range(EXPERT_TOP_K):
            copy(r, k).start()
        return c

    def wait(r, c):
        for k in range(EXPERT_TOP_K):
            copy(r, k).wait()
        return c

    lax.fori_loop(0, tm, start, 0, unroll=4)
    lax.fori_loop(0, tm, wait, 0, unroll=4)


def _dispatch_rows(dest, h, xs, tok0, tm):
    n, d = h.shape
    assert n % tm == 0
    grid_spec = pltpu.PrefetchScalarGridSpec(
        num_scalar_prefetch=1,
        grid=(n // tm,),
        in_specs=[pl.BlockSpec((tm, d), lambda i, dst: (i, 0)), pl.BlockSpec(memory_space=pl.ANY)],
        out_specs=pl.BlockSpec(memory_space=pl.ANY),
        scratch_shapes=[pltpu.SemaphoreType.DMA(())],
    )
    return pl.pallas_call(
        functools.partial(_dispatch_kernel, tok0=tok0),
        grid_spec=grid_spec,
        out_shape=jax.ShapeDtypeStruct(xs.shape, xs.dtype),
        input_output_aliases={2: 0},
        compiler_params=_cparams(("arbitrary",)),
        name="moe_dispatch",
    )(dest, h, xs)


def _expert_kernel(be_ref, xs_ref, wg_ref, wu_ref, wd_ref, ys_ref, wgb, wub, wdb):
    b = pl.program_id(0)

    @pl.when((b == 0) | (be_ref[b] != be_ref[jnp.maximum(b - 1, 0)]))
    def _():
        wgb[...] = wg_ref[0].astype(BF16)
        wub[...] = wu_ref[0].astype(BF16)
        wdb[...] = wd_ref[0].astype(BF16)

    xb = xs_ref[...].astype(BF16)
    gate = jnp.dot(xb, wgb[...], preferred_element_type=F32)
    up = jnp.dot(xb, wub[...], preferred_element_type=F32)
    act = (jax.nn.silu(gate) * up).astype(BF16)
    ys_ref[...] = jnp.dot(act, wdb[...], preferred_element_type=F32)


def _expert_ffn(block_e, xs, w_gate, w_up, w_down, bm):
    n_slots, d = xs.shape
    ff = w_gate.shape[2]
    grid_spec = pltpu.PrefetchScalarGridSpec(
        num_scalar_prefetch=1,
        grid=(n_slots // bm,),
        in_specs=[pl.BlockSpec((bm, d), lambda b, be: (b, 0)),
                  pl.BlockSpec((1, d, ff), lambda b, be: (be[b], 0, 0)),
                  pl.BlockSpec((1, d, ff), lambda b, be: (be[b], 0, 0)),
                  pl.BlockSpec((1, ff, d), lambda b, be: (be[b], 0, 0))],
        out_specs=pl.BlockSpec((bm, d), lambda b, be: (b, 0)),
        scratch_shapes=[pltpu.VMEM((d, ff), BF16), pltpu.VMEM((d, ff), BF16), pltpu.VMEM((ff, d), BF16)],
    )
    return pl.pallas_call(
        _expert_kernel,
        grid_spec=grid_spec,
        out_shape=jax.ShapeDtypeStruct((n_slots, d), F32),
        compiler_params=_cparams(("arbitrary",)),
        name="expert_ffn",
    )(block_e, xs, w_gate, w_up, w_down)


def _nt(a, b):
    return lax.dot_general(a, b, (((1,), (1,)), ((), ())), preferred_element_type=F32)


def _flash_rows(s, v, m_prev, l_prev, acc_prev, v_transposed=False):
    m_new = jnp.maximum(m_prev, jnp.max(s, axis=-1, keepdims=True))
    alpha = jnp.exp2(m_prev - m_new)
    p = jnp.exp2(s - m_new)
    l_new = alpha * l_prev + jnp.sum(p, axis=-1, keepdims=True)
    pb = p.astype(BF16)
    pv = _nt(pb, v) if v_transposed else jnp.dot(pb, v, preferred_element_type=F32)
    return m_new, l_new, alpha * acc_prev + pv


def _page_copies(pt_ref, first_page, n, src_hbm, dst, sem, slab):
    out = []
    for k in range(n):
        p0 = pl.multiple_of(pt_ref[first_page + k] * slab, slab)
        out.append(pltpu.make_async_copy(src_hbm.at[pl.ds(p0, slab), :], dst.at[pl.ds(k * slab, slab), :], sem))
    return out


def _diff_sample_kernel(pt_ref, q_ref, kc_hbm, vc_hbm, kn_ref, vn_ref, lam_ref, g_ref, o_ref,
                        kbuf, vbuf, sem, m_ref, l_ref, acc_ref, *, page, ppc, n_chunks, lam_init):
    b = pl.program_id(0)
    c = pl.program_id(1)
    step = b * n_chunks + c
    total = pl.num_programs(0) * n_chunks
    t = q_ref.shape[1]
    kslab = WB * page // LANES
    vslab = page * DIFF_HEADS

    def copies(s, slot):
        first = s * ppc
        return (_page_copies(pt_ref, first, ppc, kc_hbm, kbuf.at[slot], sem.at[0, slot], kslab)
                + _page_copies(pt_ref, first, ppc, vc_hbm, vbuf.at[slot], sem.at[1, slot], vslab))

    slot = step % 2

    @pl.when(step == 0)
    def _():
        for cp in copies(0, 0):
            cp.start()

    @pl.when(step + 1 < total)
    def _():
        for cp in copies(step + 1, 1 - slot):
            cp.start()

    @pl.when(c == 0)
    def _():
        m_ref[...] = jnp.full_like(m_ref, NEG)
        l_ref[...] = jnp.zeros_like(l_ref)
        acc_ref[...] = jnp.zeros_like(acc_ref)

    for cp in copies(step, slot):
        cp.wait()

    q = q_ref[0].astype(F32)
    lane = lax.broadcasted_iota(jnp.int32, (t, 2 * HEAD_DIM), 1)

    def q_pair(h):
        qh = q[:, 2 * HEAD_DIM * h:2 * HEAD_DIM * (h + 1)]
        return jnp.concatenate([jnp.where(lane < HEAD_DIM, qh, 0.0), jnp.where(lane >= HEAD_DIM, qh, 0.0)],
                               axis=0).astype(BF16)

    two_hd = 2 * HEAD_DIM
    for h in range(DIFF_HEADS):
        kt = jnp.concatenate([kbuf[slot, kslab * k + two_hd * h:kslab * k + two_hd * (h + 1), :]
                              for k in range(ppc)], axis=1).astype(BF16)
        v2 = vbuf[slot, pl.ds(h, ppc * page, stride=DIFF_HEADS), :].astype(BF16)
        s = jnp.dot(q_pair(h), kt, preferred_element_type=F32)
        m, l, acc = _flash_rows(s, v2, m_ref[h], l_ref[h], acc_ref[h])
        m_ref[h] = m
        l_ref[h] = l
        acc_ref[h] = acc

    @pl.when(c == n_chunks - 1)
    def _():
        lam = _diff_lambda(lam_ref, lam_init)
        trow = lax.broadcasted_iota(jnp.int32, (2 * t, t), 0) % t
        tcol = lax.broadcasted_iota(jnp.int32, (2 * t, t), 1)
        for h in range(DIFF_HEADS):
            kn = kn_ref[0][:, 2 * HEAD_DIM * h:2 * HEAD_DIM * (h + 1)].astype(BF16)
            vn = vn_ref[0][:, DIFF_VDIM * h:DIFF_VDIM * (h + 1)].astype(BF16)
            s = jnp.where(tcol <= trow, _nt(q_pair(h), kn), NEG)
            m, l, acc = _flash_rows(s, vn, m_ref[h], l_ref[h], acc_ref[h])
            o = acc / l
            a = o[:t] - lam * o[t:]
            y = a * lax.rsqrt(jnp.mean(a * a, axis=-1, keepdims=True) + EPS) * g_ref[...] * (1.0 - lam_init)
            o_ref[0, :, DIFF_VDIM * h:DIFF_VDIM * (h + 1)] = y.astype(o_ref.dtype)


def _diff_sample(page_table, q3, kcache, vcache, kn3, vn3, diff_lambda, subln_g, lam_init, page, ppc):
    db, t, _ = q3.shape
    n_pages = page_table.shape[1]
    assert n_pages % ppc == 0 and t == 8
    n_chunks = n_pages // ppc
    tks = ppc * page
    per_b = lambda w: pl.BlockSpec((1, t, w), lambda b, c, pt: (b, 0, 0))
    const = lambda shape: pl.BlockSpec(shape, lambda b, c, pt: (0,) * len(shape))
    grid_spec = pltpu.PrefetchScalarGridSpec(
        num_scalar_prefetch=1,
        grid=(db, n_chunks),
        in_specs=[per_b(WB), pl.BlockSpec(memory_space=pl.ANY), pl.BlockSpec(memory_space=pl.ANY),
                  per_b(WB), per_b(WV), const(diff_lambda.shape), const((1, DIFF_VDIM))],
        out_specs=per_b(WV),
        scratch_shapes=[pltpu.VMEM((2, tks * WB // LANES, LANES), F32),
                        pltpu.VMEM((2, tks * DIFF_HEADS, DIFF_VDIM), F32),
                        pltpu.SemaphoreType.DMA((2, 2)),
                        pltpu.VMEM((DIFF_HEADS, 2 * t, 1), F32), pltpu.VMEM((DIFF_HEADS, 2 * t, 1), F32),
                        pltpu.VMEM((DIFF_HEADS, 2 * t, DIFF_VDIM), F32)],
    )
    return pl.pallas_call(
        functools.partial(_diff_sample_kernel, page=page, ppc=ppc, n_chunks=n_chunks, lam_init=lam_init),
        grid_spec=grid_spec,
        out_shape=jax.ShapeDtypeStruct((db, t, WV), BF16),
        compiler_params=_cparams(("arbitrary", "arbitrary")),
        name="diff_sample",
    )(page_table.reshape(-1), q3, kcache, vcache, kn3, vn3, diff_lambda, subln_g.reshape(1, DIFF_VDIM))


def _softmax_rows(s, valid):
    sm = jnp.where(valid, s, NEG)
    m = jnp.max(sm, axis=-1, keepdims=True)
    e = jnp.where(valid, jnp.exp2(sm - m), 0.0)
    return e / jnp.maximum(jnp.sum(e, axis=-1, keepdims=True), 1e-30)


def _select_blocks_rows(score, blk):
    big = jnp.int32(2 ** 30)

    def body(_, carry):
        sc, bias = carry
        mx = jnp.max(sc, axis=-1, keepdims=True)
        first = jnp.min(jnp.where(sc == mx, blk, big), axis=-1, keepdims=True)
        hit = blk == first
        return jnp.where(hit, -3e38, sc), jnp.where(hit, 0.0, bias)

    return lax.fori_loop(0, TOP_N, body, (score, jnp.full(score.shape, NEG, F32)))[1]


def _pad_rows(x, rows):
    return jnp.concatenate([x, jnp.zeros((rows - x.shape[0], x.shape[1]), x.dtype)], axis=0)


TOK_PITCH = CMP_STRIDE + 1


def _nsa_sample_kernel(pt_ref, q_ref, qr_ref, ga_ref, cache_hbm, new_ref, wst_ref, wnew_ref,
                       w1k_ref, w1v_ref, ck_ref, cv_ref, w2k_ref, w2v_ref, impt_ref, exp_ref, o_ref,
                       ring, sring, kctok, vctok, abk_ref, abv_ref, s0_ref, s1_ref, sem, ssem, *, page, n_pages, ppc,
                       past, n_sel):
    b = pl.program_id(0)
    nb = pl.num_programs(0)
    t = q_ref.shape[1]
    g4 = NSA_GROUP
    n_sub = past // CMP_STRIDE
    n_cmp = n_sub - 1
    n_selp = impt_ref.shape[1]
    slab = 4 * KV * page // LANES
    half = slab // 2
    n_chunks = n_pages // ppc
    tks = ppc * page

    def copies(bb, c, second, dst, s):
        first = bb * n_pages + c * ppc
        out = []
        for k in range(ppc):
            p0 = pl.multiple_of(pt_ref[first + k] * slab + half * second, half)
            out.append(pltpu.make_async_copy(cache_hbm.at[pl.ds(p0, half), :], dst.at[pl.ds(k * half, half), :], s))
        return out

    def cmp_copies(bb, c):
        return copies(bb, c, 0, ring.at[c % 2], sem.at[c % 2])

    def sel_copies(c):
        return copies(b, c, 1, sring.at[c % 3], ssem.at[c % 3])

    def begin(jj):
        if jj + 1 < n_chunks:
            for cp in cmp_copies(b, jj + 1):
                cp.start()
        else:
            @pl.when(b + 1 < nb)
            def _():
                for cp in cmp_copies(b + 1, 0):
                    cp.start()
        for cp in cmp_copies(b, jj):
            cp.wait()
        return jj % 2

    @pl.when(b == 0)
    def _():
        for cp in cmp_copies(0, 0):
            cp.start()

    for c in range(min(2, n_chunks)):
        for cp in sel_copies(c):
            cp.start()

    subs_pc = tks // CMP_STRIDE

    def compress_chunk(jj):
        base = TOK_PITCH * subs_pc * jj
        for x_ref, w_ref, ab_ref in ((kctok, w1k_ref, abk_ref), (vctok, w1v_ref, abv_ref)):
            acc = jnp.zeros((subs_pc, 2 * KV), F32)
            for r in range(0, CMP_STRIDE, 2):
                xr = jnp.concatenate([x_ref[pl.ds(base + r + d, subs_pc, stride=TOK_PITCH), :] for d in range(2)],
                                     axis=1)
                acc += jnp.dot(xr.astype(BF16), w_ref[r // 2], preferred_element_type=F32)
                yield
            ab_ref[jj * subs_pc:(jj + 1) * subs_pc, :] = acc

    prev = iter(())
    for jj in range(n_chunks):
        slot = begin(jj)
        for k in range(ppc):
            r0 = (jj * ppc + k) * page
            for src, dst in ((ring[slot, k * half:k * half + KV, :].T, kctok),
                             (ring[slot, k * half + KV:(k + 1) * half, :].T, vctok)):
                for u in range(page // CMP_STRIDE):
                    row = TOK_PITCH * (r0 // CMP_STRIDE + u)
                    dst[row:row + CMP_STRIDE, :] = src[u * CMP_STRIDE:(u + 1) * CMP_STRIDE]
            next(prev, None)
        for _ in prev:
            pass
        prev = compress_chunk(jj)
    for _ in prev:
        pass
    kcmp = jnp.dot(_compress_hidden(abk_ref[...], ck_ref[...]), w2k_ref[...],
                   preferred_element_type=F32).astype(BF16)
    vcmp = jnp.dot(_compress_hidden(abv_ref[...], cv_ref[...]), w2v_ref[...],
                   preferred_element_type=F32).astype(BF16)

    q = q_ref[0].astype(F32)
    qr = qr_ref[0].astype(F32)
    ga = ga_ref[0]
    lane = lax.broadcasted_iota(jnp.int32, (t, LANES), 1)
    trow = lax.broadcasted_iota(jnp.int32, (g4 * t, 1), 0) % t
    qpos = past + trow
    qpos_t = past + lax.broadcasted_iota(jnp.int32, (t, 1), 0)

    def to_half(x, have, want):
        return x if have == want else pltpu.roll(x, HEAD_DIM, axis=1)

    def q_rows(qq, h):
        keep = (lane >= HEAD_DIM * h) & (lane < HEAD_DIM * (h + 1))
        rows = []
        for g in range(g4):
            hd = g4 * h + g
            tile = to_half(qq[:, LANES * (hd // 2):LANES * (hd // 2 + 1)], hd % 2, h)
            rows.append(jnp.where(keep, tile, 0.0))
        return jnp.concatenate(rows, axis=0).astype(BF16)

    new = new_ref[0]
    ksn = _pad_rows(new[:, 2 * KV:3 * KV], LANES).astype(BF16)
    vsn = _pad_rows(new[:, 3 * KV:4 * KV], LANES).astype(BF16)
    kwst = wst_ref[0][:KV, :].astype(BF16)
    vwst = wst_ref[0][KV:, :].astype(BF16)
    kwn = _pad_rows(wnew_ref[0][:, :KV], LANES).astype(BF16)
    vwn = _pad_rows(wnew_ref[0][:, KV:], LANES).astype(BF16)
    wbuf = wst_ref.shape[2]
    ncol = lax.broadcasted_iota(jnp.int32, (g4 * t, LANES), 1)
    new_ok = (ncol < t) & (ncol <= trow)
    blocks_per_chunk = tks // SEL_BLOCK
    out_tiles = [[None, None] for _ in range(WA // LANES)]
    o_cmps, scores, qrhs = [], [], []
    blk = lax.broadcasted_iota(jnp.int32, (t, n_selp), 1)

    for h in range(NSA_KV_HEADS):
        s = _nt(q_rows(q, h), kcmp)
        nidx = lax.broadcasted_iota(jnp.int32, s.shape, 1)
        p = _softmax_rows(s, (nidx * CMP_STRIDE + (CMP_BLOCK - 1) <= qpos) & (nidx < n_cmp))
        o_cmps.append(jnp.dot(p.astype(BF16), vcmp, preferred_element_type=F32))
        psum = p[:t]
        for g in range(1, g4):
            psum = psum + p[g * t:(g + 1) * t]
        imp = jnp.zeros((t, n_selp), F32)
        for part in _split3(psum):
            imp += jnp.dot(part, impt_ref[...], preferred_element_type=F32)
        cur = qpos_t // SEL_BLOCK
        forced = (blk == 0) | (blk == cur) | (blk == cur - 1)
        score = jnp.where(blk > cur, -1.0, jnp.where(forced, FORCED_SCORE, imp))
        scores.append(jnp.where(blk < n_sel, score, -3e38))
        qrhs.append(q_rows(qr, h))

    nh = NSA_KV_HEADS
    sc = jnp.concatenate(scores + [jnp.full((LANES - nh * t, n_selp), -3e38, F32)], axis=0).T
    n_sel8 = -(-n_sel // 8) * 8
    picked = _select_blocks(sc[:n_sel8], lax.broadcasted_iota(jnp.int32, (n_sel8, LANES), 0))
    selbias = jnp.concatenate([picked, jnp.full((n_selp - n_sel8, LANES), NEG, F32)], axis=0).T[:nh * t]
    qr2 = jnp.concatenate(qrhs, axis=0)
    rows2 = nh * g4 * t

    def by_rows(x):
        return jnp.concatenate([x[h * t:(h + 1) * t] for h in range(nh) for _ in range(g4)], axis=0)

    s_refs = (s0_ref, s1_ref)
    pgs = 4
    m_hist = [jnp.full((rows2, 1), NEG, F32)]
    accs = [jnp.zeros((rows2, LANES + SUM_ROWS), F32)]

    def pages(c, k0, second):
        return jnp.concatenate([sring[c % 3, k * half + KV * second:k * half + KV * (second + 1), :]
                                for k in range(k0, k0 + pgs)], axis=1).astype(BF16)

    def score(c):
        b0 = c * blocks_per_chunk
        tile = selbias[:, LANES * (b0 // LANES):LANES * (b0 // LANES + 1)].astype(BF16)
        bias = by_rows(jnp.dot(tile, exp_ref[(b0 % LANES) // blocks_per_chunk], preferred_element_type=F32))
        m_run = m_hist[-1]
        for k0 in range(0, ppc, pgs):
            cols = slice(k0 * page, (k0 + pgs) * page)
            s = jnp.dot(qr2, pages(c, k0, 0), preferred_element_type=F32) + bias[:, cols]
            s_refs[c % 2][:, cols] = s
            m_run = jnp.maximum(m_run, jnp.max(s, axis=-1, keepdims=True))
            yield
        m_hist.append(m_run)

    def accumulate(c):
        m_new = m_hist[c + 1]
        acc = jnp.exp2(m_hist[c] - m_new) * accs[-1]
        for k0 in range(0, ppc, pgs):
            cols = slice(k0 * page, (k0 + pgs) * page)
            p = jnp.exp2(s_refs[c % 2][:, cols] - m_new).astype(BF16)
            acc = acc + _nt(p, _with_ones(pages(c, k0, 1)))
            yield
        accs.append(acc)

    for cp in sel_copies(0):
        cp.wait()
    for _ in score(0):
        pass
    for c in range(n_chunks):
        if c + 2 < n_chunks:
            for cp in sel_copies(c + 2):
                cp.start()
        nxt = iter(())
        if c + 1 < n_chunks:
            for cp in sel_copies(c + 1):
                cp.wait()
            nxt = score(c + 1)
        for _ in accumulate(c):
            next(nxt, None)
        for _ in nxt:
            pass
    m, acc = m_hist[-1], accs[-1]

    nb_blk = past // SEL_BLOCK
    new_ok2 = jnp.concatenate([new_ok] * nh, axis=0)
    s = jnp.where(new_ok2, _nt(qr2, ksn) + by_rows(selbias[:, nb_blk:nb_blk + 1]), NEG)
    m_new = jnp.maximum(m, jnp.max(s, axis=-1, keepdims=True))
    alpha = jnp.exp2(m - m_new)
    p = jnp.exp2(s - m_new)
    l = alpha * acc[:, LANES:LANES + 1] + jnp.sum(p, axis=-1, keepdims=True)
    o_sel2 = (alpha * acc[:, :LANES] + jnp.dot(p.astype(BF16), vsn, preferred_element_type=F32)) / jnp.maximum(l, 1e-30)

    for h in range(NSA_KV_HEADS):
        qrh, o_cmp = qrhs[h], o_cmps[h]
        o_sel = o_sel2[h * g4 * t:(h + 1) * g4 * t]

        s = jnp.concatenate([jnp.dot(qrh, kwst, preferred_element_type=F32), _nt(qrh, kwn)], axis=1)
        widx = lax.broadcasted_iota(jnp.int32, s.shape, 1)
        kpos = past - wbuf + widx
        valid = (kpos <= qpos) & (kpos >= qpos - WINDOW) & (widx < wbuf + t)
        p = _softmax_rows(s, valid).astype(BF16)
        o_win = _nt(p[:, :wbuf], vwst) + jnp.dot(p[:, wbuf:], vwn, preferred_element_type=F32)

        for g in range(g4):
            hd = g4 * h + g
            r = 3 * hd
            rs = slice(g * t, (g + 1) * t)
            o = ga[:, r:r + 1] * o_cmp[rs] + ga[:, r + 1:r + 2] * o_sel[rs] + ga[:, r + 2:r + 3] * o_win[rs]
            out_tiles[hd // 2][hd % 2] = to_half(o, h, hd % 2)

    for k, (lo, hi) in enumerate(out_tiles):
        o_ref[0, :, LANES * k:LANES * (k + 1)] = jnp.where(lane < HEAD_DIM, lo, hi).astype(o_ref.dtype)


def _nsa_sample(page_table, q3, qr3, ga3, cache, new3, win_state, wnew3, cw, page, ppc):
    db, t, _ = q3.shape
    n_pages = page_table.shape[1]
    past = n_pages * page
    wbuf = win_state.shape[2]
    w1k, w1v, ck, cv, w2k, w2v, _ = cw
    n_sub = past // CMP_STRIDE
    n_sel = -(-(past + t) // SEL_BLOCK)
    n_selp = -(-n_sel // LANES) * LANES
    tks = ppc * page
    blocks_per_chunk = tks // SEL_BLOCK
    assert t == 8 and (past + t) // CMP_STRIDE == n_sub and n_pages % ppc == 0 and past % SEL_BLOCK == 0
    assert LANES % blocks_per_chunk == 0 and wbuf == WINDOW and t <= SEL_BLOCK and page == LANES
    assert (n_pages // ppc) % 2 == 0 and ppc % 4 == 0
    impt = jnp.pad(_importance_matrix(n_sel, n_sub).T, ((0, 0), (0, n_selp - n_sel)))
    m = jnp.arange(LANES // blocks_per_chunk)[:, None, None]
    j = jnp.arange(LANES)[None, :, None]
    u = jnp.arange(tks)[None, None, :]
    expand = (j == blocks_per_chunk * m + u // SEL_BLOCK).astype(BF16)
    per_b = lambda r, w: pl.BlockSpec((1, r, w), lambda b, pt: (b, 0, 0))
    const = lambda a: pl.BlockSpec(a.shape, lambda b, pt: (0,) * a.ndim)
    grid_spec = pltpu.PrefetchScalarGridSpec(
        num_scalar_prefetch=1,
        grid=(db,),
        in_specs=[per_b(t, WA), per_b(t, WA), per_b(t, LANES), pl.BlockSpec(memory_space=pl.ANY),
                  per_b(t, 4 * KV), per_b(2 * KV, wbuf), per_b(t, 2 * KV),
                  const(w1k), const(w1v), const(ck), const(cv), const(w2k), const(w2v), const(impt),
                  const(expand)],
        out_specs=per_b(t, WA),
        scratch_shapes=[pltpu.VMEM((2, ppc * 2 * KV, LANES), F32), pltpu.VMEM((3, ppc * 2 * KV, LANES), F32),
                        pltpu.VMEM(((n_sub + 1) * TOK_PITCH, KV), F32),
                        pltpu.VMEM(((n_sub + 1) * TOK_PITCH, KV), F32),
                        pltpu.VMEM((n_sub, 2 * KV), F32), pltpu.VMEM((n_sub, 2 * KV), F32),
                        pltpu.VMEM((NSA_HEADS * t, tks), F32), pltpu.VMEM((NSA_HEADS * t, tks), F32),
                        pltpu.SemaphoreType.DMA((2,)), pltpu.SemaphoreType.DMA((3,))],
    )
    return pl.pallas_call(
        functools.partial(_nsa_sample_kernel, page=page, n_pages=n_pages, ppc=ppc, past=past, n_sel=n_sel),
        grid_spec=grid_spec,
        out_shape=jax.ShapeDtypeStruct((db, t, WA), BF16),
        compiler_params=_cparams(("arbitrary",)),
        name="nsa_sample",
    )(page_table.reshape(-1), q3, qr3, ga3, cache, new3, win_state, wnew3, w1k, w1v, ck, cv, w2k, w2v, impt,
      expand)


def _combine_kernel(dest_ref, x_ref, w_ref, g_ref, ys_hbm, o_ref, buf, sem, *, tok0):
    i = pl.program_id(0)
    n = pl.num_programs(0)
    tm = x_ref.shape[0]

    def copy(tile, slot, r, k):
        src = dest_ref[(tok0 + tile * tm + r) * EXPERT_TOP_K + k]
        return pltpu.make_async_copy(ys_hbm.at[pl.ds(src, 1), :], buf.at[slot, k, pl.ds(r, 1), :], sem.at[slot])

    def start_tile(tile, slot):
        def body(r, c):
            for k in range(EXPERT_TOP_K):
                copy(tile, slot, r, k).start()
            return c
        lax.fori_loop(0, tm, body, 0, unroll=4)

    slot = i % 2

    @pl.when(i == 0)
    def _():
        start_tile(0, 0)

    @pl.when(i + 1 < n)
    def _():
        start_tile(i + 1, 1 - slot)

    def wait(r, c):
        for k in range(EXPERT_TOP_K):
            copy(i, slot, r, k).wait()
        return c

    lax.fori_loop(0, tm, wait, 0, unroll=4)
    x = x_ref[...]
    for k in range(EXPERT_TOP_K):
        x = x + w_ref[:, k:k + 1] * buf[slot, k]
    o_ref[...] = x * lax.rsqrt(jnp.mean(x * x, axis=-1, keepdims=True) + EPS) * g_ref[...]


def _combine_norm(dest, x1, weights, ys, g, tok0, tm):
    n, d = x1.shape
    assert n % tm == 0
    grid_spec = pltpu.PrefetchScalarGridSpec(
        num_scalar_prefetch=1,
        grid=(n // tm,),
        in_specs=[pl.BlockSpec((tm, d), lambda i, dst: (i, 0)), pl.BlockSpec((tm, ROUTE_COLS), lambda i, dst: (i, 0)),
                  pl.BlockSpec((1, d), lambda i, dst: (0, 0)), pl.BlockSpec(memory_space=pl.ANY)],
        out_specs=pl.BlockSpec((tm, d), lambda i, dst: (i, 0)),
        scratch_shapes=[pltpu.VMEM((2, EXPERT_TOP_K, tm, d), F32), pltpu.SemaphoreType.DMA((2,))],
    )
    return pl.pallas_call(
        functools.partial(_combine_kernel, tok0=tok0),
        grid_spec=grid_spec,
        out_shape=jax.ShapeDtypeStruct((n, d), F32),
        compiler_params=_cparams(("arbitrary",)),
        name="moe_combine_norm",
    )(dest, x1, weights, g.reshape(1, d), ys)


MOE_ROWS = 256


def kernel(x_prompt, x_sample, cache_nsa_kv, cache_diff_k, cache_diff_v, state_nsa_win_kv, page_table,
           norm_mix_g, w_in, nsa_cmp_pos, nsa_cmp_k_w1, nsa_cmp_k_w2, nsa_cmp_v_w1, nsa_cmp_v_w2,
           diff_lambda, diff_subln_g, w_proj_a, w_proj_b, w_out, norm_ffn_g,
           router_group_w, router_group_b, router_expert_w, router_expert_b,
           expert_w_gate, expert_w_up, expert_w_down, norm_final_g):
    depth = w_in.shape[0]
    bsz, seq, d = x_prompt.shape
    db, t, _ = x_sample.shape
    n_pool, page = cache_nsa_kv.shape[1:3]
    past = page_table.shape[1] * page
    wbuf = state_nsa_win_kv.shape[2]
    assert depth == 1 and bsz == 1
    l = 0
    lam_init = 0.8 - 0.6 * math.exp(-0.3 * l)
    w = _split_w_in(w_in[l], d)
    cw = _compress_weights(nsa_cmp_pos[l], nsa_cmp_k_w1[l], nsa_cmp_k_w2[l], nsa_cmp_v_w1[l], nsa_cmp_v_w2[l])
    wa, wb, wo = w_proj_a[l].astype(BF16), w_proj_b[l].astype(BF16), w_out[l].astype(BF16)
    rw = jnp.pad(jnp.concatenate([router_group_w[l], router_expert_w[l]], axis=1),
                 ((0, 0), (0, ROUTE_COLS - N_GROUPS - N_EXPERTS)))
    rb = jnp.pad(jnp.concatenate([router_group_b[l], router_expert_b[l]]),
                 (0, ROUTE_COLS - N_GROUPS - N_EXPERTS)).reshape(1, ROUTE_COLS)
    rwh = rw.astype(BF16)
    rwl = (rw - rwh.astype(F32)).astype(BF16)

    xp = x_prompt.reshape(seq, d)
    (nsa_p, win_p, dk_p, dv_p, gm_p, ks_b, kw_b, kb_b,
     qat, qart, qbt, vst, vwt, vbt, gat) = _project(xp, jnp.arange(seq), norm_mix_g[l], w, 512, True)
    kcmp, _, vcmpt = _compress(nsa_p, cw, min(256, seq // CMP_STRIDE))
    oat = _nsa_prompt(qat, qart, gat, kcmp, vcmpt, ks_b, vst, kw_b, vwt, 128, 512)
    obt = _diff_prompt(qbt, kb_b, vbt, diff_lambda[l], diff_subln_g[l], lam_init, 512, 512)
    x1p, h2p, re_p, rw_p = _merge(oat, obt, gm_p, xp, wa, wb, wo, norm_ffn_g[l], rwh, rwl, rb, 512, True)

    ns = db * t
    xs = x_sample.reshape(ns, d)
    pos_s = past + jnp.arange(ns) % t
    (nsa_s, win_s, dk_s, dv_s, gm_s, _, _, _, qa_s, qar_s, qb_s, ga_s) = _project(
        xs, pos_s, norm_mix_g[l], w, ns, False)
    r3 = lambda a: a.reshape(db, t, a.shape[-1])
    slabs = lambda c: c.transpose(0, 2, 3, 4, 1).reshape(-1, page)
    state_t = state_nsa_win_kv[l].transpose(0, 2, 3, 4, 1).reshape(db, 2 * KV, wbuf)
    oa_s = _nsa_sample(page_table, r3(qa_s), r3(qar_s), r3(ga_s), slabs(cache_nsa_kv[l]),
                       r3(nsa_s), state_t, r3(win_s), cw, page, 16)
    ob_s = _diff_sample(page_table, r3(qb_s), slabs(cache_diff_k[l]),
                        cache_diff_v[l].reshape(-1, DIFF_VDIM), r3(dk_s), r3(dv_s),
                        diff_lambda[l], diff_subln_g[l], lam_init, page, 16)
    x1s, h2s, re_s, rw_s = _merge(oa_s.reshape(ns, -1), ob_s.reshape(ns, -1), gm_s, xs, wa, wb, wo,
                                  norm_ffn_g[l], rwh, rwl, rb, ns, False)

    n_all = seq + ns
    plan_tile = max(tm for tm in range(8, 1025, 8) if n_all % tm == 0)
    dest, block_e, n_blocks = _route_plan(jnp.concatenate([re_p, re_s], axis=0), MOE_ROWS, plan_tile)
    xs = jnp.zeros((n_blocks * MOE_ROWS, d), F32)
    xs = _dispatch_rows(dest, h2p, xs, 0, 256)
    xs = _dispatch_rows(dest, h2s, xs, seq, ns)
    ys = _expert_ffn(block_e, xs, expert_w_gate[l], expert_w_up[l], expert_w_down[l], MOE_ROWS)
    y_prompt = _combine_norm(dest, x1p, rw_p, ys, norm_final_g, 0, 256)
    y_sample = _combine_norm(dest, x1s, rw_s, ys, norm_final_g, seq, ns)

    wn = min(WINDOW, seq)
    win_all_t = jnp.concatenate([state_t, r3(win_s).transpose(0, 2, 1)], axis=2)[:, :, -min(WINDOW, past + t):]
    win_all = win_all_t.reshape(db, 2, NSA_KV_HEADS, HEAD_DIM, -1).transpose(0, 4, 1, 2, 3)
    kvs = (4, NSA_KV_HEADS, HEAD_DIM)
    dks = (DIFF_HEADS, 2, HEAD_DIM)
    dvs = (DIFF_HEADS, DIFF_VDIM)
    return (y_prompt.reshape(1, seq, d), y_sample.reshape(db, t, d),
            nsa_p.reshape((1, 1, seq) + kvs), nsa_s.reshape((1, db, t) + kvs),
            dk_p.reshape((1, 1, seq) + dks), dk_s.reshape((1, db, t) + dks),
            dv_p.reshape((1, 1, seq) + dvs), dv_s.reshape((1, db, t) + dvs),
            win_p[seq - wn:].reshape(1, 1, wn, 2, NSA_KV_HEADS, HEAD_DIM),
            win_all[None])
```

```python
import functools
import math

import jax
import jax.numpy as jnp
from jax import lax
from jax.experimental import pallas as pl
from jax.experimental.pallas import tpu as pltpu

F32 = jnp.float32
BF16 = jnp.bfloat16

HEAD_DIM = 64
HALF = HEAD_DIM // 2
NSA_HEADS = 8
NSA_KV_HEADS = 2
NSA_GROUP = NSA_HEADS // NSA_KV_HEADS
CMP_STRIDE = 16
CMP_BLOCK = 2 * CMP_STRIDE
SEL_BLOCK = 64
SEL_PER_CMP = SEL_BLOCK // CMP_STRIDE
TOP_N = 16
WINDOW = 512
FORCED_SCORE = 1e4
DIFF_HEADS = 4
DIFF_VDIM = 2 * HEAD_DIM
N_GROUPS = 4
EXPERTS_PER_GROUP = 8
N_EXPERTS = N_GROUPS * EXPERTS_PER_GROUP
EXPERT_TOP_K = 2
ROPE_THETA = 10000.0
EPS = 1e-6
NEG = -1e30
SCALE = HEAD_DIM ** -0.5 * math.log2(math.e)

LANES = 128
VMEM_LIMIT = 56 * 1024 * 1024

WA = NSA_HEADS * HEAD_DIM
KV = NSA_KV_HEADS * HEAD_DIM
WB = DIFF_HEADS * 2 * HEAD_DIM
WV = DIFF_HEADS * DIFF_VDIM
N_GATE = 3 * NSA_HEADS


def _cparams(sem, flags=None):
    return pltpu.CompilerParams(dimension_semantics=sem, vmem_limit_bytes=VMEM_LIMIT, flags=flags)


def _full(shape):
    return pl.BlockSpec(shape, lambda *_: (0,) * len(shape))


def _swap_halves(t):
    lane = lax.broadcasted_iota(jnp.int32, t.shape, 1)
    fwd = pltpu.roll(t, LANES - HALF, axis=1)
    bwd = pltpu.roll(t, HALF, axis=1)
    return jnp.where(lane % HEAD_DIM < HALF, fwd, bwd)


def _rope_rows(t, cos, sin):
    outs = []
    for a in range(0, t.shape[1], LANES):
        x = t[:, a:a + LANES]
        outs.append(x * cos + _swap_halves(x) * sin)
    return outs[0] if len(outs) == 1 else jnp.concatenate(outs, axis=1)


def _rope_cols(t, cos, sin):
    outs = []
    for a in range(0, t.shape[0], HEAD_DIM):
        x1 = t[a:a + HALF]
        x2 = t[a + HALF:a + HEAD_DIM]
        outs.append(x1 * cos - x2 * sin)
        outs.append(x2 * cos + x1 * sin)
    return jnp.concatenate(outs, axis=0)


def _proj_kernel(x_ref, g_ref, wn_ref, wt_ref, cosn_ref, sinn_ref, cost_ref, sint_ref,
                 nsa_ref, win_ref, dk_ref, dv_ref, gm_ref, ksb_ref, kwb_ref, kbb_ref, *rest, transposed_q):
    x = x_ref[...]
    h = x * lax.rsqrt(jnp.mean(x * x, axis=-1, keepdims=True) + EPS) * g_ref[...]
    hb = h.astype(BF16)
    cosn = cosn_ref[...]
    sinn = sinn_ref[...]

    def mm(a, b):
        return jnp.dot(hb, wn_ref[:, a:b], preferred_element_type=F32)

    c = 0
    y = mm(c, c + 4 * KV)
    ks = _rope_rows(y[:, 2 * KV:3 * KV], cosn, sinn)
    nsa_ref[:, :2 * KV] = y[:, :2 * KV]
    nsa_ref[:, 2 * KV:3 * KV] = ks
    nsa_ref[:, 3 * KV:] = y[:, 3 * KV:]
    ksb_ref[...] = ks.astype(BF16)
    c += 4 * KV
    y = mm(c, c + 2 * KV)
    kw = _rope_rows(y[:, :KV], cosn, sinn)
    win_ref[:, :KV] = kw
    win_ref[:, KV:] = y[:, KV:]
    kwb_ref[...] = kw.astype(BF16)
    c += 2 * KV
    kb = _rope_rows(mm(c, c + WB), cosn, sinn)
    dk_ref[...] = kb
    kbb_ref[...] = kb.astype(BF16)
    c += WB
    dv_ref[...] = mm(c, c + WV)
    c += WV
    d_model = x.shape[1]
    gm_ref[...] = jax.nn.sigmoid(mm(c, c + 2 * d_model))
    c += 2 * d_model

    if transposed_q:
        qat_ref, qart_ref, qbt_ref, vst_ref, vwt_ref, vbt_ref, gat_ref = rest
        cost = cost_ref[...]
        sint = sint_ref[...]

        def mmt(a, b):
            return lax.dot_general(wt_ref[a:b, :], hb, (((1,), (1,)), ((), ())), preferred_element_type=F32)

        r = 0
        qa = mmt(r, r + WA) * SCALE
        qat_ref[...] = qa.astype(BF16)
        qart_ref[...] = _rope_cols(qa, cost, sint).astype(BF16)
        r += WA
        qbt_ref[...] = _rope_cols(mmt(r, r + WB) * SCALE, cost, sint).astype(BF16)
        r += WB
        vst_ref[...] = mmt(r, r + KV).astype(BF16)
        r += KV
        vwt_ref[...] = mmt(r, r + KV).astype(BF16)
        r += KV
        vbt_ref[...] = mmt(r, r + WV).astype(BF16)
        r += WV
        gat_ref[...] = jax.nn.sigmoid(mmt(r, r + 32))
    else:
        qa_ref, qar_ref, qb_ref, ga_ref = rest
        qa = mm(c, c + WA) * SCALE
        qa_ref[...] = qa.astype(BF16)
        qar_ref[...] = _rope_rows(qa, cosn, sinn).astype(BF16)
        c += WA
        qb_ref[...] = _rope_rows(mm(c, c + WB) * SCALE, cosn, sinn).astype(BF16)
        c += WB
        ga_ref[...] = jax.nn.sigmoid(mm(c, c + LANES))


def _split_w_in(w_in, d_model):
    sizes = [WA, KV, KV, KV, KV, KV, KV, N_GATE, WB, WB, WV, 2 * d_model]
    offs = [0]
    for s in sizes:
        offs.append(offs[-1] + s)
    names = ["qa", "kc", "vc", "ks", "vs", "kw", "vw", "ga", "qb", "kb", "vb", "gm"]
    return {n: w_in[:, offs[i]:offs[i + 1]] for i, n in enumerate(names)}


def _rope_tables(pos):
    inv = ROPE_THETA ** (-jnp.arange(HALF, dtype=F32) / HALF)
    ang = pos.astype(F32)[:, None] * inv[None, :]
    cos, sin = jnp.cos(ang), jnp.sin(ang)
    cosn = jnp.tile(cos, (1, LANES // HALF))
    sinn = jnp.tile(jnp.concatenate([-sin, sin], axis=1), (1, LANES // HEAD_DIM))
    return cosn, sinn, cos.T, sin.T


def _project(x, pos, norm_g, w, tm, transposed_q):
    n, d = x.shape
    assert n % tm == 0
    cosn, sinn, cost, sint = _rope_tables(pos)
    wn_parts = [w["kc"], w["vc"], w["ks"], w["vs"], w["kw"], w["vw"], w["kb"], w["vb"], w["gm"]]
    ga_pad = jnp.pad(w["ga"], ((0, 0), (0, LANES - N_GATE)))
    if transposed_q:
        wt = jnp.concatenate([w["qa"], w["qb"], w["vs"], w["vw"], w["vb"], ga_pad[:, :32]], axis=1).T.astype(BF16)
    else:
        wn_parts += [w["qa"], w["qb"], ga_pad]
        wt = jnp.zeros((8, d), BF16)
    wn = jnp.concatenate(wn_parts, axis=1).astype(BF16)

    row = lambda c: pl.BlockSpec((tm, c), lambda i: (i, 0))
    col = lambda r: pl.BlockSpec((r, tm), lambda i: (0, i))
    out_shape = [jax.ShapeDtypeStruct((n, 4 * KV), F32), jax.ShapeDtypeStruct((n, 2 * KV), F32),
                 jax.ShapeDtypeStruct((n, WB), F32), jax.ShapeDtypeStruct((n, WV), F32),
                 jax.ShapeDtypeStruct((n, 2 * d), F32), jax.ShapeDtypeStruct((n, KV), BF16),
                 jax.ShapeDtypeStruct((n, KV), BF16), jax.ShapeDtypeStruct((n, WB), BF16)]
    out_specs = [row(4 * KV), row(2 * KV), row(WB), row(WV), row(2 * d), row(KV), row(KV), row(WB)]
    if transposed_q:
        out_shape += [jax.ShapeDtypeStruct((WA, n), BF16), jax.ShapeDtypeStruct((WA, n), BF16),
                      jax.ShapeDtypeStruct((WB, n), BF16), jax.ShapeDtypeStruct((KV, n), BF16),
                      jax.ShapeDtypeStruct((KV, n), BF16), jax.ShapeDtypeStruct((WV, n), BF16),
                      jax.ShapeDtypeStruct((32, n), F32)]
        out_specs += [col(WA), col(WA), col(WB), col(KV), col(KV), col(WV), col(32)]
    else:
        out_shape += [jax.ShapeDtypeStruct((n, WA), BF16), jax.ShapeDtypeStruct((n, WA), BF16),
                      jax.ShapeDtypeStruct((n, WB), BF16), jax.ShapeDtypeStruct((n, LANES), F32)]
        out_specs += [row(WA), row(WA), row(WB), row(LANES)]
    return pl.pallas_call(
        functools.partial(_proj_kernel, transposed_q=transposed_q),
        grid=(n // tm,),
        in_specs=[row(d), _full((1, d)), _full(wn.shape), _full(wt.shape),
                  row(LANES), row(LANES), col(HALF), col(HALF)],
        out_specs=out_specs,
        out_shape=out_shape,
        compiler_params=_cparams(("parallel",)),
        name="proj",
    )(x, norm_g.reshape(1, d), wn, wt, cosn, sinn, cost, sint)


def _pad_head(qt, slot):
    z = jnp.zeros_like(qt)
    return jnp.concatenate([qt, z] if slot == 0 else [z, qt], axis=0)


SUM_ROWS = 16


def _with_ones(vt):
    return jnp.concatenate([vt, jnp.ones((SUM_ROWS, vt.shape[1]), vt.dtype)], axis=0)


def _flash_step(st, vt1, m_prev, acc_prev):
    m_new = jnp.maximum(m_prev, jnp.max(st, axis=0, keepdims=True))
    p = jnp.exp2(st - m_new).astype(BF16)
    return m_new, jnp.exp2(m_prev - m_new) * acc_prev + jnp.dot(vt1, p, preferred_element_type=F32)


def _diff_lambda(lam_ref, lam_init):
    lv = lam_ref[...]
    a = jnp.sum(lv[0:1] * lv[1:2], axis=-1, keepdims=True)
    b = jnp.sum(lv[2:3] * lv[3:4], axis=-1, keepdims=True)
    return jnp.exp(a) - jnp.exp(b) + lam_init


DIFF_AHEAD = 3


def _diff_prompt_kernel(it_ref, jt_ref, qt_ref, k_ref, vt_ref, lam_ref, g_ref, o_ref, m_ref, acc_ref, *s_refs,
                        tq, tk, sub, lam_init):
    i = it_ref[pl.program_id(0)]
    j = jt_ref[pl.program_id(0)]
    first_diag = i * tq // tk

    @pl.when(j == 0)
    def _():
        m_ref[...] = jnp.full_like(m_ref, NEG)
        acc_ref[...] = jnp.zeros_like(acc_ref)

    n_maps = 2 * DIFF_HEADS
    subs = [slice(r, r + sub) for r in range(0, tk, sub)]

    def step(causal):
        def score(hc, out):
            h, c = divmod(hc, 2)
            qt = _pad_head(qt_ref[HEAD_DIM * hc:HEAD_DIM * (hc + 1), :], c)
            m_new = m_ref[hc:hc + 1]
            for rows in subs:
                st = jnp.dot(k_ref[rows, 2 * HEAD_DIM * h:2 * HEAD_DIM * (h + 1)], qt, preferred_element_type=F32)
                if causal:
                    kpos = j * tk + rows.start + lax.broadcasted_iota(jnp.int32, (sub, tq), 0)
                    qpos = i * tq + lax.broadcasted_iota(jnp.int32, (sub, tq), 1)
                    st = jnp.where(kpos <= qpos, st, NEG)
                s_refs[hc % len(s_refs)][rows, :] = st
                m_new = jnp.maximum(m_new, jnp.max(st, axis=0, keepdims=True))
                yield
            out.append(m_new)

        def accumulate(hc, m_new):
            h = hc // 2
            acc = jnp.exp2(m_ref[hc:hc + 1] - m_new) * acc_ref[hc]
            for rows in subs:
                p = jnp.exp2(s_refs[hc % len(s_refs)][rows, :] - m_new).astype(BF16)
                vt1 = _with_ones(vt_ref[DIFF_VDIM * h:DIFF_VDIM * (h + 1), rows])
                acc = acc + jnp.dot(vt1, p, preferred_element_type=F32)
                yield
            m_ref[hc:hc + 1] = m_new
            acc_ref[hc] = acc

        m_new = []
        for hc in range(DIFF_AHEAD):
            for _ in score(hc, m_new):
                pass
        for hc in range(n_maps):
            nxt = score(hc + DIFF_AHEAD, m_new) if hc + DIFF_AHEAD < n_maps else iter(())
            for _ in accumulate(hc, m_new[hc]):
                next(nxt, None)
            for _ in nxt:
                pass

    pl.when(j < first_diag)(functools.partial(step, False))
    pl.when(j >= first_diag)(functools.partial(step, True))

    @pl.when(j == ((i + 1) * tq - 1) // tk)
    def _():
        lam = _diff_lambda(lam_ref, lam_init)
        for h in range(DIFF_HEADS):
            a0, a1 = acc_ref[2 * h], acc_ref[2 * h + 1]
            o0 = a0[:DIFF_VDIM] / a0[DIFF_VDIM:DIFF_VDIM + 1]
            o1 = a1[:DIFF_VDIM] / a1[DIFF_VDIM:DIFF_VDIM + 1]
            a = o0 - lam * o1
            y = a * lax.rsqrt(jnp.mean(a * a, axis=0, keepdims=True) + EPS) * g_ref[...] * (1.0 - lam_init)
            o_ref[DIFF_VDIM * h:DIFF_VDIM * (h + 1), :] = y.astype(o_ref.dtype)


def _diff_prompt(qbt, kb, vbt, diff_lambda, subln_g, lam_init, tq, tk):
    n = kb.shape[0]
    assert n % tq == 0 and n % tk == 0 and tq % tk == 0
    pairs = [(i, j) for i in range(n // tq) for j in range(((i + 1) * tq - 1) // tk + 1)]
    it = jnp.asarray([p[0] for p in pairs], jnp.int32)
    jt = jnp.asarray([p[1] for p in pairs], jnp.int32)
    const = lambda shape: pl.BlockSpec(shape, lambda s, it, jt: (0,) * len(shape))
    grid_spec = pltpu.PrefetchScalarGridSpec(
        num_scalar_prefetch=2,
        grid=(len(pairs),),
        in_specs=[pl.BlockSpec((WB, tq), lambda s, it, jt: (0, it[s])),
                  pl.BlockSpec((tk, WB), lambda s, it, jt: (jt[s], 0)),
                  pl.BlockSpec((WV, tk), lambda s, it, jt: (0, jt[s])),
                  const(diff_lambda.shape), const((DIFF_VDIM, 1))],
        out_specs=pl.BlockSpec((WV, tq), lambda s, it, jt: (0, it[s])),
        scratch_shapes=[pltpu.VMEM((2 * DIFF_HEADS, tq), F32),
                        pltpu.VMEM((2 * DIFF_HEADS, DIFF_VDIM + SUM_ROWS, tq), F32),
                        ] + [pltpu.VMEM((tk, tq), F32)] * (DIFF_AHEAD + 1),
    )
    return pl.pallas_call(
        functools.partial(_diff_prompt_kernel, tq=tq, tk=tk, sub=min(tk, 2 * LANES), lam_init=lam_init),
        grid_spec=grid_spec,
        out_shape=jax.ShapeDtypeStruct((WV, n), BF16),
        compiler_params=_cparams(("arbitrary",)),
        name="diff_prompt",
    )(it, jt, qbt, kb, vbt, diff_lambda, subln_g.reshape(DIFF_VDIM, 1))


def _compress_weights(pos, k_w1, k_w2, v_w1, v_w2):
    hd = HEAD_DIM
    z = jnp.zeros((CMP_STRIDE, hd, hd), F32)

    def halves(w1):
        w3 = w1.reshape(CMP_BLOCK, hd, -1)
        return w3[:CMP_STRIDE], w3[CMP_STRIDE:]

    def expand(top, bot):
        rows = [[top, z, bot, z], [z, top, z, bot]]
        w = jnp.concatenate([jnp.concatenate(r, axis=2) for r in rows], axis=1)
        return w.reshape(CMP_STRIDE // 2, 2 * KV, 2 * KV).astype(BF16)

    pf = pos.reshape(1, -1)
    ck, cv = pf @ k_w1, pf @ v_w1
    z2 = jnp.zeros((hd, hd), F32)
    w2k = jnp.block([[k_w2, z2], [z2, k_w2]]).astype(BF16)
    w2v = jnp.block([[v_w2, z2], [z2, v_w2]]).astype(BF16)
    return (expand(*halves(k_w1)), expand(*halves(v_w1)), jnp.concatenate([ck, ck], axis=1),
            jnp.concatenate([cv, cv], axis=1), w2k, w2v, w2v.T)


def _compress_ab(x_ref, w_ref, n_sub, pitch=CMP_STRIDE):
    acc = jnp.zeros((n_sub, 2 * KV), F32)
    for r in range(0, CMP_STRIDE, 2):
        xr = jnp.concatenate([x_ref[pl.ds(r + d, n_sub, stride=pitch), :] for d in range(2)], axis=1)
        acc += jnp.dot(xr.astype(BF16), w_ref[r // 2], preferred_element_type=F32)
    return acc


def _compress_hidden(ab, c):
    n_sub = ab.shape[0]
    nxt = pltpu.roll(ab[:, KV:], n_sub - 1, axis=0)
    return jax.nn.gelu(ab[:, :KV] + nxt + c).astype(BF16)


def _compress_ab_kernel(xk_ref, xv_ref, wk_ref, wv_ref, abk_ref, abv_ref, *, n_sub):
    abk_ref[...] = _compress_ab(xk_ref, wk_ref, n_sub)
    abv_ref[...] = _compress_ab(xv_ref, wv_ref, n_sub)


def _compress_mlp_kernel(abk_ref, abv_ref, ck_ref, cv_ref, w2k_ref, w2v_ref, w2vt_ref, kc_ref, vc_ref, vct_ref):
    gk = _compress_hidden(abk_ref[...], ck_ref[...])
    gv = _compress_hidden(abv_ref[...], cv_ref[...])
    kc_ref[...] = jnp.dot(gk, w2k_ref[...], preferred_element_type=F32).astype(BF16)
    vc_ref[...] = jnp.dot(gv, w2v_ref[...], preferred_element_type=F32).astype(BF16)
    vct_ref[...] = lax.dot_general(w2vt_ref[...], gv, (((1,), (1,)), ((), ())),
                                   preferred_element_type=F32).astype(BF16)


def _compress(kv, cw, sub_tile):
    w1k, w1v, ck, cv, w2k, w2v, w2vt = cw
    n_sub = kv.shape[0] // CMP_STRIDE
    assert n_sub % sub_tile == 0
    ab_shape = jax.ShapeDtypeStruct((n_sub, 2 * KV), F32)
    ab_spec = pl.BlockSpec((sub_tile, 2 * KV), lambda i: (i, 0))
    abk, abv = pl.pallas_call(
        functools.partial(_compress_ab_kernel, n_sub=sub_tile),
        grid=(n_sub // sub_tile,),
        in_specs=[pl.BlockSpec((sub_tile * CMP_STRIDE, KV), lambda i: (i, 0)),
                  pl.BlockSpec((sub_tile * CMP_STRIDE, KV), lambda i: (i, 1)), _full(w1k.shape), _full(w1v.shape)],
        out_specs=[ab_spec, ab_spec],
        out_shape=[ab_shape, ab_shape],
        compiler_params=_cparams(("parallel",)),
        name="compress_ab",
    )(kv, kv, w1k, w1v)
    return pl.pallas_call(
        _compress_mlp_kernel,
        out_shape=[jax.ShapeDtypeStruct((n_sub, KV), BF16), jax.ShapeDtypeStruct((n_sub, KV), BF16),
                   jax.ShapeDtypeStruct((KV, n_sub), BF16)],
        compiler_params=pltpu.CompilerParams(vmem_limit_bytes=VMEM_LIMIT),
        name="compress_mlp",
    )(abk, abv, ck, cv, w2k, w2v, w2vt)


def _importance_matrix(n_sel, n_cmp):
    j = jnp.arange(n_sel)[:, None]
    n = jnp.arange(n_cmp)[None, :]
    return ((n >= SEL_PER_CMP * j - 1) & (n <= SEL_PER_CMP * j + SEL_PER_CMP - 1)).astype(BF16)


def _split3(x):
    hi = x.astype(BF16)
    r = x - hi.astype(F32)
    mid = r.astype(BF16)
    lo = (r - mid.astype(F32)).astype(BF16)
    return hi, mid, lo


def _tile_lanes(x, k):
    return jnp.concatenate([x] * k, axis=1)


def _select_blocks(score, blk):
    big = jnp.int32(2 ** 30)

    def body(_, carry):
        sc, bias = carry
        mx = jnp.max(sc, axis=0, keepdims=True)
        first = jnp.min(jnp.where(sc == mx, blk, big), axis=0, keepdims=True)
        hit = blk == first
        return jnp.where(hit, -3e38, sc), jnp.where(hit, 0.0, bias)

    n_pick = min(TOP_N, score.shape[0])
    return lax.fori_loop(0, n_pick, body, (score, jnp.full(score.shape, NEG, F32)))[1]


def _softmax_cols(st, valid):
    sm = jnp.where(valid, st, NEG)
    m = jnp.max(sm, axis=0, keepdims=True)
    e = jnp.where(valid, jnp.exp2(sm - m), 0.0)
    return e / jnp.maximum(jnp.sum(e, axis=0, keepdims=True), 1e-30)


NSA_GROUP_CHUNKS = 4
NSA_AHEAD = 3


def _nsa_prompt_kernel(qat_ref, qart_ref, gat_ref, kc_ref, vct_ref, imp_ref, hot_ref, ks_ref, vst_ref, kw_ref,
                       vwt_ref, o_ref, bias_ref, ocmp_ref, m_ref, acc_ref, *s_refs, tq, tk, sub, n_cmp):
    i = pl.program_id(0)
    g4 = NSA_GROUP
    ncp = kc_ref.shape[0]
    n_sel = imp_ref.shape[0]
    q0 = i * tq
    qlane = q0 + lax.broadcasted_iota(jnp.int32, (1, tq), 1)

    def heads_t(ref, h):
        return jnp.concatenate([ref[HEAD_DIM * (g4 * h + g):HEAD_DIM * (g4 * h + g + 1), :] for g in range(g4)],
                               axis=1)

    for h in range(NSA_KV_HEADS):
        qt = _pad_head(heads_t(qat_ref, h), h)
        st = jnp.dot(kc_ref[...], qt, preferred_element_type=F32)
        nrow = lax.broadcasted_iota(jnp.int32, (ncp, tq), 0)
        valid = (nrow * CMP_STRIDE + (CMP_BLOCK - 1) <= qlane) & (nrow < n_cmp)
        p = _softmax_cols(st, _tile_lanes(valid, g4))
        ocmp_ref[h] = jnp.dot(vct_ref[HEAD_DIM * h:HEAD_DIM * (h + 1), :], p.astype(BF16),
                              preferred_element_type=F32)
        psum = p[:, :tq]
        for g in range(1, g4):
            psum = psum + p[:, g * tq:(g + 1) * tq]
        imp = jnp.zeros((n_sel, tq), F32)
        for part in _split3(psum):
            imp += jnp.dot(imp_ref[...], part, preferred_element_type=F32)
        blk = lax.broadcasted_iota(jnp.int32, (n_sel, tq), 0)
        cur = qlane // SEL_BLOCK
        forced = (blk == 0) | (blk == cur) | (blk == cur - 1)
        score = jnp.where(blk > cur, -1.0, jnp.where(forced, FORCED_SCORE, imp))
        bias_ref[h] = _select_blocks(score, blk)

    qrts = [_pad_head(heads_t(qart_ref, h), h) for h in range(NSA_KV_HEADS)]
    per_chunk = tk // SEL_BLOCK
    zpad = jnp.zeros((LANES - per_chunk, g4 * tq), F32)
    subs = [slice(r, r + sub) for r in range(0, tk, sub)]
    n_key_chunks = ks_ref.shape[0] // tk
    m_ref[...] = jnp.full_like(m_ref, NEG)
    acc_ref[...] = jnp.zeros_like(acc_ref)

    def group(c_base, masked, n_group):
        n_maps = n_group * NSA_KV_HEADS
        m_hist = [[m_ref[h:h + 1]] for h in range(NSA_KV_HEADS)]

        def where(k):
            cc, h = divmod(k, NSA_KV_HEADS)
            c = c_base + cc
            cl = jnp.minimum(c, n_key_chunks - 1) if masked else c
            return h, c, cl, pl.multiple_of(cl * tk, tk)

        def score(k):
            h, c, cl, k0 = where(k)
            b8 = bias_ref[h, pl.ds(pl.multiple_of(cl * per_chunk, per_chunk), per_chunk), :]
            qx = jnp.concatenate([qrts[h], jnp.concatenate([_tile_lanes(b8, g4), zpad], axis=0).astype(BF16)],
                                 axis=0)
            m_run = m_hist[h][-1]
            for rows in subs:
                kx = jnp.concatenate([ks_ref[pl.ds(k0 + rows.start, sub), :], hot_ref[rows, :]], axis=1)
                st = jnp.dot(kx, qx, preferred_element_type=F32)
                if masked:
                    kpos = c * tk + rows.start + lax.broadcasted_iota(jnp.int32, (sub, tq), 0)
                    st = jnp.where(_tile_lanes(kpos <= qlane, g4), st, NEG)
                s_refs[k % len(s_refs)][rows, :] = st
                m_run = jnp.maximum(m_run, jnp.max(st, axis=0, keepdims=True))
                yield
            m_hist[h].append(m_run)

        def accumulate(k):
            h, _, _, k0 = where(k)
            cc = k // NSA_KV_HEADS
            m_new = m_hist[h][cc + 1]
            acc = jnp.exp2(m_hist[h][cc] - m_new) * acc_ref[h]
            for rows in subs:
                p = jnp.exp2(s_refs[k % len(s_refs)][rows, :] - m_new).astype(BF16)
                vt1 = _with_ones(vst_ref[HEAD_DIM * h:HEAD_DIM * (h + 1), pl.ds(k0 + rows.start, sub)])
                acc = acc + jnp.dot(vt1, p, preferred_element_type=F32)
                yield
            acc_ref[h] = acc

        for k in range(min(NSA_AHEAD, n_maps)):
            for _ in score(k):
                pass
        for k in range(n_maps):
            nxt = score(k + NSA_AHEAD) if k + NSA_AHEAD < n_maps else iter(())
            for _ in accumulate(k):
                next(nxt, None)
            for _ in nxt:
                pass
        for h in range(NSA_KV_HEADS):
            m_ref[h:h + 1] = m_hist[h][-1]

    n_full = q0 // tk
    n_plain = n_full // NSA_GROUP_CHUNKS
    pair = NSA_GROUP_CHUNKS // 2
    rest = n_full - n_plain * NSA_GROUP_CHUNKS

    def plain(gi, carry):
        group(gi * NSA_GROUP_CHUNKS, False, NSA_GROUP_CHUNKS)
        return carry

    lax.fori_loop(0, n_plain, plain, 0)
    pl.when(rest >= pair)(lambda: group(n_plain * NSA_GROUP_CHUNKS, False, pair))
    group(n_plain * NSA_GROUP_CHUNKS + (rest // pair) * pair, True, pair)

    for h in range(NSA_KV_HEADS):
        qrt = qrts[h]
        o_cmp = ocmp_ref[h]
        acc = acc_ref[h]
        o_sel = acc[:HEAD_DIM] / jnp.maximum(acc[HEAD_DIM:HEAD_DIM + 1], 1e-30)

        nw = WINDOW + tq
        w0 = pl.multiple_of(jnp.maximum(q0 - WINDOW, 0), LANES)
        st = jnp.dot(kw_ref[pl.ds(w0, nw), :], qrt, preferred_element_type=F32)
        kpos = w0 + lax.broadcasted_iota(jnp.int32, (nw, tq), 0)
        valid = (kpos <= qlane) & (kpos >= qlane - WINDOW)
        p = _softmax_cols(st, _tile_lanes(valid, g4))
        o_win = jnp.dot(vwt_ref[HEAD_DIM * h:HEAD_DIM * (h + 1), pl.ds(w0, nw)], p.astype(BF16),
                        preferred_element_type=F32)

        for g in range(g4):
            r = 3 * (g4 * h + g)
            sl = slice(g * tq, (g + 1) * tq)
            o = (gat_ref[r:r + 1, :] * o_cmp[:, sl] + gat_ref[r + 1:r + 2, :] * o_sel[:, sl]
                 + gat_ref[r + 2:r + 3, :] * o_win[:, sl])
            o_ref[HEAD_DIM * (g4 * h + g):HEAD_DIM * (g4 * h + g + 1), :] = o.astype(o_ref.dtype)


def _nsa_prompt(qat, qart, gat, kc, vct, ks, vst, kw, vwt, tq, tk):
    n = ks.shape[0]
    n_sub = kc.shape[0]
    n_cmp = n_sub - 1
    n_sel = n // SEL_BLOCK
    assert n % tq == 0 and n % tk == 0 and tk % SEL_BLOCK == 0 and n >= WINDOW + tq and tq % LANES == 0
    assert tk % tq == 0 and tk // SEL_BLOCK <= LANES
    imp = _importance_matrix(n_sel, n_sub)
    hot = (jnp.arange(tk)[:, None] // SEL_BLOCK == jnp.arange(LANES)[None, :]).astype(BF16)
    col = lambda r: pl.BlockSpec((r, tq), lambda i: (0, i))
    return pl.pallas_call(
        functools.partial(_nsa_prompt_kernel, tq=tq, tk=tk, sub=min(tk, 2 * LANES), n_cmp=n_cmp),
        grid=(n // tq,),
        in_specs=[col(WA), col(WA), col(32), _full(kc.shape), _full(vct.shape), _full(imp.shape), _full(hot.shape),
                  _full(ks.shape), _full(vst.shape), _full(kw.shape), _full(vwt.shape)],
        out_specs=col(WA),
        out_shape=jax.ShapeDtypeStruct((WA, n), BF16),
        scratch_shapes=[pltpu.VMEM((NSA_KV_HEADS, n_sel, tq), F32),
                        pltpu.VMEM((NSA_KV_HEADS, HEAD_DIM, NSA_GROUP * tq), F32),
                        pltpu.VMEM((NSA_KV_HEADS, NSA_GROUP * tq), F32),
                        pltpu.VMEM((NSA_KV_HEADS, HEAD_DIM + SUM_ROWS, NSA_GROUP * tq), F32),
                        ] + [pltpu.VMEM((tk, NSA_GROUP * tq), F32)] * (NSA_AHEAD + 1),
        compiler_params=_cparams(("parallel",)),
        name="nsa_prompt",
    )(qat, qart, gat, kc, vct, imp, hot, ks, vst, kw, vwt)


ROUTE_COLS = LANES


def _first_lane_of_max(v, lane):
    mx = jnp.max(v, axis=-1, keepdims=True)
    return mx, jnp.min(jnp.where(v == mx, lane, ROUTE_COLS), axis=-1, keepdims=True)


def _merge_kernel(oa_ref, ob_ref, gm_ref, x_ref, wa_ref, wb_ref, wo_ref, g_ref, rwh_ref, rwl_ref, rb_ref,
                  x1_ref, h2_ref, re_ref, rw_ref, *, transposed):
    d = x_ref.shape[1]
    dims = (((0,), (0,)), ((), ())) if transposed else (((1,), (0,)), ((), ()))
    ya = lax.dot_general(oa_ref[...], wa_ref[...], dims, preferred_element_type=F32)
    yb = lax.dot_general(ob_ref[...], wb_ref[...], dims, preferred_element_type=F32)
    mix = gm_ref[:, :d] * ya + gm_ref[:, d:] * yb
    x1 = x_ref[...] + jnp.dot(mix.astype(BF16), wo_ref[...], preferred_element_type=F32)
    x1_ref[...] = x1
    h2 = x1 * lax.rsqrt(jnp.mean(x1 * x1, axis=-1, keepdims=True) + EPS) * g_ref[...]
    h2_ref[...] = h2

    hi = h2.astype(BF16)
    lo = (h2 - hi.astype(F32)).astype(BF16)
    logits = (jnp.dot(hi, rwh_ref[...], preferred_element_type=F32)
              + jnp.dot(lo, rwh_ref[...], preferred_element_type=F32)
              + jnp.dot(hi, rwl_ref[...], preferred_element_type=F32)) + rb_ref[...]
    lane = lax.broadcasted_iota(jnp.int32, logits.shape, 1)
    is_g = lane < N_GROUPS
    gl = jnp.where(is_g, logits, NEG)
    gmx, grp = _first_lane_of_max(gl, lane)
    p_grp = 1.0 / jnp.sum(jnp.where(is_g, jnp.exp(gl - gmx), 0.0), axis=-1, keepdims=True)
    e_id = lane - N_GROUPS
    in_grp = (e_id >= 0) & (e_id < N_EXPERTS) & (e_id // EXPERTS_PER_GROUP == grp)
    el = jnp.where(in_grp, logits, NEG)
    emx = jnp.max(el, axis=-1, keepdims=True)
    ee = jnp.where(in_grp, jnp.exp(el - emx), -1.0)
    e1, i1 = _first_lane_of_max(ee, lane)
    e2, i2 = _first_lane_of_max(jnp.where(lane == i1, -1.0, ee), lane)
    inv = p_grp / (e1 + e2)
    re_ref[...] = jnp.where(lane == 0, i1 - N_GROUPS, jnp.where(lane == 1, i2 - N_GROUPS, 0))
    rw_ref[...] = jnp.where(lane == 0, e1 * inv, jnp.where(lane == 1, e2 * inv, 0.0))


def _merge(oa, ob, gm, x, wa, wb, wo, ffn_g, rwh, rwl, rb, tm, transposed):
    n, d = x.shape
    assert n % tm == 0
    row = lambda c: pl.BlockSpec((tm, c), lambda i: (i, 0))
    o_spec = pl.BlockSpec((WA, tm), lambda i: (0, i)) if transposed else row(WA)
    return pl.pallas_call(
        functools.partial(_merge_kernel, transposed=transposed),
        grid=(n // tm,),
        in_specs=[o_spec, o_spec, row(2 * d), row(d), _full(wa.shape), _full(wb.shape), _full(wo.shape),
                  _full((1, d)), _full(rwh.shape), _full(rwl.shape), _full((1, ROUTE_COLS))],
        out_specs=[row(d), row(d), row(ROUTE_COLS), row(ROUTE_COLS)],
        out_shape=[jax.ShapeDtypeStruct((n, d), F32), jax.ShapeDtypeStruct((n, d), F32),
                   jax.ShapeDtypeStruct((n, ROUTE_COLS), jnp.int32), jax.ShapeDtypeStruct((n, ROUTE_COLS), F32)],
        compiler_params=_cparams(("parallel",)),
        name="merge_route",
    )(oa, ob, gm, x, wa, wb, wo, ffn_g.reshape(1, d), rwh, rwl, rb)


def _route_plan_kernel(re_ref, dest_ref, be_ref, cnt_ref, start_ref, carry_ref, tri_ref, *, bm, n_blocks):
    phase = pl.program_id(0)
    tile = pl.program_id(1)
    tm = re_ref.shape[0]
    lane = lax.broadcasted_iota(jnp.int32, (tm, ROUTE_COLS), 1)
    hot = [lane == re_ref[:, k:k + 1] for k in range(EXPERT_TOP_K)]
    both = sum(h.astype(F32) for h in hot)

    @pl.when((phase == 0) & (tile == 0))
    def _():
        cnt_ref[...] = jnp.zeros_like(cnt_ref)
        r = lax.broadcasted_iota(jnp.int32, (tm, tm), 0)
        c = lax.broadcasted_iota(jnp.int32, (tm, tm), 1)
        tri_ref[...] = (c < r).astype(BF16)

    @pl.when(phase == 0)
    def _():
        cnt_ref[...] += jnp.sum(both, axis=0, keepdims=True)

    @pl.when((phase == 1) & (tile == 0))
    def _():
        cnt = jnp.broadcast_to(cnt_ref[...], (8, ROUTE_COLS))
        padded = jnp.ceil(cnt / bm) * bm
        l8 = lax.broadcasted_iota(jnp.int32, (8, ROUTE_COLS), 1)
        end = padded
        shift = 1
        while shift < N_EXPERTS:
            end = end + jnp.where(l8 >= shift, pltpu.roll(end, shift, axis=1), 0.0)
            shift *= 2
        start_ref[...] = (end - padded)[:1]
        carry_ref[...] = jnp.zeros_like(carry_ref)
        nbp = be_ref.shape[0]
        first = (lax.broadcasted_iota(jnp.int32, (nbp, ROUTE_COLS), 0) * bm).astype(F32)
        lb = lax.broadcasted_iota(jnp.int32, (nbp, ROUTE_COLS), 1)
        hits = jnp.where((lb < N_EXPERTS) & (jnp.broadcast_to(end[:1], (nbp, ROUTE_COLS)) <= first), 1.0, 0.0)
        be = jnp.minimum(jnp.sum(hits, axis=1, keepdims=True), N_EXPERTS - 1.0)
        be_ref[...] = jnp.broadcast_to(be, (nbp, ROUTE_COLS)).astype(jnp.int32)

    @pl.when(phase == 1)
    def _():
        before = jnp.dot(tri_ref[...], both.astype(BF16), preferred_element_type=F32) + carry_ref[...]
        slot = before + start_ref[...]
        dest = [jnp.sum(jnp.where(h, slot, 0.0), axis=1, keepdims=True) for h in hot]
        dest_ref[...] = jnp.where(lane == 0, dest[0], jnp.where(lane == 1, dest[1], 0.0)).astype(jnp.int32)
        carry_ref[...] += jnp.sum(both, axis=0, keepdims=True)


def _route_plan(experts, bm, tm):
    n = experts.shape[0]
    assert n % tm == 0
    n_blocks = -(-n * EXPERT_TOP_K // bm) + N_EXPERTS
    nbp = -(-n_blocks // 8) * 8
    dest, be = pl.pallas_call(
        functools.partial(_route_plan_kernel, bm=bm, n_blocks=n_blocks),
        grid=(2, n // tm),
        in_specs=[pl.BlockSpec((tm, ROUTE_COLS), lambda p, t: (t, 0))],
        out_specs=[pl.BlockSpec((tm, ROUTE_COLS), lambda p, t: (t * p, 0)),
                   pl.BlockSpec((nbp, ROUTE_COLS), lambda p, t: (0, 0))],
        out_shape=[jax.ShapeDtypeStruct((n, ROUTE_COLS), jnp.int32),
                   jax.ShapeDtypeStruct((nbp, ROUTE_COLS), jnp.int32)],
        scratch_shapes=[pltpu.VMEM((1, ROUTE_COLS), F32), pltpu.VMEM((1, ROUTE_COLS), F32),
                        pltpu.VMEM((1, ROUTE_COLS), F32), pltpu.VMEM((tm, tm), BF16)],
        compiler_params=_cparams(("arbitrary", "arbitrary")),
        name="route_plan",
    )(experts)
    return dest[:, :EXPERT_TOP_K].reshape(-1), be[:n_blocks, 0], n_blocks


def _dispatch_kernel(dest_ref, h_ref, xs_in, xs_hbm, sem, *, tok0):
    del xs_in
    tm = h_ref.shape[0]
    base = (tok0 + pl.program_id(0) * tm) * EXPERT_TOP_K

    def copy(r, k):
        return pltpu.make_async_copy(h_ref.at[pl.ds(r, 1), :],
                                     xs_hbm.at[pl.ds(dest_ref[base + EXPERT_TOP_K * r + k], 1), :], sem)

    def start(r, c):
        for k in range(EXPERT_TOP_K):
            copy(r, k).start()
        return c

    def wait(r, c):
        for k in range(EXPERT_TOP_K):
            copy(r, k).wait()
        return c

    lax.fori_loop(0, tm, start, 0, unroll=4)
    lax.fori_loop(0, tm, wait, 0, unroll=4)


def _dispatch_rows(dest, h, xs, tok0, tm):
    n, d = h.shape
    assert n % tm == 0
    grid_spec = pltpu.PrefetchScalarGridSpec(
        num_scalar_prefetch=1,
        grid=(n // tm,),
        in_specs=[pl.BlockSpec((tm, d), lambda i, dst: (i, 0)), pl.BlockSpec(memory_space=pl.ANY)],
        out_specs=pl.BlockSpec(memory_space=pl.ANY),
        scratch_shapes=[pltpu.SemaphoreType.DMA(())],
    )
    return pl.pallas_call(
        functools.partial(_dispatch_kernel, tok0=tok0),
        grid_spec=grid_spec,
        out_shape=jax.ShapeDtypeStruct(xs.shape, xs.dtype),
        input_output_aliases={2: 0},
        compiler_params=_cparams(("arbitrary",)),
        name="moe_dispatch",
    )(dest, h, xs)


def _expert_kernel(be_ref, xs_ref, wg_ref, wu_ref, wd_ref, ys_ref, wgb, wub, wdb):
    b = pl.program_id(0)

    @pl.when((b == 0) | (be_ref[b] != be_ref[jnp.maximum(b - 1, 0)]))
    def _():
        wgb[...] = wg_ref[0].astype(BF16)
        wub[...] = wu_ref[0].astype(BF16)
        wdb[...] = wd_ref[0].astype(BF16)

    xb = xs_ref[...].astype(BF16)
    gate = jnp.dot(xb, wgb[...], preferred_element_type=F32)
    up = jnp.dot(xb, wub[...], preferred_element_type=F32)
    act = (jax.nn.silu(gate) * up).astype(BF16)
    ys_ref[...] = jnp.dot(act, wdb[...], preferred_element_type=F32)


def _expert_ffn(block_e, xs, w_gate, w_up, w_down, bm):
    n_slots, d = xs.shape
    ff = w_gate.shape[2]
    grid_spec = pltpu.PrefetchScalarGridSpec(
        num_scalar_prefetch=1,
        grid=(n_slots // bm,),
        in_specs=[pl.BlockSpec((bm, d), lambda b, be: (b, 0)),
                  pl.BlockSpec((1, d, ff), lambda b, be: (be[b], 0, 0)),
                  pl.BlockSpec((1, d, ff), lambda b, be: (be[b], 0, 0)),
                  pl.BlockSpec((1, ff, d), lambda b, be: (be[b], 0, 0))],
        out_specs=pl.BlockSpec((bm, d), lambda b, be: (b, 0)),
        scratch_shapes=[pltpu.VMEM((d, ff), BF16), pltpu.VMEM((d, ff), BF16), pltpu.VMEM((ff, d), BF16)],
    )
    return pl.pallas_call(
        _expert_kernel,
        grid_spec=grid_spec,
        out_shape=jax.ShapeDtypeStruct((n_slots, d), F32),
        compiler_params=_cparams(("arbitrary",)),
        name="expert_ffn",
    )(block_e, xs, w_gate, w_up, w_down)


def _nt(a, b):
    return lax.dot_general(a, b, (((1,), (1,)), ((), ())), preferred_element_type=F32)


def _flash_rows(s, v, m_prev, l_prev, acc_prev, v_transposed=False):
    m_new = jnp.maximum(m_prev, jnp.max(s, axis=-1, keepdims=True))
    alpha = jnp.exp2(m_prev - m_new)
    p = jnp.exp2(s - m_new)
    l_new = alpha * l_prev + jnp.sum(p, axis=-1, keepdims=True)
    pb = p.astype(BF16)
    pv = _nt(pb, v) if v_transposed else jnp.dot(pb, v, preferred_element_type=F32)
    return m_new, l_new, alpha * acc_prev + pv


def _page_copies(pt_ref, first_page, n, src_hbm, dst, sem, slab):
    out = []
    for k in range(n):
        p0 = pl.multiple_of(pt_ref[first_page + k] * slab, slab)
        out.append(pltpu.make_async_copy(src_hbm.at[pl.ds(p0, slab), :], dst.at[pl.ds(k * slab, slab), :], sem))
    return out


DIFF_RING = 3


def _diff_sample_kernel(pt_ref, q_ref, kc_hbm, vc_hbm, kn_ref, vn_ref, lam_ref, g_ref, o_ref,
                        kbuf, vbuf, sem, m_ref, l_ref, acc_ref, *, page, ppc, n_chunks, lam_init):
    b = pl.program_id(0)
    c = pl.program_id(1)
    step = b * n_chunks + c
    total = pl.num_programs(0) * n_chunks
    t = q_ref.shape[1]
    kslab = WB * page // LANES
    vslab = page * DIFF_HEADS

    def copies(s, slot):
        first = s * ppc
        return (_page_copies(pt_ref, first, ppc, kc_hbm, kbuf.at[slot], sem.at[0, slot], kslab)
                + _page_copies(pt_ref, first, ppc, vc_hbm, vbuf.at[slot], sem.at[1, slot], vslab))

    n_ring = kbuf.shape[0]
    ahead = n_ring - 1
    slot = step % n_ring

    @pl.when(step == 0)
    def _():
        for s in range(ahead):
            for cp in copies(s, s):
                cp.start()

    @pl.when(step + ahead < total)
    def _():
        for cp in copies(step + ahead, (step + ahead) % n_ring):
            cp.start()

    @pl.when(c == 0)
    def _():
        m_ref[...] = jnp.full_like(m_ref, NEG)
        l_ref[...] = jnp.zeros_like(l_ref)
        acc_ref[...] = jnp.zeros_like(acc_ref)

    for cp in copies(step, slot):
        cp.wait()

    q = q_ref[0].astype(F32)
    lane = lax.broadcasted_iota(jnp.int32, (t, 2 * HEAD_DIM), 1)

    def q_pair(h):
        qh = q[:, 2 * HEAD_DIM * h:2 * HEAD_DIM * (h + 1)]
        return jnp.concatenate([jnp.where(lane < HEAD_DIM, qh, 0.0), jnp.where(lane >= HEAD_DIM, qh, 0.0)],
                               axis=0).astype(BF16)

    two_hd = 2 * HEAD_DIM
    for h in range(DIFF_HEADS):
        kt = jnp.concatenate([kbuf[slot, kslab * k + two_hd * h:kslab * k + two_hd * (h + 1), :]
                              for k in range(ppc)], axis=1).astype(BF16)
        v2 = vbuf[slot, pl.ds(h, ppc * page, stride=DIFF_HEADS), :].astype(BF16)
        s = jnp.dot(q_pair(h), kt, preferred_element_type=F32)
        m, l, acc = _flash_rows(s, v2, m_ref[h], l_ref[h], acc_ref[h])
        m_ref[h] = m
        l_ref[h] = l
        acc_ref[h] = acc

    @pl.when(c == n_chunks - 1)
    def _():
        lam = _diff_lambda(lam_ref, lam_init)
        trow = lax.broadcasted_iota(jnp.int32, (2 * t, t), 0) % t
        tcol = lax.broadcasted_iota(jnp.int32, (2 * t, t), 1)
        for h in range(DIFF_HEADS):
            kn = kn_ref[0][:, 2 * HEAD_DIM * h:2 * HEAD_DIM * (h + 1)].astype(BF16)
            vn = vn_ref[0][:, DIFF_VDIM * h:DIFF_VDIM * (h + 1)].astype(BF16)
            s = jnp.where(tcol <= trow, _nt(q_pair(h), kn), NEG)
            m, l, acc = _flash_rows(s, vn, m_ref[h], l_ref[h], acc_ref[h])
            o = acc / l
            a = o[:t] - lam * o[t:]
            y = a * lax.rsqrt(jnp.mean(a * a, axis=-1, keepdims=True) + EPS) * g_ref[...] * (1.0 - lam_init)
            o_ref[0, :, DIFF_VDIM * h:DIFF_VDIM * (h + 1)] = y.astype(o_ref.dtype)


def _diff_sample(page_table, q3, kcache, vcache, kn3, vn3, diff_lambda, subln_g, lam_init, page, ppc):
    db, t, _ = q3.shape
    n_pages = page_table.shape[1]
    assert n_pages % ppc == 0 and t == 8
    n_chunks = n_pages // ppc
    tks = ppc * page
    per_b = lambda w: pl.BlockSpec((1, t, w), lambda b, c, pt: (b, 0, 0))
    const = lambda shape: pl.BlockSpec(shape, lambda b, c, pt: (0,) * len(shape))
    grid_spec = pltpu.PrefetchScalarGridSpec(
        num_scalar_prefetch=1,
        grid=(db, n_chunks),
        in_specs=[per_b(WB), pl.BlockSpec(memory_space=pl.ANY), pl.BlockSpec(memory_space=pl.ANY),
                  per_b(WB), per_b(WV), const(diff_lambda.shape), const((1, DIFF_VDIM))],
        out_specs=per_b(WV),
        scratch_shapes=[pltpu.VMEM((DIFF_RING, tks * WB // LANES, LANES), F32),
                        pltpu.VMEM((DIFF_RING, tks * DIFF_HEADS, DIFF_VDIM), F32),
                        pltpu.SemaphoreType.DMA((2, DIFF_RING)),
                        pltpu.VMEM((DIFF_HEADS, 2 * t, 1), F32), pltpu.VMEM((DIFF_HEADS, 2 * t, 1), F32),
                        pltpu.VMEM((DIFF_HEADS, 2 * t, DIFF_VDIM), F32)],
    )
    return pl.pallas_call(
        functools.partial(_diff_sample_kernel, page=page, ppc=ppc, n_chunks=n_chunks, lam_init=lam_init),
        grid_spec=grid_spec,
        out_shape=jax.ShapeDtypeStruct((db, t, WV), BF16),
        compiler_params=_cparams(("arbitrary", "arbitrary")),
        name="diff_sample",
    )(page_table.reshape(-1), q3, kcache, vcache, kn3, vn3, diff_lambda, subln_g.reshape(1, DIFF_VDIM))


def _softmax_rows(s, valid):
    sm = jnp.where(valid, s, NEG)
    m = jnp.max(sm, axis=-1, keepdims=True)
    e = jnp.where(valid, jnp.exp2(sm - m), 0.0)
    return e / jnp.maximum(jnp.sum(e, axis=-1, keepdims=True), 1e-30)


def _select_blocks_rows(score, blk):
    big = jnp.int32(2 ** 30)

    def body(_, carry):
        sc, bias = carry
        mx = jnp.max(sc, axis=-1, keepdims=True)
        first = jnp.min(jnp.where(sc == mx, blk, big), axis=-1, keepdims=True)
        hit = blk == first
        return jnp.where(hit, -3e38, sc), jnp.where(hit, 0.0, bias)

    return lax.fori_loop(0, TOP_N, body, (score, jnp.full(score.shape, NEG, F32)))[1]


def _pad_rows(x, rows):
    return jnp.concatenate([x, jnp.zeros((rows - x.shape[0], x.shape[1]), x.dtype)], axis=0)


TOK_PITCH = CMP_STRIDE + 1


def _nsa_sample_kernel(pt_ref, q_ref, qr_ref, ga_ref, cache_hbm, new_ref, wst_ref, wnew_ref,
                       w1k_ref, w1v_ref, ck_ref, cv_ref, w2k_ref, w2v_ref, impt_ref, exp_ref, o_ref,
                       ring, sring, kctok, vctok, abk_ref, abv_ref, s0_ref, s1_ref, sem, ssem, *, page, n_pages, ppc,
                       past, n_sel):
    b = pl.program_id(0)
    nb = pl.num_programs(0)
    t = q_ref.shape[1]
    g4 = NSA_GROUP
    n_sub = past // CMP_STRIDE
    n_cmp = n_sub - 1
    n_selp = impt_ref.shape[1]
    slab = 4 * KV * page // LANES
    half = slab // 2
    n_chunks = n_pages // ppc
    tks = ppc * page

    def copies(bb, c, second, dst, s):
        first = bb * n_pages + c * ppc
        out = []
        for k in range(ppc):
            p0 = pl.multiple_of(pt_ref[first + k] * slab + half * second, half)
            out.append(pltpu.make_async_copy(cache_hbm.at[pl.ds(p0, half), :], dst.at[pl.ds(k * half, half), :], s))
        return out

    n_ring = ring.shape[0]
    ahead = n_ring - 1

    def cmp_copies(bb, c):
        return copies(bb, c, 0, ring.at[c % n_ring], sem.at[c % n_ring])

    def sel_copies(c):
        return copies(b, c, 1, sring.at[c % 3], ssem.at[c % 3])

    def begin(jj):
        if jj + ahead < n_chunks:
            for cp in cmp_copies(b, jj + ahead):
                cp.start()
        else:
            @pl.when(b + 1 < nb)
            def _():
                for cp in cmp_copies(b + 1, jj + ahead - n_chunks):
                    cp.start()
        for cp in cmp_copies(b, jj):
            cp.wait()
        return jj % n_ring

    @pl.when(b == 0)
    def _():
        for c in range(ahead):
            for cp in cmp_copies(0, c):
                cp.start()

    for c in range(min(2, n_chunks)):
        for cp in sel_copies(c):
            cp.start()

    subs_pc = tks // CMP_STRIDE

    def compress_chunk(jj):
        base = TOK_PITCH * subs_pc * jj
        for x_ref, w_ref, ab_ref in ((kctok, w1k_ref, abk_ref), (vctok, w1v_ref, abv_ref)):
            acc = jnp.zeros((subs_pc, 2 * KV), F32)
            for r in range(0, CMP_STRIDE, 2):
                xr = jnp.concatenate([x_ref[pl.ds(base + r + d, subs_pc, stride=TOK_PITCH), :] for d in range(2)],
                                     axis=1)
                acc += jnp.dot(xr.astype(BF16), w_ref[r // 2], preferred_element_type=F32)
                yield
            ab_ref[jj * subs_pc:(jj + 1) * subs_pc, :] = acc

    prev = iter(())
    for jj in range(n_chunks):
        slot = begin(jj)
        for k in range(ppc):
            r0 = (jj * ppc + k) * page
            for src, dst in ((ring[slot, k * half:k * half + KV, :].T, kctok),
                             (ring[slot, k * half + KV:(k + 1) * half, :].T, vctok)):
                for u in range(page // CMP_STRIDE):
                    row = TOK_PITCH * (r0 // CMP_STRIDE + u)
                    dst[row:row + CMP_STRIDE, :] = src[u * CMP_STRIDE:(u + 1) * CMP_STRIDE]
            next(prev, None)
        for _ in prev:
            pass
        prev = compress_chunk(jj)
    for _ in prev:
        pass
    kcmp = jnp.dot(_compress_hidden(abk_ref[...], ck_ref[...]), w2k_ref[...],
                   preferred_element_type=F32).astype(BF16)
    vcmp = jnp.dot(_compress_hidden(abv_ref[...], cv_ref[...]), w2v_ref[...],
                   preferred_element_type=F32).astype(BF16)

    q = q_ref[0].astype(F32)
    qr = qr_ref[0].astype(F32)
    ga = ga_ref[0]
    lane = lax.broadcasted_iota(jnp.int32, (t, LANES), 1)
    trow = lax.broadcasted_iota(jnp.int32, (g4 * t, 1), 0) % t
    qpos = past + trow
    qpos_t = past + lax.broadcasted_iota(jnp.int32, (t, 1), 0)

    def to_half(x, have, want):
        return x if have == want else pltpu.roll(x, HEAD_DIM, axis=1)

    def q_rows(qq, h):
        keep = (lane >= HEAD_DIM * h) & (lane < HEAD_DIM * (h + 1))
        rows = []
        for g in range(g4):
            hd = g4 * h + g
            tile = to_half(qq[:, LANES * (hd // 2):LANES * (hd // 2 + 1)], hd % 2, h)
            rows.append(jnp.where(keep, tile, 0.0))
        return jnp.concatenate(rows, axis=0).astype(BF16)

    new = new_ref[0]
    ksn = _pad_rows(new[:, 2 * KV:3 * KV], LANES).astype(BF16)
    vsn = _pad_rows(new[:, 3 * KV:4 * KV], LANES).astype(BF16)
    kwst = wst_ref[0][:KV, :].astype(BF16)
    vwst = wst_ref[0][KV:, :].astype(BF16)
    kwn = _pad_rows(wnew_ref[0][:, :KV], LANES).astype(BF16)
    vwn = _pad_rows(wnew_ref[0][:, KV:], LANES).astype(BF16)
    wbuf = wst_ref.shape[2]
    ncol = lax.broadcasted_iota(jnp.int32, (g4 * t, LANES), 1)
    new_ok = (ncol < t) & (ncol <= trow)
    blocks_per_chunk = tks // SEL_BLOCK
    out_tiles = [[None, None] for _ in range(WA // LANES)]
    o_cmps, scores, qrhs = [], [], []
    blk = lax.broadcasted_iota(jnp.int32, (t, n_selp), 1)

    for h in range(NSA_KV_HEADS):
        s = _nt(q_rows(q, h), kcmp)
        nidx = lax.broadcasted_iota(jnp.int32, s.shape, 1)
        p = _softmax_rows(s, (nidx * CMP_STRIDE + (CMP_BLOCK - 1) <= qpos) & (nidx < n_cmp))
        o_cmps.append(jnp.dot(p.astype(BF16), vcmp, preferred_element_type=F32))
        psum = p[:t]
        for g in range(1, g4):
            psum = psum + p[g * t:(g + 1) * t]
        imp = jnp.zeros((t, n_selp), F32)
        for part in _split3(psum):
            imp += jnp.dot(part, impt_ref[...], preferred_element_type=F32)
        cur = qpos_t // SEL_BLOCK
        forced = (blk == 0) | (blk == cur) | (blk == cur - 1)
        score = jnp.where(blk > cur, -1.0, jnp.where(forced, FORCED_SCORE, imp))
        scores.append(jnp.where(blk < n_sel, score, -3e38))
        qrhs.append(q_rows(qr, h))

    nh = NSA_KV_HEADS
    sc = jnp.concatenate(scores + [jnp.full((LANES - nh * t, n_selp), -3e38, F32)], axis=0).T
    n_sel8 = -(-n_sel // 8) * 8
    picked = _select_blocks(sc[:n_sel8], lax.broadcasted_iota(jnp.int32, (n_sel8, LANES), 0))
    selbias = jnp.concatenate([picked, jnp.full((n_selp - n_sel8, LANES), NEG, F32)], axis=0).T[:nh * t]
    qr2 = jnp.concatenate(qrhs, axis=0)
    rows2 = nh * g4 * t

    def by_rows(x):
        return jnp.concatenate([x[h * t:(h + 1) * t] for h in range(nh) for _ in range(g4)], axis=0)

    s_refs = (s0_ref, s1_ref)
    pgs = 4
    m_hist = [jnp.full((rows2, 1), NEG, F32)]
    accs = [jnp.zeros((rows2, LANES + SUM_ROWS), F32)]

    def pages(c, k0, second):
        return jnp.concatenate([sring[c % 3, k * half + KV * second:k * half + KV * (second + 1), :]
                                for k in range(k0, k0 + pgs)], axis=1).astype(BF16)

    def score(c):
        b0 = c * blocks_per_chunk
        tile = selbias[:, LANES * (b0 // LANES):LANES * (b0 // LANES + 1)].astype(BF16)
        bias = by_rows(jnp.dot(tile, exp_ref[(b0 % LANES) // blocks_per_chunk], preferred_element_type=F32))
        m_run = m_hist[-1]
        for k0 in range(0, ppc, pgs):
            cols = slice(k0 * page, (k0 + pgs) * page)
            s = jnp.dot(qr2, pages(c, k0, 0), preferred_element_type=F32) + bias[:, cols]
            s_refs[c % 2][:, cols] = s
            m_run = jnp.maximum(m_run, jnp.max(s, axis=-1, keepdims=True))
            yield
        m_hist.append(m_run)

    def accumulate(c):
        m_new = m_hist[c + 1]
        acc = jnp.exp2(m_hist[c] - m_new) * accs[-1]
        for k0 in range(0, ppc, pgs):
            cols = slice(k0 * page, (k0 + pgs) * page)
            p = jnp.exp2(s_refs[c % 2][:, cols] - m_new).astype(BF16)
            acc = acc + _nt(p, _with_ones(pages(c, k0, 1)))
            yield
        accs.append(acc)

    for cp in sel_copies(0):
        cp.wait()
    for _ in score(0):
        pass
    for c in range(n_chunks):
        if c + 2 < n_chunks:
            for cp in sel_copies(c + 2):
                cp.start()
        nxt = iter(())
        if c + 1 < n_chunks:
            for cp in sel_copies(c + 1):
                cp.wait()
            nxt = score(c + 1)
        for _ in accumulate(c):
            next(nxt, None)
        for _ in nxt:
            pass
    m, acc = m_hist[-1], accs[-1]

    nb_blk = past // SEL_BLOCK
    new_ok2 = jnp.concatenate([new_ok] * nh, axis=0)
    s = jnp.where(new_ok2, _nt(qr2, ksn) + by_rows(selbias[:, nb_blk:nb_blk + 1]), NEG)
    m_new = jnp.maximum(m, jnp.max(s, axis=-1, keepdims=True))
    alpha = jnp.exp2(m - m_new)
    p = jnp.exp2(s - m_new)
    l = alpha * acc[:, LANES:LANES + 1] + jnp.sum(p, axis=-1, keepdims=True)
    o_sel2 = (alpha * acc[:, :LANES] + jnp.dot(p.astype(BF16), vsn, preferred_element_type=F32)) / jnp.maximum(l, 1e-30)

    for h in range(NSA_KV_HEADS):
        qrh, o_cmp = qrhs[h], o_cmps[h]
        o_sel = o_sel2[h * g4 * t:(h + 1) * g4 * t]

        s = jnp.concatenate([jnp.dot(qrh, kwst, preferred_element_type=F32), _nt(qrh, kwn)], axis=1)
        widx = lax.broadcasted_iota(jnp.int32, s.shape, 1)
        kpos = past - wbuf + widx
        valid = (kpos <= qpos) & (kpos >= qpos - WINDOW) & (widx < wbuf + t)
        p = _softmax_rows(s, valid).astype(BF16)
        o_win = _nt(p[:, :wbuf], vwst) + jnp.dot(p[:, wbuf:], vwn, preferred_element_type=F32)

        for g in range(g4):
            hd = g4 * h + g
            r = 3 * hd
            rs = slice(g * t, (g + 1) * t)
            o = ga[:, r:r + 1] * o_cmp[rs] + ga[:, r + 1:r + 2] * o_sel[rs] + ga[:, r + 2:r + 3] * o_win[rs]
            out_tiles[hd // 2][hd % 2] = to_half(o, h, hd % 2)

    for k, (lo, hi) in enumerate(out_tiles):
        o_ref[0, :, LANES * k:LANES * (k + 1)] = jnp.where(lane < HEAD_DIM, lo, hi).astype(o_ref.dtype)


def _nsa_sample(page_table, q3, qr3, ga3, cache, new3, win_state, wnew3, cw, page, ppc):
    db, t, _ = q3.shape
    n_pages = page_table.shape[1]
    past = n_pages * page
    wbuf = win_state.shape[2]
    w1k, w1v, ck, cv, w2k, w2v, _ = cw
    n_sub = past // CMP_STRIDE
    n_sel = -(-(past + t) // SEL_BLOCK)
    n_selp = -(-n_sel // LANES) * LANES
    tks = ppc * page
    blocks_per_chunk = tks // SEL_BLOCK
    assert t == 8 and (past + t) // CMP_STRIDE == n_sub and n_pages % ppc == 0 and past % SEL_BLOCK == 0
    assert LANES % blocks_per_chunk == 0 and wbuf == WINDOW and t <= SEL_BLOCK and page == LANES
    n_ring = 4 if (n_pages // ppc) % 4 == 0 else 2
    assert (n_pages // ppc) % n_ring == 0 and ppc % 4 == 0
    impt = jnp.pad(_importance_matrix(n_sel, n_sub).T, ((0, 0), (0, n_selp - n_sel)))
    m = jnp.arange(LANES // blocks_per_chunk)[:, None, None]
    j = jnp.arange(LANES)[None, :, None]
    u = jnp.arange(tks)[None, None, :]
    expand = (j == blocks_per_chunk * m + u // SEL_BLOCK).astype(BF16)
    per_b = lambda r, w: pl.BlockSpec((1, r, w), lambda b, pt: (b, 0, 0))
    const = lambda a: pl.BlockSpec(a.shape, lambda b, pt: (0,) * a.ndim)
    grid_spec = pltpu.PrefetchScalarGridSpec(
        num_scalar_prefetch=1,
        grid=(db,),
        in_specs=[per_b(t, WA), per_b(t, WA), per_b(t, LANES), pl.BlockSpec(memory_space=pl.ANY),
                  per_b(t, 4 * KV), per_b(2 * KV, wbuf), per_b(t, 2 * KV),
                  const(w1k), const(w1v), const(ck), const(cv), const(w2k), const(w2v), const(impt),
                  const(expand)],
        out_specs=per_b(t, WA),
        scratch_shapes=[pltpu.VMEM((n_ring, ppc * 2 * KV, LANES), F32), pltpu.VMEM((3, ppc * 2 * KV, LANES), F32),
                        pltpu.VMEM(((n_sub + 1) * TOK_PITCH, KV), F32),
                        pltpu.VMEM(((n_sub + 1) * TOK_PITCH, KV), F32),
                        pltpu.VMEM((n_sub, 2 * KV), F32), pltpu.VMEM((n_sub, 2 * KV), F32),
                        pltpu.VMEM((NSA_HEADS * t, tks), F32), pltpu.VMEM((NSA_HEADS * t, tks), F32),
                        pltpu.SemaphoreType.DMA((n_ring,)), pltpu.SemaphoreType.DMA((3,))],
    )
    return pl.pallas_call(
        functools.partial(_nsa_sample_kernel, page=page, n_pages=n_pages, ppc=ppc, past=past, n_sel=n_sel),
        grid_spec=grid_spec,
        out_shape=jax.ShapeDtypeStruct((db, t, WA), BF16),
        compiler_params=_cparams(("arbitrary",)),
        name="nsa_sample",
    )(page_table.reshape(-1), q3, qr3, ga3, cache, new3, win_state, wnew3, w1k, w1v, ck, cv, w2k, w2v, impt,
      expand)


def _combine_kernel(dest_ref, x_ref, w_ref, g_ref, ys_hbm, o_ref, buf, sem, *, tok0):
    i = pl.program_id(0)
    n = pl.num_programs(0)
    tm = x_ref.shape[0]

    def copy(tile, slot, r, k):
        src = dest_ref[(tok0 + tile * tm + r) * EXPERT_TOP_K + k]
        return pltpu.make_async_copy(ys_hbm.at[pl.ds(src, 1), :], buf.at[slot, k, pl.ds(r, 1), :], sem.at[slot])

    def start_tile(tile, slot):
        def body(r, c):
            for k in range(EXPERT_TOP_K):
                copy(tile, slot, r, k).start()
            return c
        lax.fori_loop(0, tm, body, 0, unroll=4)

    slot = i % 2

    @pl.when(i == 0)
    def _():
        start_tile(0, 0)

    @pl.when(i + 1 < n)
    def _():
        start_tile(i + 1, 1 - slot)

    def wait(r, c):
        for k in range(EXPERT_TOP_K):
            copy(i, slot, r, k).wait()
        return c

    lax.fori_loop(0, tm, wait, 0, unroll=4)
    x = x_ref[...]
    for k in range(EXPERT_TOP_K):
        x = x + w_ref[:, k:k + 1] * buf[slot, k]
    o_ref[...] = x * lax.rsqrt(jnp.mean(x * x, axis=-1, keepdims=True) + EPS) * g_ref[...]


def _combine_norm(dest, x1, weights, ys, g, tok0, tm):
    n, d = x1.shape
    assert n % tm == 0
    grid_spec = pltpu.PrefetchScalarGridSpec(
        num_scalar_prefetch=1,
        grid=(n // tm,),
        in_specs=[pl.BlockSpec((tm, d), lambda i, dst: (i, 0)), pl.BlockSpec((tm, ROUTE_COLS), lambda i, dst: (i, 0)),
                  pl.BlockSpec((1, d), lambda i, dst: (0, 0)), pl.BlockSpec(memory_space=pl.ANY)],
        out_specs=pl.BlockSpec((tm, d), lambda i, dst: (i, 0)),
        scratch_shapes=[pltpu.VMEM((2, EXPERT_TOP_K, tm, d), F32), pltpu.SemaphoreType.DMA((2,))],
    )
    return pl.pallas_call(
        functools.partial(_combine_kernel, tok0=tok0),
        grid_spec=grid_spec,
        out_shape=jax.ShapeDtypeStruct((n, d), F32),
        compiler_params=_cparams(("arbitrary",)),
        name="moe_combine_norm",
    )(dest, x1, weights, g.reshape(1, d), ys)


MOE_ROWS = 256


def kernel(x_prompt, x_sample, cache_nsa_kv, cache_diff_k, cache_diff_v, state_nsa_win_kv, page_table,
           norm_mix_g, w_in, nsa_cmp_pos, nsa_cmp_k_w1, nsa_cmp_k_w2, nsa_cmp_v_w1, nsa_cmp_v_w2,
           diff_lambda, diff_subln_g, w_proj_a, w_proj_b, w_out, norm_ffn_g,
           router_group_w, router_group_b, router_expert_w, router_expert_b,
           expert_w_gate, expert_w_up, expert_w_down, norm_final_g):
    depth = w_in.shape[0]
    bsz, seq, d = x_prompt.shape
    db, t, _ = x_sample.shape
    n_pool, page = cache_nsa_kv.shape[1:3]
    past = page_table.shape[1] * page
    wbuf = state_nsa_win_kv.shape[2]
    assert depth == 1 and bsz == 1
    l = 0
    lam_init = 0.8 - 0.6 * math.exp(-0.3 * l)
    w = _split_w_in(w_in[l], d)
    cw = _compress_weights(nsa_cmp_pos[l], nsa_cmp_k_w1[l], nsa_cmp_k_w2[l], nsa_cmp_v_w1[l], nsa_cmp_v_w2[l])
    wa, wb, wo = w_proj_a[l].astype(BF16), w_proj_b[l].astype(BF16), w_out[l].astype(BF16)
    rw = jnp.pad(jnp.concatenate([router_group_w[l], router_expert_w[l]], axis=1),
                 ((0, 0), (0, ROUTE_COLS - N_GROUPS - N_EXPERTS)))
    rb = jnp.pad(jnp.concatenate([router_group_b[l], router_expert_b[l]]),
                 (0, ROUTE_COLS - N_GROUPS - N_EXPERTS)).reshape(1, ROUTE_COLS)
    rwh = rw.astype(BF16)
    rwl = (rw - rwh.astype(F32)).astype(BF16)

    xp = x_prompt.reshape(seq, d)
    (nsa_p, win_p, dk_p, dv_p, gm_p, ks_b, kw_b, kb_b,
     qat, qart, qbt, vst, vwt, vbt, gat) = _project(xp, jnp.arange(seq), norm_mix_g[l], w, 512, True)
    kcmp, _, vcmpt = _compress(nsa_p, cw, min(256, seq // CMP_STRIDE))
    oat = _nsa_prompt(qat, qart, gat, kcmp, vcmpt, ks_b, vst, kw_b, vwt, 128, 512)
    obt = _diff_prompt(qbt, kb_b, vbt, diff_lambda[l], diff_subln_g[l], lam_init, 512, 512)
    x1p, h2p, re_p, rw_p = _merge(oat, obt, gm_p, xp, wa, wb, wo, norm_ffn_g[l], rwh, rwl, rb, 512, True)

    ns = db * t
    xs = x_sample.reshape(ns, d)
    pos_s = past + jnp.arange(ns) % t
    (nsa_s, win_s, dk_s, dv_s, gm_s, _, _, _, qa_s, qar_s, qb_s, ga_s) = _project(
        xs, pos_s, norm_mix_g[l], w, ns, False)
    r3 = lambda a: a.reshape(db, t, a.shape[-1])
    slabs = lambda c: c.transpose(0, 2, 3, 4, 1).reshape(-1, page)
    state_t = state_nsa_win_kv[l].transpose(0, 2, 3, 4, 1).reshape(db, 2 * KV, wbuf)
    oa_s = _nsa_sample(page_table, r3(qa_s), r3(qar_s), r3(ga_s), slabs(cache_nsa_kv[l]),
                       r3(nsa_s), state_t, r3(win_s), cw, page, 16)
    ob_s = _diff_sample(page_table, r3(qb_s), slabs(cache_diff_k[l]),
                        cache_diff_v[l].reshape(-1, DIFF_VDIM), r3(dk_s), r3(dv_s),
                        diff_lambda[l], diff_subln_g[l], lam_init, page, 16)
    x1s, h2s, re_s, rw_s = _merge(oa_s.reshape(ns, -1), ob_s.reshape(ns, -1), gm_s, xs, wa, wb, wo,
                                  norm_ffn_g[l], rwh, rwl, rb, ns, False)

    n_all = seq + ns
    plan_tile = max(tm for tm in range(8, 1025, 8) if n_all % tm == 0)
    dest, block_e, n_blocks = _route_plan(jnp.concatenate([re_p, re_s], axis=0), MOE_ROWS, plan_tile)
    xs = jnp.zeros((n_blocks * MOE_ROWS, d), F32)
    xs = _dispatch_rows(dest, h2p, xs, 0, 256)
    xs = _dispatch_rows(dest, h2s, xs, seq, ns)
    ys = _expert_ffn(block_e, xs, expert_w_gate[l], expert_w_up[l], expert_w_down[l], MOE_ROWS)
    y_prompt = _combine_norm(dest, x1p, rw_p, ys, norm_final_g, 0, 256)
    y_sample = _combine_norm(dest, x1s, rw_s, ys, norm_final_g, seq, ns)

    wn = min(WINDOW, seq)
    win_all_t = jnp.concatenate([state_t, r3(win_s).transpose(0, 2, 1)], axis=2)[:, :, -min(WINDOW, past + t):]
    win_all = win_all_t.reshape(db, 2, NSA_KV_HEADS, HEAD_DIM, -1).transpose(0, 4, 1, 2, 3)
    kvs = (4, NSA_KV_HEADS, HEAD_DIM)
    dks = (DIFF_HEADS, 2, HEAD_DIM)
    dvs = (DIFF_HEADS, DIFF_VDIM)
    return (y_prompt.reshape(1, seq, d), y_sample.reshape(db, t, d),
            nsa_p.reshape((1, 1, seq) + kvs), nsa_s.reshape((1, db, t) + kvs),
            dk_p.reshape((1, 1, seq) + dks), dk_s.reshape((1, db, t) + dks),
            dv_p.reshape((1, 1, seq) + dvs), dv_s.reshape((1, db, t) + dvs),
            win_p[seq - wn:].reshape(1, 1, wn, 2, NSA_KV_HEADS, HEAD_DIM),
            win_all[None])
```

```python
import functools
import math

import jax
import jax.numpy as jnp
from jax import lax
from jax.experimental import pallas as pl
from jax.experimental.pallas import tpu as pltpu

F32 = jnp.float32
BF16 = jnp.bfloat16

HEAD_DIM = 64
HALF = HEAD_DIM // 2
NSA_HEADS = 8
NSA_KV_HEADS = 2
NSA_GROUP = NSA_HEADS // NSA_KV_HEADS
CMP_STRIDE = 16
CMP_BLOCK = 2 * CMP_STRIDE
SEL_BLOCK = 64
SEL_PER_CMP = SEL_BLOCK // CMP_STRIDE
TOP_N = 16
WINDOW = 512
FORCED_SCORE = 1e4
DIFF_HEADS = 4
DIFF_VDIM = 2 * HEAD_DIM
N_GROUPS = 4
EXPERTS_PER_GROUP = 8
N_EXPERTS = N_GROUPS * EXPERTS_PER_GROUP
EXPERT_TOP_K = 2
ROPE_THETA = 10000.0
EPS = 1e-6
NEG = -1e30
SCALE = HEAD_DIM ** -0.5 * math.log2(math.e)

LANES = 128
VMEM_LIMIT = 56 * 1024 * 1024

WA = NSA_HEADS * HEAD_DIM
KV = NSA_KV_HEADS * HEAD_DIM
WB = DIFF_HEADS * 2 * HEAD_DIM
WV = DIFF_HEADS * DIFF_VDIM
N_GATE = 3 * NSA_HEADS


def _cparams(sem, flags=None):
    return pltpu.CompilerParams(dimension_semantics=sem, vmem_limit_bytes=VMEM_LIMIT, flags=flags)


def _full(shape):
    return pl.BlockSpec(shape, lambda *_: (0,) * len(shape))


def _swap_halves(t):
    lane = lax.broadcasted_iota(jnp.int32, t.shape, 1)
    fwd = pltpu.roll(t, LANES - HALF, axis=1)
    bwd = pltpu.roll(t, HALF, axis=1)
    return jnp.where(lane % HEAD_DIM < HALF, fwd, bwd)


def _rope_rows(t, cos, sin):
    outs = []
    for a in range(0, t.shape[1], LANES):
        x = t[:, a:a + LANES]
        outs.append(x * cos + _swap_halves(x) * sin)
    return outs[0] if len(outs) == 1 else jnp.concatenate(outs, axis=1)


def _rope_cols(t, cos, sin):
    outs = []
    for a in range(0, t.shape[0], HEAD_DIM):
        x1 = t[a:a + HALF]
        x2 = t[a + HALF:a + HEAD_DIM]
        outs.append(x1 * cos - x2 * sin)
        outs.append(x2 * cos + x1 * sin)
    return jnp.concatenate(outs, axis=0)


def _proj_kernel(x_ref, g_ref, wn_ref, wt_ref, cosn_ref, sinn_ref, cost_ref, sint_ref,
                 nsa_ref, win_ref, dk_ref, dv_ref, gm_ref, ksb_ref, kwb_ref, kbb_ref, *rest, transposed_q):
    x = x_ref[...]
    h = x * lax.rsqrt(jnp.mean(x * x, axis=-1, keepdims=True) + EPS) * g_ref[...]
    hb = h.astype(BF16)
    cosn = cosn_ref[...]
    sinn = sinn_ref[...]

    def mm(a, b):
        return jnp.dot(hb, wn_ref[:, a:b], preferred_element_type=F32)

    c = 0
    y = mm(c, c + 4 * KV)
    ks = _rope_rows(y[:, 2 * KV:3 * KV], cosn, sinn)
    nsa_ref[:, :2 * KV] = y[:, :2 * KV]
    nsa_ref[:, 2 * KV:3 * KV] = ks
    nsa_ref[:, 3 * KV:] = y[:, 3 * KV:]
    ksb_ref[...] = ks.astype(BF16)
    c += 4 * KV
    y = mm(c, c + 2 * KV)
    kw = _rope_rows(y[:, :KV], cosn, sinn)
    win_ref[:, :KV] = kw
    win_ref[:, KV:] = y[:, KV:]
    kwb_ref[...] = kw.astype(BF16)
    c += 2 * KV
    kb = _rope_rows(mm(c, c + WB), cosn, sinn)
    dk_ref[...] = kb
    kbb_ref[...] = kb.astype(BF16)
    c += WB
    dv_ref[...] = mm(c, c + WV)
    c += WV
    d_model = x.shape[1]
    gm_ref[...] = jax.nn.sigmoid(mm(c, c + 2 * d_model))
    c += 2 * d_model

    if transposed_q:
        qat_ref, qart_ref, qbt_ref, vst_ref, vwt_ref, vbt_ref, gat_ref = rest
        cost = cost_ref[...]
        sint = sint_ref[...]

        def mmt(a, b):
            return lax.dot_general(wt_ref[a:b, :], hb, (((1,), (1,)), ((), ())), preferred_element_type=F32)

        r = 0
        qa = mmt(r, r + WA) * SCALE
        qat_ref[...] = qa.astype(BF16)
        qart_ref[...] = _rope_cols(qa, cost, sint).astype(BF16)
        r += WA
        qbt_ref[...] = _rope_cols(mmt(r, r + WB) * SCALE, cost, sint).astype(BF16)
        r += WB
        vst_ref[...] = mmt(r, r + KV).astype(BF16)
        r += KV
        vwt_ref[...] = mmt(r, r + KV).astype(BF16)
        r += KV
        vbt_ref[...] = mmt(r, r + WV).astype(BF16)
        r += WV
        gat_ref[...] = jax.nn.sigmoid(mmt(r, r + 32))
    else:
        qa_ref, qar_ref, qb_ref, ga_ref = rest
        qa = mm(c, c + WA) * SCALE
        qa_ref[...] = qa.astype(BF16)
        qar_ref[...] = _rope_rows(qa, cosn, sinn).astype(BF16)
        c += WA
        qb_ref[...] = _rope_rows(mm(c, c + WB) * SCALE, cosn, sinn).astype(BF16)
        c += WB
        ga_ref[...] = jax.nn.sigmoid(mm(c, c + LANES))


def _split_w_in(w_in, d_model):
    sizes = [WA, KV, KV, KV, KV, KV, KV, N_GATE, WB, WB, WV, 2 * d_model]
    offs = [0]
    for s in sizes:
        offs.append(offs[-1] + s)
    names = ["qa", "kc", "vc", "ks", "vs", "kw", "vw", "ga", "qb", "kb", "vb", "gm"]
    return {n: w_in[:, offs[i]:offs[i + 1]] for i, n in enumerate(names)}


def _rope_tables(pos):
    inv = ROPE_THETA ** (-jnp.arange(HALF, dtype=F32) / HALF)
    ang = pos.astype(F32)[:, None] * inv[None, :]
    cos, sin = jnp.cos(ang), jnp.sin(ang)
    cosn = jnp.tile(cos, (1, LANES // HALF))
    sinn = jnp.tile(jnp.concatenate([-sin, sin], axis=1), (1, LANES // HEAD_DIM))
    return cosn, sinn, cos.T, sin.T


def _project(x, pos, norm_g, w, tm, transposed_q):
    n, d = x.shape
    assert n % tm == 0
    cosn, sinn, cost, sint = _rope_tables(pos)
    wn_parts = [w["kc"], w["vc"], w["ks"], w["vs"], w["kw"], w["vw"], w["kb"], w["vb"], w["gm"]]
    ga_pad = jnp.pad(w["ga"], ((0, 0), (0, LANES - N_GATE)))
    if transposed_q:
        wt = jnp.concatenate([w["qa"], w["qb"], w["vs"], w["vw"], w["vb"], ga_pad[:, :32]], axis=1).T.astype(BF16)
    else:
        wn_parts += [w["qa"], w["qb"], ga_pad]
        wt = jnp.zeros((8, d), BF16)
    wn = jnp.concatenate(wn_parts, axis=1).astype(BF16)

    row = lambda c: pl.BlockSpec((tm, c), lambda i: (i, 0))
    col = lambda r: pl.BlockSpec((r, tm), lambda i: (0, i))
    out_shape = [jax.ShapeDtypeStruct((n, 4 * KV), F32), jax.ShapeDtypeStruct((n, 2 * KV), F32),
                 jax.ShapeDtypeStruct((n, WB), F32), jax.ShapeDtypeStruct((n, WV), F32),
                 jax.ShapeDtypeStruct((n, 2 * d), F32), jax.ShapeDtypeStruct((n, KV), BF16),
                 jax.ShapeDtypeStruct((n, KV), BF16), jax.ShapeDtypeStruct((n, WB), BF16)]
    out_specs = [row(4 * KV), row(2 * KV), row(WB), row(WV), row(2 * d), row(KV), row(KV), row(WB)]
    if transposed_q:
        out_shape += [jax.ShapeDtypeStruct((WA, n), BF16), jax.ShapeDtypeStruct((WA, n), BF16),
                      jax.ShapeDtypeStruct((WB, n), BF16), jax.ShapeDtypeStruct((KV, n), BF16),
                      jax.ShapeDtypeStruct((KV, n), BF16), jax.ShapeDtypeStruct((WV, n), BF16),
                      jax.ShapeDtypeStruct((32, n), F32)]
        out_specs += [col(WA), col(WA), col(WB), col(KV), col(KV), col(WV), col(32)]
    else:
        out_shape += [jax.ShapeDtypeStruct((n, WA), BF16), jax.ShapeDtypeStruct((n, WA), BF16),
                      jax.ShapeDtypeStruct((n, WB), BF16), jax.ShapeDtypeStruct((n, LANES), F32)]
        out_specs += [row(WA), row(WA), row(WB), row(LANES)]
    return pl.pallas_call(
        functools.partial(_proj_kernel, transposed_q=transposed_q),
        grid=(n // tm,),
        in_specs=[row(d), _full((1, d)), _full(wn.shape), _full(wt.shape),
                  row(LANES), row(LANES), col(HALF), col(HALF)],
        out_specs=out_specs,
        out_shape=out_shape,
        compiler_params=_cparams(("parallel",)),
        name="proj",
    )(x, norm_g.reshape(1, d), wn, wt, cosn, sinn, cost, sint)


def _pad_head(qt, slot):
    z = jnp.zeros_like(qt)
    return jnp.concatenate([qt, z] if slot == 0 else [z, qt], axis=0)


SUM_ROWS = 16


def _with_ones(vt):
    return jnp.concatenate([vt, jnp.ones((SUM_ROWS, vt.shape[1]), vt.dtype)], axis=0)


def _flash_step(st, vt1, m_prev, acc_prev):
    m_new = jnp.maximum(m_prev, jnp.max(st, axis=0, keepdims=True))
    p = jnp.exp2(st - m_new).astype(BF16)
    return m_new, jnp.exp2(m_prev - m_new) * acc_prev + jnp.dot(vt1, p, preferred_element_type=F32)


def _diff_lambda(lam_ref, lam_init):
    lv = lam_ref[...]
    a = jnp.sum(lv[0:1] * lv[1:2], axis=-1, keepdims=True)
    b = jnp.sum(lv[2:3] * lv[3:4], axis=-1, keepdims=True)
    return jnp.exp(a) - jnp.exp(b) + lam_init


DIFF_AHEAD = 3


def _diff_prompt_kernel(it_ref, jt_ref, qt_ref, k_ref, vt_ref, lam_ref, g_ref, o_ref, m_ref, acc_ref, *s_refs,
                        tq, tk, sub, lam_init):
    i = it_ref[pl.program_id(0)]
    j = jt_ref[pl.program_id(0)]
    first_diag = i * tq // tk

    @pl.when(j == 0)
    def _():
        m_ref[...] = jnp.full_like(m_ref, NEG)
        acc_ref[...] = jnp.zeros_like(acc_ref)

    n_maps = 2 * DIFF_HEADS
    subs = [slice(r, r + sub) for r in range(0, tk, sub)]

    def step(causal):
        def score(hc, out):
            h, c = divmod(hc, 2)
            qt = _pad_head(qt_ref[HEAD_DIM * hc:HEAD_DIM * (hc + 1), :], c)
            m_new = m_ref[hc:hc + 1]
            for rows in subs:
                st = jnp.dot(k_ref[rows, 2 * HEAD_DIM * h:2 * HEAD_DIM * (h + 1)], qt, preferred_element_type=F32)
                if causal:
                    kpos = j * tk + rows.start + lax.broadcasted_iota(jnp.int32, (sub, tq), 0)
                    qpos = i * tq + lax.broadcasted_iota(jnp.int32, (sub, tq), 1)
                    st = jnp.where(kpos <= qpos, st, NEG)
                s_refs[hc % len(s_refs)][rows, :] = st
                m_new = jnp.maximum(m_new, jnp.max(st, axis=0, keepdims=True))
                yield
            out.append(m_new)

        def accumulate(hc, m_new):
            h = hc // 2
            acc = jnp.exp2(m_ref[hc:hc + 1] - m_new) * acc_ref[hc]
            for rows in subs:
                p = jnp.exp2(s_refs[hc % len(s_refs)][rows, :] - m_new).astype(BF16)
                vt1 = _with_ones(vt_ref[DIFF_VDIM * h:DIFF_VDIM * (h + 1), rows])
                acc = acc + jnp.dot(vt1, p, preferred_element_type=F32)
                yield
            m_ref[hc:hc + 1] = m_new
            acc_ref[hc] = acc

        m_new = []
        for hc in range(DIFF_AHEAD):
            for _ in score(hc, m_new):
                pass
        for hc in range(n_maps):
            nxt = score(hc + DIFF_AHEAD, m_new) if hc + DIFF_AHEAD < n_maps else iter(())
            for _ in accumulate(hc, m_new[hc]):
                next(nxt, None)
            for _ in nxt:
                pass

    pl.when(j < first_diag)(functools.partial(step, False))
    pl.when(j >= first_diag)(functools.partial(step, True))

    @pl.when(j == ((i + 1) * tq - 1) // tk)
    def _():
        lam = _diff_lambda(lam_ref, lam_init)
        for h in range(DIFF_HEADS):
            a0, a1 = acc_ref[2 * h], acc_ref[2 * h + 1]
            o0 = a0[:DIFF_VDIM] / a0[DIFF_VDIM:DIFF_VDIM + 1]
            o1 = a1[:DIFF_VDIM] / a1[DIFF_VDIM:DIFF_VDIM + 1]
            a = o0 - lam * o1
            y = a * lax.rsqrt(jnp.mean(a * a, axis=0, keepdims=True) + EPS) * g_ref[...] * (1.0 - lam_init)
            o_ref[DIFF_VDIM * h:DIFF_VDIM * (h + 1), :] = y.astype(o_ref.dtype)


def _diff_prompt(qbt, kb, vbt, diff_lambda, subln_g, lam_init, tq, tk):
    n = kb.shape[0]
    assert n % tq == 0 and n % tk == 0 and tq % tk == 0
    pairs = [(i, j) for i in range(n // tq) for j in range(((i + 1) * tq - 1) // tk + 1)]
    it = jnp.asarray([p[0] for p in pairs], jnp.int32)
    jt = jnp.asarray([p[1] for p in pairs], jnp.int32)
    const = lambda shape: pl.BlockSpec(shape, lambda s, it, jt: (0,) * len(shape))
    grid_spec = pltpu.PrefetchScalarGridSpec(
        num_scalar_prefetch=2,
        grid=(len(pairs),),
        in_specs=[pl.BlockSpec((WB, tq), lambda s, it, jt: (0, it[s])),
                  pl.BlockSpec((tk, WB), lambda s, it, jt: (jt[s], 0)),
                  pl.BlockSpec((WV, tk), lambda s, it, jt: (0, jt[s])),
                  const(diff_lambda.shape), const((DIFF_VDIM, 1))],
        out_specs=pl.BlockSpec((WV, tq), lambda s, it, jt: (0, it[s])),
        scratch_shapes=[pltpu.VMEM((2 * DIFF_HEADS, tq), F32),
                        pltpu.VMEM((2 * DIFF_HEADS, DIFF_VDIM + SUM_ROWS, tq), F32),
                        ] + [pltpu.VMEM((tk, tq), F32)] * (DIFF_AHEAD + 1),
    )
    return pl.pallas_call(
        functools.partial(_diff_prompt_kernel, tq=tq, tk=tk, sub=min(tk, 2 * LANES), lam_init=lam_init),
        grid_spec=grid_spec,
        out_shape=jax.ShapeDtypeStruct((WV, n), BF16),
        compiler_params=_cparams(("arbitrary",)),
        name="diff_prompt",
    )(it, jt, qbt, kb, vbt, diff_lambda, subln_g.reshape(DIFF_VDIM, 1))


def _compress_weights(pos, k_w1, k_w2, v_w1, v_w2):
    hd = HEAD_DIM
    z = jnp.zeros((CMP_STRIDE, hd, hd), F32)

    def halves(w1):
        w3 = w1.reshape(CMP_BLOCK, hd, -1)
        return w3[:CMP_STRIDE], w3[CMP_STRIDE:]

    def expand(top, bot):
        rows = [[top, z, bot, z], [z, top, z, bot]]
        w = jnp.concatenate([jnp.concatenate(r, axis=2) for r in rows], axis=1)
        return w.reshape(CMP_STRIDE // 2, 2 * KV, 2 * KV).astype(BF16)

    pf = pos.reshape(1, -1)
    ck, cv = pf @ k_w1, pf @ v_w1
    z2 = jnp.zeros((hd, hd), F32)
    w2k = jnp.block([[k_w2, z2], [z2, k_w2]]).astype(BF16)
    w2v = jnp.block([[v_w2, z2], [z2, v_w2]]).astype(BF16)
    return (expand(*halves(k_w1)), expand(*halves(v_w1)), jnp.concatenate([ck, ck], axis=1),
            jnp.concatenate([cv, cv], axis=1), w2k, w2v, w2v.T)


def _compress_ab(x_ref, w_ref, n_sub, pitch=CMP_STRIDE):
    acc = jnp.zeros((n_sub, 2 * KV), F32)
    for r in range(0, CMP_STRIDE, 2):
        xr = jnp.concatenate([x_ref[pl.ds(r + d, n_sub, stride=pitch), :] for d in range(2)], axis=1)
        acc += jnp.dot(xr.astype(BF16), w_ref[r // 2], preferred_element_type=F32)
    return acc


def _compress_hidden(ab, c):
    n_sub = ab.shape[0]
    nxt = pltpu.roll(ab[:, KV:], n_sub - 1, axis=0)
    return jax.nn.gelu(ab[:, :KV] + nxt + c).astype(BF16)


def _compress_ab_kernel(xk_ref, xv_ref, wk_ref, wv_ref, abk_ref, abv_ref, *, n_sub):
    abk_ref[...] = _compress_ab(xk_ref, wk_ref, n_sub)
    abv_ref[...] = _compress_ab(xv_ref, wv_ref, n_sub)


def _compress_mlp_kernel(abk_ref, abv_ref, ck_ref, cv_ref, w2k_ref, w2v_ref, w2vt_ref, kc_ref, vc_ref, vct_ref):
    gk = _compress_hidden(abk_ref[...], ck_ref[...])
    gv = _compress_hidden(abv_ref[...], cv_ref[...])
    kc_ref[...] = jnp.dot(gk, w2k_ref[...], preferred_element_type=F32).astype(BF16)
    vc_ref[...] = jnp.dot(gv, w2v_ref[...], preferred_element_type=F32).astype(BF16)
    vct_ref[...] = lax.dot_general(w2vt_ref[...], gv, (((1,), (1,)), ((), ())),
                                   preferred_element_type=F32).astype(BF16)


def _compress(kv, cw, sub_tile):
    w1k, w1v, ck, cv, w2k, w2v, w2vt = cw
    n_sub = kv.shape[0] // CMP_STRIDE
    assert n_sub % sub_tile == 0
    ab_shape = jax.ShapeDtypeStruct((n_sub, 2 * KV), F32)
    ab_spec = pl.BlockSpec((sub_tile, 2 * KV), lambda i: (i, 0))
    abk, abv = pl.pallas_call(
        functools.partial(_compress_ab_kernel, n_sub=sub_tile),
        grid=(n_sub // sub_tile,),
        in_specs=[pl.BlockSpec((sub_tile * CMP_STRIDE, KV), lambda i: (i, 0)),
                  pl.BlockSpec((sub_tile * CMP_STRIDE, KV), lambda i: (i, 1)), _full(w1k.shape), _full(w1v.shape)],
        out_specs=[ab_spec, ab_spec],
        out_shape=[ab_shape, ab_shape],
        compiler_params=_cparams(("parallel",)),
        name="compress_ab",
    )(kv, kv, w1k, w1v)
    return pl.pallas_call(
        _compress_mlp_kernel,
        out_shape=[jax.ShapeDtypeStruct((n_sub, KV), BF16), jax.ShapeDtypeStruct((n_sub, KV), BF16),
                   jax.ShapeDtypeStruct((KV, n_sub), BF16)],
        compiler_params=pltpu.CompilerParams(vmem_limit_bytes=VMEM_LIMIT),
        name="compress_mlp",
    )(abk, abv, ck, cv, w2k, w2v, w2vt)


def _importance_matrix(n_sel, n_cmp):
    j = jnp.arange(n_sel)[:, None]
    n = jnp.arange(n_cmp)[None, :]
    return ((n >= SEL_PER_CMP * j - 1) & (n <= SEL_PER_CMP * j + SEL_PER_CMP - 1)).astype(BF16)


def _split3(x):
    hi = x.astype(BF16)
    r = x - hi.astype(F32)
    mid = r.astype(BF16)
    lo = (r - mid.astype(F32)).astype(BF16)
    return hi, mid, lo


def _tile_lanes(x, k):
    return jnp.concatenate([x] * k, axis=1)


def _select_blocks(score, blk):
    big = jnp.int32(2 ** 30)

    def body(_, carry):
        sc, bias = carry
        mx = jnp.max(sc, axis=0, keepdims=True)
        first = jnp.min(jnp.where(sc == mx, blk, big), axis=0, keepdims=True)
        hit = blk == first
        return jnp.where(hit, -3e38, sc), jnp.where(hit, 0.0, bias)

    n_pick = min(TOP_N, score.shape[0])
    return lax.fori_loop(0, n_pick, body, (score, jnp.full(score.shape, NEG, F32)))[1]


def _softmax_cols(st, valid):
    sm = jnp.where(valid, st, NEG)
    m = jnp.max(sm, axis=0, keepdims=True)
    e = jnp.where(valid, jnp.exp2(sm - m), 0.0)
    return e / jnp.maximum(jnp.sum(e, axis=0, keepdims=True), 1e-30)


NSA_GROUP_CHUNKS = 4
NSA_AHEAD = 3


def _nsa_prompt_kernel(qat_ref, qart_ref, gat_ref, kc_ref, vct_ref, imp_ref, hot_ref, ks_ref, vst_ref, kw_ref,
                       vwt_ref, o_ref, bias_ref, ocmp_ref, m_ref, acc_ref, *s_refs, tq, tk, sub, n_cmp):
    i = pl.program_id(0)
    g4 = NSA_GROUP
    ncp = kc_ref.shape[0]
    n_sel = imp_ref.shape[0]
    q0 = i * tq
    qlane = q0 + lax.broadcasted_iota(jnp.int32, (1, tq), 1)

    def heads_t(ref, h):
        return jnp.concatenate([ref[HEAD_DIM * (g4 * h + g):HEAD_DIM * (g4 * h + g + 1), :] for g in range(g4)],
                               axis=1)

    for h in range(NSA_KV_HEADS):
        qt = _pad_head(heads_t(qat_ref, h), h)
        st = jnp.dot(kc_ref[...], qt, preferred_element_type=F32)
        nrow = lax.broadcasted_iota(jnp.int32, (ncp, tq), 0)
        valid = (nrow * CMP_STRIDE + (CMP_BLOCK - 1) <= qlane) & (nrow < n_cmp)
        p = _softmax_cols(st, _tile_lanes(valid, g4))
        ocmp_ref[h] = jnp.dot(vct_ref[HEAD_DIM * h:HEAD_DIM * (h + 1), :], p.astype(BF16),
                              preferred_element_type=F32)
        psum = p[:, :tq]
        for g in range(1, g4):
            psum = psum + p[:, g * tq:(g + 1) * tq]
        imp = jnp.zeros((n_sel, tq), F32)
        for part in _split3(psum):
            imp += jnp.dot(imp_ref[...], part, preferred_element_type=F32)
        blk = lax.broadcasted_iota(jnp.int32, (n_sel, tq), 0)
        cur = qlane // SEL_BLOCK
        forced = (blk == 0) | (blk == cur) | (blk == cur - 1)
        score = jnp.where(blk > cur, -1.0, jnp.where(forced, FORCED_SCORE, imp))
        bias_ref[h] = _select_blocks(score, blk)

    qrts = [_pad_head(heads_t(qart_ref, h), h) for h in range(NSA_KV_HEADS)]
    per_chunk = tk // SEL_BLOCK
    zpad = jnp.zeros((LANES - per_chunk, g4 * tq), F32)
    subs = [slice(r, r + sub) for r in range(0, tk, sub)]
    n_key_chunks = ks_ref.shape[0] // tk
    m_ref[...] = jnp.full_like(m_ref, NEG)
    acc_ref[...] = jnp.zeros_like(acc_ref)

    def group(c_base, masked, n_group):
        n_maps = n_group * NSA_KV_HEADS
        m_hist = [[m_ref[h:h + 1]] for h in range(NSA_KV_HEADS)]

        def where(k):
            cc, h = divmod(k, NSA_KV_HEADS)
            c = c_base + cc
            cl = jnp.minimum(c, n_key_chunks - 1) if masked else c
            return h, c, cl, pl.multiple_of(cl * tk, tk)

        def score(k):
            h, c, cl, k0 = where(k)
            b8 = bias_ref[h, pl.ds(pl.multiple_of(cl * per_chunk, per_chunk), per_chunk), :]
            qx = jnp.concatenate([qrts[h], jnp.concatenate([_tile_lanes(b8, g4), zpad], axis=0).astype(BF16)],
                                 axis=0)
            m_run = m_hist[h][-1]
            for rows in subs:
                kx = jnp.concatenate([ks_ref[pl.ds(k0 + rows.start, sub), :], hot_ref[rows, :]], axis=1)
                st = jnp.dot(kx, qx, preferred_element_type=F32)
                if masked:
                    kpos = c * tk + rows.start + lax.broadcasted_iota(jnp.int32, (sub, tq), 0)
                    st = jnp.where(_tile_lanes(kpos <= qlane, g4), st, NEG)
                s_refs[k % len(s_refs)][rows, :] = st
                m_run = jnp.maximum(m_run, jnp.max(st, axis=0, keepdims=True))
                yield
            m_hist[h].append(m_run)

        def accumulate(k):
            h, _, _, k0 = where(k)
            cc = k // NSA_KV_HEADS
            m_new = m_hist[h][cc + 1]
            acc = jnp.exp2(m_hist[h][cc] - m_new) * acc_ref[h]
            for rows in subs:
                p = jnp.exp2(s_refs[k % len(s_refs)][rows, :] - m_new).astype(BF16)
                vt1 = _with_ones(vst_ref[HEAD_DIM * h:HEAD_DIM * (h + 1), pl.ds(k0 + rows.start, sub)])
                acc = acc + jnp.dot(vt1, p, preferred_element_type=F32)
                yield
            acc_ref[h] = acc

        for k in range(min(NSA_AHEAD, n_maps)):
            for _ in score(k):
                pass
        for k in range(n_maps):
            nxt = score(k + NSA_AHEAD) if k + NSA_AHEAD < n_maps else iter(())
            for _ in accumulate(k):
                next(nxt, None)
            for _ in nxt:
                pass
        for h in range(NSA_KV_HEADS):
            m_ref[h:h + 1] = m_hist[h][-1]

    n_full = q0 // tk
    n_plain = n_full // NSA_GROUP_CHUNKS
    pair = NSA_GROUP_CHUNKS // 2
    rest = n_full - n_plain * NSA_GROUP_CHUNKS

    def plain(gi, carry):
        group(gi * NSA_GROUP_CHUNKS, False, NSA_GROUP_CHUNKS)
        return carry

    lax.fori_loop(0, n_plain, plain, 0)
    pl.when(rest >= pair)(lambda: group(n_plain * NSA_GROUP_CHUNKS, False, pair))
    group(n_plain * NSA_GROUP_CHUNKS + (rest // pair) * pair, True, pair)

    for h in range(NSA_KV_HEADS):
        qrt = qrts[h]
        o_cmp = ocmp_ref[h]
        acc = acc_ref[h]
        o_sel = acc[:HEAD_DIM] / jnp.maximum(acc[HEAD_DIM:HEAD_DIM + 1], 1e-30)

        nw = WINDOW + tq
        w0 = pl.multiple_of(jnp.maximum(q0 - WINDOW, 0), LANES)
        st = jnp.dot(kw_ref[pl.ds(w0, nw), :], qrt, preferred_element_type=F32)
        kpos = w0 + lax.broadcasted_iota(jnp.int32, (nw, tq), 0)
        valid = (kpos <= qlane) & (kpos >= qlane - WINDOW)
        p = _softmax_cols(st, _tile_lanes(valid, g4))
        o_win = jnp.dot(vwt_ref[HEAD_DIM * h:HEAD_DIM * (h + 1), pl.ds(w0, nw)], p.astype(BF16),
                        preferred_element_type=F32)

        for g in range(g4):
            r = 3 * (g4 * h + g)
            sl = slice(g * tq, (g + 1) * tq)
            o = (gat_ref[r:r + 1, :] * o_cmp[:, sl] + gat_ref[r + 1:r + 2, :] * o_sel[:, sl]
                 + gat_ref[r + 2:r + 3, :] * o_win[:, sl])
            o_ref[HEAD_DIM * (g4 * h + g):HEAD_DIM * (g4 * h + g + 1), :] = o.astype(o_ref.dtype)


def _nsa_prompt(qat, qart, gat, kc, vct, ks, vst, kw, vwt, tq, tk):
    n = ks.shape[0]
    n_sub = kc.shape[0]
    n_cmp = n_sub - 1
    n_sel = n // SEL_BLOCK
    assert n % tq == 0 and n % tk == 0 and tk % SEL_BLOCK == 0 and n >= WINDOW + tq and tq % LANES == 0
    assert tk % tq == 0 and tk // SEL_BLOCK <= LANES
    imp = _importance_matrix(n_sel, n_sub)
    hot = (jnp.arange(tk)[:, None] // SEL_BLOCK == jnp.arange(LANES)[None, :]).astype(BF16)
    col = lambda r: pl.BlockSpec((r, tq), lambda i: (0, i))
    return pl.pallas_call(
        functools.partial(_nsa_prompt_kernel, tq=tq, tk=tk, sub=min(tk, 2 * LANES), n_cmp=n_cmp),
        grid=(n // tq,),
        in_specs=[col(WA), col(WA), col(32), _full(kc.shape), _full(vct.shape), _full(imp.shape), _full(hot.shape),
                  _full(ks.shape), _full(vst.shape), _full(kw.shape), _full(vwt.shape)],
        out_specs=col(WA),
        out_shape=jax.ShapeDtypeStruct((WA, n), BF16),
        scratch_shapes=[pltpu.VMEM((NSA_KV_HEADS, n_sel, tq), F32),
                        pltpu.VMEM((NSA_KV_HEADS, HEAD_DIM, NSA_GROUP * tq), F32),
                        pltpu.VMEM((NSA_KV_HEADS, NSA_GROUP * tq), F32),
                        pltpu.VMEM((NSA_KV_HEADS, HEAD_DIM + SUM_ROWS, NSA_GROUP * tq), F32),
                        ] + [pltpu.VMEM((tk, NSA_GROUP * tq), F32)] * (NSA_AHEAD + 1),
        compiler_params=_cparams(("parallel",)),
        name="nsa_prompt",
    )(qat, qart, gat, kc, vct, imp, hot, ks, vst, kw, vwt)


ROUTE_COLS = LANES


def _first_lane_of_max(v, lane):
    mx = jnp.max(v, axis=-1, keepdims=True)
    return mx, jnp.min(jnp.where(v == mx, lane, ROUTE_COLS), axis=-1, keepdims=True)


def _merge_kernel(oa_ref, ob_ref, gm_ref, x_ref, wa_ref, wb_ref, wo_ref, g_ref, rwh_ref, rwl_ref, rb_ref,
                  x1_ref, h2_ref, re_ref, rw_ref, *, transposed):
    d = x_ref.shape[1]
    dims = (((0,), (0,)), ((), ())) if transposed else (((1,), (0,)), ((), ()))
    ya = lax.dot_general(oa_ref[...], wa_ref[...], dims, preferred_element_type=F32)
    yb = lax.dot_general(ob_ref[...], wb_ref[...], dims, preferred_element_type=F32)
    mix = gm_ref[:, :d] * ya + gm_ref[:, d:] * yb
    x1 = x_ref[...] + jnp.dot(mix.astype(BF16), wo_ref[...], preferred_element_type=F32)
    x1_ref[...] = x1
    h2 = x1 * lax.rsqrt(jnp.mean(x1 * x1, axis=-1, keepdims=True) + EPS) * g_ref[...]
    h2_ref[...] = h2

    hi = h2.astype(BF16)
    lo = (h2 - hi.astype(F32)).astype(BF16)
    logits = (jnp.dot(hi, rwh_ref[...], preferred_element_type=F32)
              + jnp.dot(lo, rwh_ref[...], preferred_element_type=F32)
              + jnp.dot(hi, rwl_ref[...], preferred_element_type=F32)) + rb_ref[...]
    lane = lax.broadcasted_iota(jnp.int32, logits.shape, 1)
    is_g = lane < N_GROUPS
    gl = jnp.where(is_g, logits, NEG)
    gmx, grp = _first_lane_of_max(gl, lane)
    p_grp = 1.0 / jnp.sum(jnp.where(is_g, jnp.exp(gl - gmx), 0.0), axis=-1, keepdims=True)
    e_id = lane - N_GROUPS
    in_grp = (e_id >= 0) & (e_id < N_EXPERTS) & (e_id // EXPERTS_PER_GROUP == grp)
    el = jnp.where(in_grp, logits, NEG)
    emx = jnp.max(el, axis=-1, keepdims=True)
    ee = jnp.where(in_grp, jnp.exp(el - emx), -1.0)
    e1, i1 = _first_lane_of_max(ee, lane)
    e2, i2 = _first_lane_of_max(jnp.where(lane == i1, -1.0, ee), lane)
    inv = p_grp / (e1 + e2)
    re_ref[...] = jnp.where(lane == 0, i1 - N_GROUPS, jnp.where(lane == 1, i2 - N_GROUPS, 0))
    rw_ref[...] = jnp.where(lane == 0, e1 * inv, jnp.where(lane == 1, e2 * inv, 0.0))


def _merge(oa, ob, gm, x, wa, wb, wo, ffn_g, rwh, rwl, rb, tm, transposed):
    n, d = x.shape
    assert n % tm == 0
    row = lambda c: pl.BlockSpec((tm, c), lambda i: (i, 0))
    o_spec = pl.BlockSpec((WA, tm), lambda i: (0, i)) if transposed else row(WA)
    return pl.pallas_call(
        functools.partial(_merge_kernel, transposed=transposed),
        grid=(n // tm,),
        in_specs=[o_spec, o_spec, row(2 * d), row(d), _full(wa.shape), _full(wb.shape), _full(wo.shape),
                  _full((1, d)), _full(rwh.shape), _full(rwl.shape), _full((1, ROUTE_COLS))],
        out_specs=[row(d), row(d), row(ROUTE_COLS), row(ROUTE_COLS)],
        out_shape=[jax.ShapeDtypeStruct((n, d), F32), jax.ShapeDtypeStruct((n, d), F32),
                   jax.ShapeDtypeStruct((n, ROUTE_COLS), jnp.int32), jax.ShapeDtypeStruct((n, ROUTE_COLS), F32)],
        compiler_params=_cparams(("parallel",)),
        name="merge_route",
    )(oa, ob, gm, x, wa, wb, wo, ffn_g.reshape(1, d), rwh, rwl, rb)


def _route_plan_kernel(re_ref, dest_ref, be_ref, cnt_ref, start_ref, carry_ref, tri_ref, *, bm, n_blocks):
    phase = pl.program_id(0)
    tile = pl.program_id(1)
    tm = re_ref.shape[0]
    lane = lax.broadcasted_iota(jnp.int32, (tm, ROUTE_COLS), 1)
    hot = [lane == re_ref[:, k:k + 1] for k in range(EXPERT_TOP_K)]
    both = sum(h.astype(F32) for h in hot)

    @pl.when((phase == 0) & (tile == 0))
    def _():
        cnt_ref[...] = jnp.zeros_like(cnt_ref)
        r = lax.broadcasted_iota(jnp.int32, (tm, tm), 0)
        c = lax.broadcasted_iota(jnp.int32, (tm, tm), 1)
        tri_ref[...] = (c < r).astype(BF16)

    @pl.when(phase == 0)
    def _():
        cnt_ref[...] += jnp.sum(both, axis=0, keepdims=True)

    @pl.when((phase == 1) & (tile == 0))
    def _():
        cnt = jnp.broadcast_to(cnt_ref[...], (8, ROUTE_COLS))
        padded = jnp.ceil(cnt / bm) * bm
        l8 = lax.broadcasted_iota(jnp.int32, (8, ROUTE_COLS), 1)
        end = padded
        shift = 1
        while shift < N_EXPERTS:
            end = end + jnp.where(l8 >= shift, pltpu.roll(end, shift, axis=1), 0.0)
            shift *= 2
        start_ref[...] = (end - padded)[:1]
        carry_ref[...] = jnp.zeros_like(carry_ref)
        nbp = be_ref.shape[0]
        first = (lax.broadcasted_iota(jnp.int32, (nbp, ROUTE_COLS), 0) * bm).astype(F32)
        lb = lax.broadcasted_iota(jnp.int32, (nbp, ROUTE_COLS), 1)
        hits = jnp.where((lb < N_EXPERTS) & (jnp.broadcast_to(end[:1], (nbp, ROUTE_COLS)) <= first), 1.0, 0.0)
        be = jnp.minimum(jnp.sum(hits, axis=1, keepdims=True), N_EXPERTS - 1.0)
        be_ref[...] = jnp.broadcast_to(be, (nbp, ROUTE_COLS)).astype(jnp.int32)

    @pl.when(phase == 1)
    def _():
        before = jnp.dot(tri_ref[...], both.astype(BF16), preferred_element_type=F32) + carry_ref[...]
        slot = before + start_ref[...]
        dest = [jnp.sum(jnp.where(h, slot, 0.0), axis=1, keepdims=True) for h in hot]
        dest_ref[...] = jnp.where(lane == 0, dest[0], jnp.where(lane == 1, dest[1], 0.0)).astype(jnp.int32)
        carry_ref[...] += jnp.sum(both, axis=0, keepdims=True)


def _route_plan(experts, bm, tm):
    n = experts.shape[0]
    assert n % tm == 0
    n_blocks = -(-n * EXPERT_TOP_K // bm) + N_EXPERTS
    nbp = -(-n_blocks // 8) * 8
    dest, be = pl.pallas_call(
        functools.partial(_route_plan_kernel, bm=bm, n_blocks=n_blocks),
        grid=(2, n // tm),
        in_specs=[pl.BlockSpec((tm, ROUTE_COLS), lambda p, t: (t, 0))],
        out_specs=[pl.BlockSpec((tm, ROUTE_COLS), lambda p, t: (t * p, 0)),
                   pl.BlockSpec((nbp, ROUTE_COLS), lambda p, t: (0, 0))],
        out_shape=[jax.ShapeDtypeStruct((n, ROUTE_COLS), jnp.int32),
                   jax.ShapeDtypeStruct((nbp, ROUTE_COLS), jnp.int32)],
        scratch_shapes=[pltpu.VMEM((1, ROUTE_COLS), F32), pltpu.VMEM((1, ROUTE_COLS), F32),
                        pltpu.VMEM((1, ROUTE_COLS), F32), pltpu.VMEM((tm, tm), BF16)],
        compiler_params=_cparams(("arbitrary", "arbitrary")),
        name="route_plan",
    )(experts)
    return dest[:, :EXPERT_TOP_K].reshape(-1), be[:n_blocks, 0], n_blocks


def _dispatch_kernel(dest_ref, h_ref, xs_in, xs_hbm, sem, *, tok0):
    del xs_in
    tm = h_ref.shape[0]
    base = (tok0 + pl.program_id(0) * tm) * EXPERT_TOP_K

    def copy(r, k):
        return pltpu.make_async_copy(h_ref.at[pl.ds(r, 1), :],
                                     xs_hbm.at[pl.ds(dest_ref[base + EXPERT_TOP_K * r + k], 1), :], sem)

    def start(r, c):
        for k in range(EXPERT_TOP_K):
            copy(r, k).start(priority=k % 2)
        return c

    def wait(r, c):
        for k in range(EXPERT_TOP_K):
            copy(r, k).wait()
        return c

    lax.fori_loop(0, tm, start, 0, unroll=4)
    lax.fori_loop(0, tm, wait, 0, unroll=4)


def _dispatch_rows(dest, h, xs, tok0, tm):
    n, d = h.shape
    assert n % tm == 0
    grid_spec = pltpu.PrefetchScalarGridSpec(
        num_scalar_prefetch=1,
        grid=(n // tm,),
        in_specs=[pl.BlockSpec((tm, d), lambda i, dst: (i, 0)), pl.BlockSpec(memory_space=pl.ANY)],
        out_specs=pl.BlockSpec(memory_space=pl.ANY),
        scratch_shapes=[pltpu.SemaphoreType.DMA(())],
    )
    return pl.pallas_call(
        functools.partial(_dispatch_kernel, tok0=tok0),
        grid_spec=grid_spec,
        out_shape=jax.ShapeDtypeStruct(xs.shape, xs.dtype),
        input_output_aliases={2: 0},
        compiler_params=_cparams(("arbitrary",)),
        name="moe_dispatch",
    )(dest, h, xs)


def _expert_kernel(be_ref, xs_ref, wg_ref, wu_ref, wd_ref, ys_ref, wgb, wub, wdb):
    b = pl.program_id(0)

    @pl.when((b == 0) | (be_ref[b] != be_ref[jnp.maximum(b - 1, 0)]))
    def _():
        wgb[...] = wg_ref[0].astype(BF16)
        wub[...] = wu_ref[0].astype(BF16)
        wdb[...] = wd_ref[0].astype(BF16)

    xb = xs_ref[...].astype(BF16)
    gate = jnp.dot(xb, wgb[...], preferred_element_type=F32)
    up = jnp.dot(xb, wub[...], preferred_element_type=F32)
    act = (jax.nn.silu(gate) * up).astype(BF16)
    ys_ref[...] = jnp.dot(act, wdb[...], preferred_element_type=F32)


def _expert_ffn(block_e, xs, w_gate, w_up, w_down, bm):
    n_slots, d = xs.shape
    ff = w_gate.shape[2]
    grid_spec = pltpu.PrefetchScalarGridSpec(
        num_scalar_prefetch=1,
        grid=(n_slots // bm,),
        in_specs=[pl.BlockSpec((bm, d), lambda b, be: (b, 0)),
                  pl.BlockSpec((1, d, ff), lambda b, be: (be[b], 0, 0)),
                  pl.BlockSpec((1, d, ff), lambda b, be: (be[b], 0, 0)),
                  pl.BlockSpec((1, ff, d), lambda b, be: (be[b], 0, 0))],
        out_specs=pl.BlockSpec((bm, d), lambda b, be: (b, 0)),
        scratch_shapes=[pltpu.VMEM((d, ff), BF16), pltpu.VMEM((d, ff), BF16), pltpu.VMEM((ff, d), BF16)],
    )
    return pl.pallas_call(
        _expert_kernel,
        grid_spec=grid_spec,
        out_shape=jax.ShapeDtypeStruct((n_slots, d), F32),
        compiler_params=_cparams(("arbitrary",)),
        name="expert_ffn",
    )(block_e, xs, w_gate, w_up, w_down)


def _nt(a, b):
    return lax.dot_general(a, b, (((1,), (1,)), ((), ())), preferred_element_type=F32)


def _flash_rows(s, v, m_prev, l_prev, acc_prev, v_transposed=False):
    m_new = jnp.maximum(m_prev, jnp.max(s, axis=-1, keepdims=True))
    alpha = jnp.exp2(m_prev - m_new)
    p = jnp.exp2(s - m_new)
    l_new = alpha * l_prev + jnp.sum(p, axis=-1, keepdims=True)
    pb = p.astype(BF16)
    pv = _nt(pb, v) if v_transposed else jnp.dot(pb, v, preferred_element_type=F32)
    return m_new, l_new, alpha * acc_prev + pv


def _page_copies(pt_ref, first_page, n, src_hbm, dst, sem, slab):
    out = []
    for k in range(n):
        p0 = pl.multiple_of(pt_ref[first_page + k] * slab, slab)
        out.append(pltpu.make_async_copy(src_hbm.at[pl.ds(p0, slab), :], dst.at[pl.ds(k * slab, slab), :], sem))
    return out


DIFF_RING = 3


def _diff_sample_kernel(pt_ref, q_ref, kc_hbm, vc_hbm, kn_ref, vn_ref, lam_ref, g_ref, o_ref,
                        kbuf, vbuf, sem, m_ref, l_ref, acc_ref, *, page, ppc, n_chunks, lam_init):
    b = pl.program_id(0)
    c = pl.program_id(1)
    step = b * n_chunks + c
    total = pl.num_programs(0) * n_chunks
    t = q_ref.shape[1]
    kslab = WB * page // LANES
    vslab = page * DIFF_HEADS

    def copies(s, slot):
        first = s * ppc
        return (_page_copies(pt_ref, first, ppc, kc_hbm, kbuf.at[slot], sem.at[0, slot], kslab)
                + _page_copies(pt_ref, first, ppc, vc_hbm, vbuf.at[slot], sem.at[1, slot], vslab))

    n_ring = kbuf.shape[0]
    ahead = n_ring - 1
    slot = step % n_ring

    @pl.when(step == 0)
    def _():
        for s in range(ahead):
            for cp in copies(s, s):
                cp.start()

    @pl.when(step + ahead < total)
    def _():
        for cp in copies(step + ahead, (step + ahead) % n_ring):
            cp.start()

    @pl.when(c == 0)
    def _():
        m_ref[...] = jnp.full_like(m_ref, NEG)
        l_ref[...] = jnp.zeros_like(l_ref)
        acc_ref[...] = jnp.zeros_like(acc_ref)

    for cp in copies(step, slot):
        cp.wait()

    q = q_ref[0].astype(F32)
    lane = lax.broadcasted_iota(jnp.int32, (t, 2 * HEAD_DIM), 1)

    def q_pair(h):
        qh = q[:, 2 * HEAD_DIM * h:2 * HEAD_DIM * (h + 1)]
        return jnp.concatenate([jnp.where(lane < HEAD_DIM, qh, 0.0), jnp.where(lane >= HEAD_DIM, qh, 0.0)],
                               axis=0).astype(BF16)

    two_hd = 2 * HEAD_DIM
    for h in range(DIFF_HEADS):
        kt = jnp.concatenate([kbuf[slot, kslab * k + two_hd * h:kslab * k + two_hd * (h + 1), :]
                              for k in range(ppc)], axis=1).astype(BF16)
        v2 = vbuf[slot, pl.ds(h, ppc * page, stride=DIFF_HEADS), :].astype(BF16)
        s = jnp.dot(q_pair(h), kt, preferred_element_type=F32)
        m, l, acc = _flash_rows(s, v2, m_ref[h], l_ref[h], acc_ref[h])
        m_ref[h] = m
        l_ref[h] = l
        acc_ref[h] = acc

    @pl.when(c == n_chunks - 1)
    def _():
        lam = _diff_lambda(lam_ref, lam_init)
        trow = lax.broadcasted_iota(jnp.int32, (2 * t, t), 0) % t
        tcol = lax.broadcasted_iota(jnp.int32, (2 * t, t), 1)
        for h in range(DIFF_HEADS):
            kn = kn_ref[0][:, 2 * HEAD_DIM * h:2 * HEAD_DIM * (h + 1)].astype(BF16)
            vn = vn_ref[0][:, DIFF_VDIM * h:DIFF_VDIM * (h + 1)].astype(BF16)
            s = jnp.where(tcol <= trow, _nt(q_pair(h), kn), NEG)
            m, l, acc = _flash_rows(s, vn, m_ref[h], l_ref[h], acc_ref[h])
            o = acc / l
            a = o[:t] - lam * o[t:]
            y = a * lax.rsqrt(jnp.mean(a * a, axis=-1, keepdims=True) + EPS) * g_ref[...] * (1.0 - lam_init)
            o_ref[0, :, DIFF_VDIM * h:DIFF_VDIM * (h + 1)] = y.astype(o_ref.dtype)


def _diff_sample(page_table, q3, kcache, vcache, kn3, vn3, diff_lambda, subln_g, lam_init, page, ppc):
    db, t, _ = q3.shape
    n_pages = page_table.shape[1]
    assert n_pages % ppc == 0 and t == 8
    n_chunks = n_pages // ppc
    tks = ppc * page
    per_b = lambda w: pl.BlockSpec((1, t, w), lambda b, c, pt: (b, 0, 0))
    const = lambda shape: pl.BlockSpec(shape, lambda b, c, pt: (0,) * len(shape))
    grid_spec = pltpu.PrefetchScalarGridSpec(
        num_scalar_prefetch=1,
        grid=(db, n_chunks),
        in_specs=[per_b(WB), pl.BlockSpec(memory_space=pl.ANY), pl.BlockSpec(memory_space=pl.ANY),
                  per_b(WB), per_b(WV), const(diff_lambda.shape), const((1, DIFF_VDIM))],
        out_specs=per_b(WV),
        scratch_shapes=[pltpu.VMEM((DIFF_RING, tks * WB // LANES, LANES), F32),
                        pltpu.VMEM((DIFF_RING, tks * DIFF_HEADS, DIFF_VDIM), F32),
                        pltpu.SemaphoreType.DMA((2, DIFF_RING)),
                        pltpu.VMEM((DIFF_HEADS, 2 * t, 1), F32), pltpu.VMEM((DIFF_HEADS, 2 * t, 1), F32),
                        pltpu.VMEM((DIFF_HEADS, 2 * t, DIFF_VDIM), F32)],
    )
    return pl.pallas_call(
        functools.partial(_diff_sample_kernel, page=page, ppc=ppc, n_chunks=n_chunks, lam_init=lam_init),
        grid_spec=grid_spec,
        out_shape=jax.ShapeDtypeStruct((db, t, WV), BF16),
        compiler_params=_cparams(("arbitrary", "arbitrary")),
        name="diff_sample",
    )(page_table.reshape(-1), q3, kcache, vcache, kn3, vn3, diff_lambda, subln_g.reshape(1, DIFF_VDIM))


def _softmax_rows(s, valid):
    sm = jnp.where(valid, s, NEG)
    m = jnp.max(sm, axis=-1, keepdims=True)
    e = jnp.where(valid, jnp.exp2(sm - m), 0.0)
    return e / jnp.maximum(jnp.sum(e, axis=-1, keepdims=True), 1e-30)


def _select_blocks_rows(score, blk):
    big = jnp.int32(2 ** 30)

    def body(_, carry):
        sc, bias = carry
        mx = jnp.max(sc, axis=-1, keepdims=True)
        first = jnp.min(jnp.where(sc == mx, blk, big), axis=-1, keepdims=True)
        hit = blk == first
        return jnp.where(hit, -3e38, sc), jnp.where(hit, 0.0, bias)

    return lax.fori_loop(0, TOP_N, body, (score, jnp.full(score.shape, NEG, F32)))[1]


def _pad_rows(x, rows):
    return jnp.concatenate([x, jnp.zeros((rows - x.shape[0], x.shape[1]), x.dtype)], axis=0)


TOK_PITCH = CMP_STRIDE + 1


def _nsa_sample_kernel(pt_ref, q_ref, qr_ref, ga_ref, cache_hbm, new_ref, wst_ref, wnew_ref,
                       w1k_ref, w1v_ref, ck_ref, cv_ref, w2k_ref, w2v_ref, impt_ref, exp_ref, o_ref,
                       ring, sring, kctok, vctok, abk_ref, abv_ref, s0_ref, s1_ref, sem, ssem, *, page, n_pages, ppc,
                       past, n_sel):
    b = pl.program_id(0)
    nb = pl.num_programs(0)
    t = q_ref.shape[1]
    g4 = NSA_GROUP
    n_sub = past // CMP_STRIDE
    n_cmp = n_sub - 1
    n_selp = impt_ref.shape[1]
    slab = 4 * KV * page // LANES
    half = slab // 2
    n_chunks = n_pages // ppc
    tks = ppc * page

    def copies(bb, c, second, dst, s):
        first = bb * n_pages + c * ppc
        out = []
        for k in range(ppc):
            p0 = pl.multiple_of(pt_ref[first + k] * slab + half * second, half)
            out.append(pltpu.make_async_copy(cache_hbm.at[pl.ds(p0, half), :], dst.at[pl.ds(k * half, half), :], s))
        return out

    n_ring = ring.shape[0]
    ahead = n_ring - 1

    def cmp_copies(bb, c):
        return copies(bb, c, 0, ring.at[c % n_ring], sem.at[c % n_ring])

    def sel_copies(c):
        return copies(b, c, 1, sring.at[c % 3], ssem.at[c % 3])

    def begin(jj):
        if jj + ahead < n_chunks:
            for cp in cmp_copies(b, jj + ahead):
                cp.start()
        else:
            @pl.when(b + 1 < nb)
            def _():
                for cp in cmp_copies(b + 1, jj + ahead - n_chunks):
                    cp.start()
        for cp in cmp_copies(b, jj):
            cp.wait()
        return jj % n_ring

    @pl.when(b == 0)
    def _():
        for c in range(ahead):
            for cp in cmp_copies(0, c):
                cp.start()

    for c in range(min(2, n_chunks)):
        for cp in sel_copies(c):
            cp.start()

    subs_pc = tks // CMP_STRIDE

    def compress_chunk(jj):
        base = TOK_PITCH * subs_pc * jj
        for x_ref, w_ref, ab_ref in ((kctok, w1k_ref, abk_ref), (vctok, w1v_ref, abv_ref)):
            acc = jnp.zeros((subs_pc, 2 * KV), F32)
            for r in range(0, CMP_STRIDE, 2):
                xr = jnp.concatenate([x_ref[pl.ds(base + r + d, subs_pc, stride=TOK_PITCH), :] for d in range(2)],
                                     axis=1)
                acc += jnp.dot(xr.astype(BF16), w_ref[r // 2], preferred_element_type=F32)
                yield
            ab_ref[jj * subs_pc:(jj + 1) * subs_pc, :] = acc

    prev = iter(())
    for jj in range(n_chunks):
        slot = begin(jj)
        for k in range(ppc):
            r0 = (jj * ppc + k) * page
            for src, dst in ((ring[slot, k * half:k * half + KV, :].T, kctok),
                             (ring[slot, k * half + KV:(k + 1) * half, :].T, vctok)):
                for u in range(page // CMP_STRIDE):
                    row = TOK_PITCH * (r0 // CMP_STRIDE + u)
                    dst[row:row + CMP_STRIDE, :] = src[u * CMP_STRIDE:(u + 1) * CMP_STRIDE]
            next(prev, None)
        for _ in prev:
            pass
        prev = compress_chunk(jj)
    for _ in prev:
        pass
    kcmp = jnp.dot(_compress_hidden(abk_ref[...], ck_ref[...]), w2k_ref[...],
                   preferred_element_type=F32).astype(BF16)
    vcmp = jnp.dot(_compress_hidden(abv_ref[...], cv_ref[...]), w2v_ref[...],
                   preferred_element_type=F32).astype(BF16)

    q = q_ref[0].astype(F32)
    qr = qr_ref[0].astype(F32)
    ga = ga_ref[0]
    lane = lax.broadcasted_iota(jnp.int32, (t, LANES), 1)
    trow = lax.broadcasted_iota(jnp.int32, (g4 * t, 1), 0) % t
    qpos = past + trow
    qpos_t = past + lax.broadcasted_iota(jnp.int32, (t, 1), 0)

    def to_half(x, have, want):
        return x if have == want else pltpu.roll(x, HEAD_DIM, axis=1)

    def q_rows(qq, h):
        keep = (lane >= HEAD_DIM * h) & (lane < HEAD_DIM * (h + 1))
        rows = []
        for g in range(g4):
            hd = g4 * h + g
            tile = to_half(qq[:, LANES * (hd // 2):LANES * (hd // 2 + 1)], hd % 2, h)
            rows.append(jnp.where(keep, tile, 0.0))
        return jnp.concatenate(rows, axis=0).astype(BF16)

    new = new_ref[0]
    ksn = _pad_rows(new[:, 2 * KV:3 * KV], LANES).astype(BF16)
    vsn = _pad_rows(new[:, 3 * KV:4 * KV], LANES).astype(BF16)
    kwst = wst_ref[0][:KV, :].astype(BF16)
    vwst = wst_ref[0][KV:, :].astype(BF16)
    kwn = _pad_rows(wnew_ref[0][:, :KV], LANES).astype(BF16)
    vwn = _pad_rows(wnew_ref[0][:, KV:], LANES).astype(BF16)
    wbuf = wst_ref.shape[2]
    ncol = lax.broadcasted_iota(jnp.int32, (g4 * t, LANES), 1)
    new_ok = (ncol < t) & (ncol <= trow)
    blocks_per_chunk = tks // SEL_BLOCK
    out_tiles = [[None, None] for _ in range(WA // LANES)]
    o_cmps, scores, qrhs = [], [], []
    blk = lax.broadcasted_iota(jnp.int32, (t, n_selp), 1)

    for h in range(NSA_KV_HEADS):
        s = _nt(q_rows(q, h), kcmp)
        nidx = lax.broadcasted_iota(jnp.int32, s.shape, 1)
        p = _softmax_rows(s, (nidx * CMP_STRIDE + (CMP_BLOCK - 1) <= qpos) & (nidx < n_cmp))
        o_cmps.append(jnp.dot(p.astype(BF16), vcmp, preferred_element_type=F32))
        psum = p[:t]
        for g in range(1, g4):
            psum = psum + p[g * t:(g + 1) * t]
        imp = jnp.zeros((t, n_selp), F32)
        for part in _split3(psum):
            imp += jnp.dot(part, impt_ref[...], preferred_element_type=F32)
        cur = qpos_t // SEL_BLOCK
        forced = (blk == 0) | (blk == cur) | (blk == cur - 1)
        score = jnp.where(blk > cur, -1.0, jnp.where(forced, FORCED_SCORE, imp))
        scores.append(jnp.where(blk < n_sel, score, -3e38))
        qrhs.append(q_rows(qr, h))

    nh = NSA_KV_HEADS
    sc = jnp.concatenate(scores + [jnp.full((LANES - nh * t, n_selp), -3e38, F32)], axis=0).T
    n_sel8 = -(-n_sel // 8) * 8
    picked = _select_blocks(sc[:n_sel8], lax.broadcasted_iota(jnp.int32, (n_sel8, LANES), 0))
    selbias = jnp.concatenate([picked, jnp.full((n_selp - n_sel8, LANES), NEG, F32)], axis=0).T[:nh * t]
    qr2 = jnp.concatenate(qrhs, axis=0)
    rows2 = nh * g4 * t

    def by_rows(x):
        return jnp.concatenate([x[h * t:(h + 1) * t] for h in range(nh) for _ in range(g4)], axis=0)

    s_refs = (s0_ref, s1_ref)
    pgs = 4
    m_hist = [jnp.full((rows2, 1), NEG, F32)]
    accs = [jnp.zeros((rows2, LANES + SUM_ROWS), F32)]

    def pages(c, k0, second):
        return jnp.concatenate([sring[c % 3, k * half + KV * second:k * half + KV * (second + 1), :]
                                for k in range(k0, k0 + pgs)], axis=1).astype(BF16)

    def score(c):
        b0 = c * blocks_per_chunk
        tile = selbias[:, LANES * (b0 // LANES):LANES * (b0 // LANES + 1)].astype(BF16)
        bias = by_rows(jnp.dot(tile, exp_ref[(b0 % LANES) // blocks_per_chunk], preferred_element_type=F32))
        m_run = m_hist[-1]
        for k0 in range(0, ppc, pgs):
            cols = slice(k0 * page, (k0 + pgs) * page)
            s = jnp.dot(qr2, pages(c, k0, 0), preferred_element_type=F32) + bias[:, cols]
            s_refs[c % 2][:, cols] = s
            m_run = jnp.maximum(m_run, jnp.max(s, axis=-1, keepdims=True))
            yield
        m_hist.append(m_run)

    def accumulate(c):
        m_new = m_hist[c + 1]
        acc = jnp.exp2(m_hist[c] - m_new) * accs[-1]
        for k0 in range(0, ppc, pgs):
            cols = slice(k0 * page, (k0 + pgs) * page)
            p = jnp.exp2(s_refs[c % 2][:, cols] - m_new).astype(BF16)
            acc = acc + _nt(p, _with_ones(pages(c, k0, 1)))
            yield
        accs.append(acc)

    for cp in sel_copies(0):
        cp.wait()
    for _ in score(0):
        pass
    for c in range(n_chunks):
        if c + 2 < n_chunks:
            for cp in sel_copies(c + 2):
                cp.start()
        nxt = iter(())
        if c + 1 < n_chunks:
            for cp in sel_copies(c + 1):
                cp.wait()
            nxt = score(c + 1)
        for _ in accumulate(c):
            next(nxt, None)
        for _ in nxt:
            pass
    m, acc = m_hist[-1], accs[-1]

    nb_blk = past // SEL_BLOCK
    new_ok2 = jnp.concatenate([new_ok] * nh, axis=0)
    s = jnp.where(new_ok2, _nt(qr2, ksn) + by_rows(selbias[:, nb_blk:nb_blk + 1]), NEG)
    m_new = jnp.maximum(m, jnp.max(s, axis=-1, keepdims=True))
    alpha = jnp.exp2(m - m_new)
    p = jnp.exp2(s - m_new)
    l = alpha * acc[:, LANES:LANES + 1] + jnp.sum(p, axis=-1, keepdims=True)
    o_sel2 = (alpha * acc[:, :LANES] + jnp.dot(p.astype(BF16), vsn, preferred_element_type=F32)) / jnp.maximum(l, 1e-30)

    for h in range(NSA_KV_HEADS):
        qrh, o_cmp = qrhs[h], o_cmps[h]
        o_sel = o_sel2[h * g4 * t:(h + 1) * g4 * t]

        s = jnp.concatenate([jnp.dot(qrh, kwst, preferred_element_type=F32), _nt(qrh, kwn)], axis=1)
        widx = lax.broadcasted_iota(jnp.int32, s.shape, 1)
        kpos = past - wbuf + widx
        valid = (kpos <= qpos) & (kpos >= qpos - WINDOW) & (widx < wbuf + t)
        p = _softmax_rows(s, valid).astype(BF16)
        o_win = _nt(p[:, :wbuf], vwst) + jnp.dot(p[:, wbuf:], vwn, preferred_element_type=F32)

        for g in range(g4):
            hd = g4 * h + g
            r = 3 * hd
            rs = slice(g * t, (g + 1) * t)
            o = ga[:, r:r + 1] * o_cmp[rs] + ga[:, r + 1:r + 2] * o_sel[rs] + ga[:, r + 2:r + 3] * o_win[rs]
            out_tiles[hd // 2][hd % 2] = to_half(o, h, hd % 2)

    for k, (lo, hi) in enumerate(out_tiles):
        o_ref[0, :, LANES * k:LANES * (k + 1)] = jnp.where(lane < HEAD_DIM, lo, hi).astype(o_ref.dtype)


def _nsa_sample(page_table, q3, qr3, ga3, cache, new3, win_state, wnew3, cw, page, ppc):
    db, t, _ = q3.shape
    n_pages = page_table.shape[1]
    past = n_pages * page
    wbuf = win_state.shape[2]
    w1k, w1v, ck, cv, w2k, w2v, _ = cw
    n_sub = past // CMP_STRIDE
    n_sel = -(-(past + t) // SEL_BLOCK)
    n_selp = -(-n_sel // LANES) * LANES
    tks = ppc * page
    blocks_per_chunk = tks // SEL_BLOCK
    assert t == 8 and (past + t) // CMP_STRIDE == n_sub and n_pages % ppc == 0 and past % SEL_BLOCK == 0
    assert LANES % blocks_per_chunk == 0 and wbuf == WINDOW and t <= SEL_BLOCK and page == LANES
    n_ring = 4 if (n_pages // ppc) % 4 == 0 else 2
    assert (n_pages // ppc) % n_ring == 0 and ppc % 4 == 0
    impt = jnp.pad(_importance_matrix(n_sel, n_sub).T, ((0, 0), (0, n_selp - n_sel)))
    m = jnp.arange(LANES // blocks_per_chunk)[:, None, None]
    j = jnp.arange(LANES)[None, :, None]
    u = jnp.arange(tks)[None, None, :]
    expand = (j == blocks_per_chunk * m + u // SEL_BLOCK).astype(BF16)
    per_b = lambda r, w: pl.BlockSpec((1, r, w), lambda b, pt: (b, 0, 0))
    const = lambda a: pl.BlockSpec(a.shape, lambda b, pt: (0,) * a.ndim)
    grid_spec = pltpu.PrefetchScalarGridSpec(
        num_scalar_prefetch=1,
        grid=(db,),
        in_specs=[per_b(t, WA), per_b(t, WA), per_b(t, LANES), pl.BlockSpec(memory_space=pl.ANY),
                  per_b(t, 4 * KV), per_b(2 * KV, wbuf), per_b(t, 2 * KV),
                  const(w1k), const(w1v), const(ck), const(cv), const(w2k), const(w2v), const(impt),
                  const(expand)],
        out_specs=per_b(t, WA),
        scratch_shapes=[pltpu.VMEM((n_ring, ppc * 2 * KV, LANES), F32), pltpu.VMEM((3, ppc * 2 * KV, LANES), F32),
                        pltpu.VMEM(((n_sub + 1) * TOK_PITCH, KV), F32),
                        pltpu.VMEM(((n_sub + 1) * TOK_PITCH, KV), F32),
                        pltpu.VMEM((n_sub, 2 * KV), F32), pltpu.VMEM((n_sub, 2 * KV), F32),
                        pltpu.VMEM((NSA_HEADS * t, tks), F32), pltpu.VMEM((NSA_HEADS * t, tks), F32),
                        pltpu.SemaphoreType.DMA((n_ring,)), pltpu.SemaphoreType.DMA((3,))],
    )
    return pl.pallas_call(
        functools.partial(_nsa_sample_kernel, page=page, n_pages=n_pages, ppc=ppc, past=past, n_sel=n_sel),
        grid_spec=grid_spec,
        out_shape=jax.ShapeDtypeStruct((db, t, WA), BF16),
        compiler_params=_cparams(("arbitrary",)),
        name="nsa_sample",
    )(page_table.reshape(-1), q3, qr3, ga3, cache, new3, win_state, wnew3, w1k, w1v, ck, cv, w2k, w2v, impt,
      expand)


def _combine_kernel(dest_ref, x_ref, w_ref, g_ref, ys_hbm, o_ref, buf, sem, *, tok0):
    i = pl.program_id(0)
    n = pl.num_programs(0)
    tm = x_ref.shape[0]

    def copy(tile, slot, r, k):
        src = dest_ref[(tok0 + tile * tm + r) * EXPERT_TOP_K + k]
        return pltpu.make_async_copy(ys_hbm.at[pl.ds(src, 1), :], buf.at[slot, k, pl.ds(r, 1), :], sem.at[slot])

    def start_tile(tile, slot):
        def body(r, c):
            for k in range(EXPERT_TOP_K):
                copy(tile, slot, r, k).start(priority=k % 2)
            return c
        lax.fori_loop(0, tm, body, 0, unroll=4)

    slot = i % 2

    @pl.when(i == 0)
    def _():
        start_tile(0, 0)

    @pl.when(i + 1 < n)
    def _():
        start_tile(i + 1, 1 - slot)

    def wait(r, c):
        for k in range(EXPERT_TOP_K):
            copy(i, slot, r, k).wait()
        return c

    lax.fori_loop(0, tm, wait, 0, unroll=4)
    x = x_ref[...]
    for k in range(EXPERT_TOP_K):
        x = x + w_ref[:, k:k + 1] * buf[slot, k]
    o_ref[...] = x * lax.rsqrt(jnp.mean(x * x, axis=-1, keepdims=True) + EPS) * g_ref[...]


def _combine_norm(dest, x1, weights, ys, g, tok0, tm):
    n, d = x1.shape
    assert n % tm == 0
    grid_spec = pltpu.PrefetchScalarGridSpec(
        num_scalar_prefetch=1,
        grid=(n // tm,),
        in_specs=[pl.BlockSpec((tm, d), lambda i, dst: (i, 0)), pl.BlockSpec((tm, ROUTE_COLS), lambda i, dst: (i, 0)),
                  pl.BlockSpec((1, d), lambda i, dst: (0, 0)), pl.BlockSpec(memory_space=pl.ANY)],
        out_specs=pl.BlockSpec((tm, d), lambda i, dst: (i, 0)),
        scratch_shapes=[pltpu.VMEM((2, EXPERT_TOP_K, tm, d), F32), pltpu.SemaphoreType.DMA((2,))],
    )
    return pl.pallas_call(
        functools.partial(_combine_kernel, tok0=tok0),
        grid_spec=grid_spec,
        out_shape=jax.ShapeDtypeStruct((n, d), F32),
        compiler_params=_cparams(("arbitrary",)),
        name="moe_combine_norm",
    )(dest, x1, weights, g.reshape(1, d), ys)


MOE_ROWS = 256


def kernel(x_prompt, x_sample, cache_nsa_kv, cache_diff_k, cache_diff_v, state_nsa_win_kv, page_table,
           norm_mix_g, w_in, nsa_cmp_pos, nsa_cmp_k_w1, nsa_cmp_k_w2, nsa_cmp_v_w1, nsa_cmp_v_w2,
           diff_lambda, diff_subln_g, w_proj_a, w_proj_b, w_out, norm_ffn_g,
           router_group_w, router_group_b, router_expert_w, router_expert_b,
           expert_w_gate, expert_w_up, expert_w_down, norm_final_g):
    depth = w_in.shape[0]
    bsz, seq, d = x_prompt.shape
    db, t, _ = x_sample.shape
    n_pool, page = cache_nsa_kv.shape[1:3]
    past = page_table.shape[1] * page
    wbuf = state_nsa_win_kv.shape[2]
    assert depth == 1 and bsz == 1
    l = 0
    lam_init = 0.8 - 0.6 * math.exp(-0.3 * l)
    w = _split_w_in(w_in[l], d)
    cw = _compress_weights(nsa_cmp_pos[l], nsa_cmp_k_w1[l], nsa_cmp_k_w2[l], nsa_cmp_v_w1[l], nsa_cmp_v_w2[l])
    wa, wb, wo = w_proj_a[l].astype(BF16), w_proj_b[l].astype(BF16), w_out[l].astype(BF16)
    rw = jnp.pad(jnp.concatenate([router_group_w[l], router_expert_w[l]], axis=1),
                 ((0, 0), (0, ROUTE_COLS - N_GROUPS - N_EXPERTS)))
    rb = jnp.pad(jnp.concatenate([router_group_b[l], router_expert_b[l]]),
                 (0, ROUTE_COLS - N_GROUPS - N_EXPERTS)).reshape(1, ROUTE_COLS)
    rwh = rw.astype(BF16)
    rwl = (rw - rwh.astype(F32)).astype(BF16)

    xp = x_prompt.reshape(seq, d)
    (nsa_p, win_p, dk_p, dv_p, gm_p, ks_b, kw_b, kb_b,
     qat, qart, qbt, vst, vwt, vbt, gat) = _project(xp, jnp.arange(seq), norm_mix_g[l], w, 512, True)
    kcmp, _, vcmpt = _compress(nsa_p, cw, min(256, seq // CMP_STRIDE))
    oat = _nsa_prompt(qat, qart, gat, kcmp, vcmpt, ks_b, vst, kw_b, vwt, 128, 512)
    obt = _diff_prompt(qbt, kb_b, vbt, diff_lambda[l], diff_subln_g[l], lam_init, 512, 512)
    x1p, h2p, re_p, rw_p = _merge(oat, obt, gm_p, xp, wa, wb, wo, norm_ffn_g[l], rwh, rwl, rb, 512, True)

    ns = db * t
    xs = x_sample.reshape(ns, d)
    pos_s = past + jnp.arange(ns) % t
    (nsa_s, win_s, dk_s, dv_s, gm_s, _, _, _, qa_s, qar_s, qb_s, ga_s) = _project(
        xs, pos_s, norm_mix_g[l], w, ns, False)
    r3 = lambda a: a.reshape(db, t, a.shape[-1])
    slabs = lambda c: c.transpose(0, 2, 3, 4, 1).reshape(-1, page)
    state_t = state_nsa_win_kv[l].transpose(0, 2, 3, 4, 1).reshape(db, 2 * KV, wbuf)
    oa_s = _nsa_sample(page_table, r3(qa_s), r3(qar_s), r3(ga_s), slabs(cache_nsa_kv[l]),
                       r3(nsa_s), state_t, r3(win_s), cw, page, 16)
    ob_s = _diff_sample(page_table, r3(qb_s), slabs(cache_diff_k[l]),
                        cache_diff_v[l].reshape(-1, DIFF_VDIM), r3(dk_s), r3(dv_s),
                        diff_lambda[l], diff_subln_g[l], lam_init, page, 16)
    x1s, h2s, re_s, rw_s = _merge(oa_s.reshape(ns, -1), ob_s.reshape(ns, -1), gm_s, xs, wa, wb, wo,
                                  norm_ffn_g[l], rwh, rwl, rb, ns, False)

    n_all = seq + ns
    plan_tile = max(tm for tm in range(8, 1025, 8) if n_all % tm == 0)
    dest, block_e, n_blocks = _route_plan(jnp.concatenate([re_p, re_s], axis=0), MOE_ROWS, plan_tile)
    xs = jnp.zeros((n_blocks * MOE_ROWS, d), F32)
    xs = _dispatch_rows(dest, h2p, xs, 0, 256)
    xs = _dispatch_rows(dest, h2s, xs, seq, ns)
    ys = _expert_ffn(block_e, xs, expert_w_gate[l], expert_w_up[l], expert_w_down[l], MOE_ROWS)
    y_prompt = _combine_norm(dest, x1p, rw_p, ys, norm_final_g, 0, 256)
    y_sample = _combine_norm(dest, x1s, rw_s, ys, norm_final_g, seq, ns)

    wn = min(WINDOW, seq)
    win_all_t = jnp.concatenate([state_t, r3(win_s).transpose(0, 2, 1)], axis=2)[:, :, -min(WINDOW, past + t):]
    win_all = win_all_t.reshape(db, 2, NSA_KV_HEADS, HEAD_DIM, -1).transpose(0, 4, 1, 2, 3)
    kvs = (4, NSA_KV_HEADS, HEAD_DIM)
    dks = (DIFF_HEADS, 2, HEAD_DIM)
    dvs = (DIFF_HEADS, DIFF_VDIM)
    return (y_prompt.reshape(1, seq, d), y_sample.reshape(db, t, d),
            nsa_p.reshape((1, 1, seq) + kvs), nsa_s.reshape((1, db, t) + kvs),
            dk_p.reshape((1, 1, seq) + dks), dk_s.reshape((1, db, t) + dks),
            dv_p.reshape((1, 1, seq) + dvs), dv_s.reshape((1, db, t) + dvs),
            win_p[seq - wn:].reshape(1, 1, wn, 2, NSA_KV_HEADS, HEAD_DIM),
            win_all[None])
```
